```python
import jax
import jax.numpy as jnp
from jax import lax
import numpy as np

D_MODEL = 1024
BATCH = 32
SEQ = 2048
DEPTH = 2
DEC_BATCH = 4
DEC_SEQ = 8192
PAST_LEN = 128

N_MIXERS = 2
N_ATTN_LAYERS = (DEPTH + 1) // 2
N_CONV_LAYERS = DEPTH // 2
HEAD_DIM = 64
HEADS_PER_GROUP = 4
DILATED_GROUPS = ((128, 1), (512, 4), (2048, 16))
N_HEADS = HEADS_PER_GROUP * len(DILATED_GROUPS)
ATTN_WIDTH = N_HEADS * HEAD_DIM
ROPE_DIM = HEAD_DIM // 4
ROPE_THETA = 500000.0
CONV_WIDTH = 31
N_EXPERT_GROUPS = 4
EXPERTS_PER_GROUP = 8
N_EXPERTS = N_EXPERT_GROUPS * EXPERTS_PER_GROUP
TOP_K = 2
D_EXPERT = 512
MOE_BLOCK = 256
EPS = 1e-6
NEG_INF = -1e30

kernel_name = 'hybrid_dilated_attn_conformer_hmoe_encoder'


def rms_norm(x, g):
    xf = x.astype(jnp.float32)
    y = xf * lax.rsqrt(jnp.mean(xf * xf, axis=-1, keepdims=True) + EPS)
    return (y * g.astype(jnp.float32)).astype(x.dtype)


def layer_norm(x, g, b):
    xf = x.astype(jnp.float32)
    mu = jnp.mean(xf, axis=-1, keepdims=True)
    xc = xf - mu
    var = jnp.mean(xc * xc, axis=-1, keepdims=True)
    return (xc * lax.rsqrt(var + EPS) * g.astype(jnp.float32) + b.astype(jnp.float32)).astype(x.dtype)


def partial_rope(x, seq_len):
    pos = jnp.arange(seq_len, dtype=jnp.float32)
    inv_freq = ROPE_THETA ** (-jnp.arange(0, ROPE_DIM, 2, dtype=jnp.float32) / ROPE_DIM)
    ang = pos[:, None] * inv_freq[None, :]
    cos = jnp.cos(ang)[None, :, None, :]
    sin = jnp.sin(ang)[None, :, None, :]
    xr = x[..., :ROPE_DIM].astype(jnp.float32)
    x1 = xr[..., :ROPE_DIM // 2]
    x2 = xr[..., ROPE_DIM // 2:]
    rot = jnp.concatenate([x1 * cos - x2 * sin, x2 * cos + x1 * sin], axis=-1).astype(x.dtype)
    return jnp.concatenate([rot, x[..., ROPE_DIM:]], axis=-1)


def banded_attention(q, k, v, half):
    n, seq_len, h, dh = q.shape
    nb = -(-seq_len // half)
    lp = nb * half
    pad = lp - seq_len
    qp = jnp.pad(q, ((0, 0), (0, pad), (0, 0), (0, 0)))
    kp = jnp.pad(k, ((0, 0), (half, pad + half), (0, 0), (0, 0)))
    vp = jnp.pad(v, ((0, 0), (half, pad + half), (0, 0), (0, 0)))
    qb = qp.reshape(n, nb, half, h, dh)

    def key_blocks(t):
        return jnp.concatenate([t[:, o * half:o * half + lp].reshape(n, nb, half, h, dh) for o in range(3)], axis=2)

    kb = key_blocks(kp)
    vb = key_blocks(vp)
    rel = jnp.arange(3 * half)[None, :] - half - jnp.arange(half)[:, None]
    kpos = jnp.arange(nb)[:, None] * half - half + jnp.arange(3 * half)[None, :]
    mask = (jnp.abs(rel) <= half)[None, :, :] & ((kpos >= 0) & (kpos < seq_len))[:, None, :]
    scores = jnp.einsum('nbqhd,nbkhd->nbhqk', qb, kb, preferred_element_type=jnp.float32) * (HEAD_DIM ** -0.5)
    scores = jnp.where(mask[None, :, None, :, :], scores, NEG_INF)
    m = jnp.max(scores, axis=-1, keepdims=True)
    p = jnp.exp(scores - m)
    den = jnp.sum(p, axis=-1, keepdims=True)
    lse = (m + jnp.log(den))[..., 0]
    out = jnp.einsum('nbhqk,nbkhd->nbqhd', (p / den).astype(v.dtype), vb)
    out = out.reshape(n, lp, h, dh)[:, :seq_len]
    lse = lse.transpose(0, 1, 3, 2).reshape(n, lp, h)[:, :seq_len]
    return out, lse


def dilated_attention(h, w_qkv, q_gain, k_gain, w_o):
    b, s, _ = h.shape
    qkv = h @ w_qkv
    q, k, v = jnp.split(qkv, 3, axis=-1)
    q = q.reshape(b, s, N_HEADS, HEAD_DIM)
    k = k.reshape(b, s, N_HEADS, HEAD_DIM)
    v = v.reshape(b, s, N_HEADS, HEAD_DIM)
    q = partial_rope(rms_norm(q, q_gain), s)
    k = partial_rope(rms_norm(k, k_gain), s)
    outs = []
    lses = []
    for g, (window, dil) in enumerate(DILATED_GROUPS):
        lo = g * HEADS_PER_GROUP
        hi = lo + HEADS_PER_GROUP
        ls = s // dil

        def to_strided(t):
            return t[:, :, lo:hi].reshape(b, ls, dil, HEADS_PER_GROUP, HEAD_DIM).transpose(0, 2, 1, 3, 4).reshape(b * dil, ls, HEADS_PER_GROUP, HEAD_DIM)

        o, lse = banded_attention(to_strided(q), to_strided(k), to_strided(v), window // (2 * dil))
        o = o.reshape(b, dil, ls, HEADS_PER_GROUP, HEAD_DIM).transpose(0, 2, 1, 3, 4).reshape(b, s, HEADS_PER_GROUP, HEAD_DIM)
        lse = lse.reshape(b, dil, ls, HEADS_PER_GROUP).transpose(0, 2, 1, 3).reshape(b, s, HEADS_PER_GROUP)
        outs.append(o)
        lses.append(lse)
    alpha = jax.nn.softmax(jnp.stack(lses, axis=0), axis=0)
    o = jnp.stack(outs, axis=0) * alpha[..., None].astype(h.dtype)
    o = o.transpose(1, 2, 0, 3, 4).reshape(b, s, ATTN_WIDTH)
    return o @ w_o


def conformer_conv_module(h, w_pw1, b_pw1, w_dw, b_dw, ln_g, ln_b, w_pw2, b_pw2):
    a = h @ w_pw1 + b_pw1
    u = a[..., :D_MODEL] * jax.nn.sigmoid(a[..., D_MODEL:])
    u = lax.conv_general_dilated(u, w_dw[:, None, :], window_strides=(1,), padding=((CONV_WIDTH // 2, CONV_WIDTH // 2),), dimension_numbers=('NWC', 'WIO', 'NWC'), feature_group_count=D_MODEL) + b_dw
    u = jax.nn.silu(layer_norm(u, ln_g, ln_b))
    return u @ w_pw2 + b_pw2


def hierarchical_moe(h, w_rg, b_rg, w_re, b_re, w_gate, w_up, w_down):
    b, s, d = h.shape
    x = h.reshape(-1, d)
    n = x.shape[0]
    rows = jnp.arange(n)
    g_logits = (x @ w_rg).astype(jnp.float32) + b_rg.astype(jnp.float32)
    grp = jnp.argmax(g_logits, axis=-1)
    p_grp = jax.nn.softmax(g_logits, axis=-1)[rows, grp]
    e_logits = ((x @ w_re).astype(jnp.float32) + b_re.astype(jnp.float32)).reshape(n, N_EXPERT_GROUPS, EXPERTS_PER_GROUP)
    e_in = e_logits[rows, grp]
    top_v, top_i = lax.top_k(e_in, TOP_K)
    gates = jax.nn.softmax(top_v, axis=-1) * p_grp[:, None]
    expert = (grp[:, None] * EXPERTS_PER_GROUP + top_i).astype(jnp.int32)
    a_total = n * TOP_K
    n_blocks = -(-a_total // MOE_BLOCK) + N_EXPERTS
    p_total = n_blocks * MOE_BLOCK
    flat_e = expert.reshape(-1)
    flat_tok = jnp.repeat(jnp.arange(n, dtype=jnp.int32), TOP_K)
    flat_gate = gates.reshape(-1)
    order = jnp.argsort(flat_e)
    e_sorted = flat_e[order]
    counts = jnp.bincount(flat_e, length=N_EXPERTS)
    padded = ((counts + MOE_BLOCK - 1) // MOE_BLOCK) * MOE_BLOCK
    pad_end = jnp.cumsum(padded)
    pad_start = pad_end - padded
    cnt_start = jnp.cumsum(counts) - counts
    slot = pad_start[e_sorted] + (jnp.arange(a_total) - cnt_start[e_sorted])
    slot_tok = jnp.zeros((p_total,), jnp.int32).at[slot].set(flat_tok[order])
    slot_gate = jnp.zeros((p_total,), h.dtype).at[slot].set(flat_gate[order].astype(h.dtype))
    block_expert = jnp.minimum(jnp.searchsorted(pad_end, jnp.arange(n_blocks) * MOE_BLOCK, side='right'), N_EXPERTS - 1)

    def run_block(args):
        tok, e = args
        xb = x[tok]
        hid = jax.nn.silu(xb @ w_gate[e]) * (xb @ w_up[e])
        return hid @ w_down[e]

    y_blocks = lax.map(run_block, (slot_tok.reshape(n_blocks, MOE_BLOCK), block_expert))
    y = jax.ops.segment_sum(y_blocks.reshape(p_total, d) * slot_gate[:, None], slot_tok, num_segments=n)
    return y.reshape(b, s, d)


def encoder_trunk(x, ln_mix, ln_ffn, w_qkv, q_gain, k_gain, w_o, w_pw1, b_pw1, w_dw, b_dw, conv_ln_g, conv_ln_b, w_pw2, b_pw2, w_router_group, b_router_group, w_router_expert, b_router_expert, w_gate, w_up, w_down):
    for i in range(DEPTH):
        h = rms_norm(x, ln_mix[i])
        j = i // N_MIXERS
        if i % N_MIXERS == 0:
            x = x + dilated_attention(h, w_qkv[j], q_gain[j], k_gain[j], w_o[j])
        else:
            x = x + conformer_conv_module(h, w_pw1[j], b_pw1[j], w_dw[j], b_dw[j], conv_ln_g[j], conv_ln_b[j], w_pw2[j], b_pw2[j])
        h = rms_norm(x, ln_ffn[i])
        x = x + hierarchical_moe(h, w_router_group[i], b_router_group[i], w_router_expert[i], b_router_expert[i], w_gate[i], w_up[i], w_down[i])
    return x


def setup_inputs(seed: int = 0) -> dict:
    key = jax.random.key(seed)
    ks = jax.random.split(key, 24)
    d = D_MODEL

    def nrm(k, shape, scale):
        return jax.random.normal(k, shape, jnp.float32) * scale

    return {
        'x_prompt': nrm(ks[0], (BATCH, SEQ, d), 1.0),
        'x_sample': nrm(ks[1], (DEC_BATCH, DEC_SEQ, d), 1.0),
        'ln_mix': 1.0 + nrm(ks[2], (DEPTH, d), 0.02),
        'ln_ffn': 1.0 + nrm(ks[3], (DEPTH, d), 0.02),
        'w_qkv': nrm(ks[4], (N_ATTN_LAYERS, d, 3 * ATTN_WIDTH), d ** -0.5),
        'q_gain': 1.0 + nrm(ks[5], (N_ATTN_LAYERS, HEAD_DIM), 0.02),
        'k_gain': 1.0 + nrm(ks[6], (N_ATTN_LAYERS, HEAD_DIM), 0.02),
        'w_o': nrm(ks[7], (N_ATTN_LAYERS, ATTN_WIDTH, d), ATTN_WIDTH ** -0.5),
        'w_pw1': nrm(ks[8], (N_CONV_LAYERS, d, 2 * d), d ** -0.5),
        'b_pw1': nrm(ks[9], (N_CONV_LAYERS, 2 * d), 0.02),
        'w_dw': nrm(ks[10], (N_CONV_LAYERS, CONV_WIDTH, d), CONV_WIDTH ** -0.5),
        'b_dw': nrm(ks[11], (N_CONV_LAYERS, d), 0.02),
        'conv_ln_g': 1.0 + nrm(ks[12], (N_CONV_LAYERS, d), 0.02),
        'conv_ln_b': nrm(ks[13], (N_CONV_LAYERS, d), 0.02),
        'w_pw2': nrm(ks[14], (N_CONV_LAYERS, d, d), d ** -0.5),
        'b_pw2': nrm(ks[15], (N_CONV_LAYERS, d), 0.02),
        'w_router_group': nrm(ks[16], (DEPTH, d, N_EXPERT_GROUPS), d ** -0.5),
        'b_router_group': nrm(ks[17], (DEPTH, N_EXPERT_GROUPS), 0.01),
        'w_router_expert': nrm(ks[18], (DEPTH, d, N_EXPERTS), d ** -0.5),
        'b_router_expert': nrm(ks[19], (DEPTH, N_EXPERTS), 0.01),
        'w_gate': nrm(ks[20], (DEPTH, N_EXPERTS, d, D_EXPERT), d ** -0.5),
        'w_up': nrm(ks[21], (DEPTH, N_EXPERTS, d, D_EXPERT), d ** -0.5),
        'w_down': nrm(ks[22], (DEPTH, N_EXPERTS, D_EXPERT, d), D_EXPERT ** -0.5),
    }


def reference(x_prompt, x_sample, ln_mix, ln_ffn, w_qkv, q_gain, k_gain, w_o, w_pw1, b_pw1, w_dw, b_dw, conv_ln_g, conv_ln_b, w_pw2, b_pw2, w_router_group, b_router_group, w_router_expert, b_router_expert, w_gate, w_up, w_down):
    y_prompt = encoder_trunk(x_prompt, ln_mix, ln_ffn, w_qkv, q_gain, k_gain, w_o, w_pw1, b_pw1, w_dw, b_dw, conv_ln_g, conv_ln_b, w_pw2, b_pw2, w_router_group, b_router_group, w_router_expert, b_router_expert, w_gate, w_up, w_down)
    y_sample = encoder_trunk(x_sample, ln_mix, ln_ffn, w_qkv, q_gain, k_gain, w_o, w_pw1, b_pw1, w_dw, b_dw, conv_ln_g, conv_ln_b, w_pw2, b_pw2, w_router_group, b_router_group, w_router_expert, b_router_expert, w_gate, w_up, w_down)
    return (y_prompt, y_sample)
```

```python
import functools

import jax
import jax.numpy as jnp
import numpy as np
from jax import lax
from jax.experimental import pallas as pl
from jax.experimental.pallas import tpu as pltpu

D_MODEL = 1024
HEAD_DIM = 64
HEADS_PER_GROUP = 4
DILATIONS = (1, 4, 16)
HALF_WINDOW = 64
N_GROUPS = len(DILATIONS)
GROUP_WIDTH = HEADS_PER_GROUP * HEAD_DIM
ATTN_WIDTH = N_GROUPS * GROUP_WIDTH
ROPE_DIM = HEAD_DIM // 4
ROPE_THETA = 500000.0
CONV_WIDTH = 31
CONV_HALO = 16
N_EXPERT_GROUPS = 4
EXPERTS_PER_GROUP = 8
N_EXPERTS = N_EXPERT_GROUPS * EXPERTS_PER_GROUP
TOP_K = 2
D_EXPERT = 512
EPS = 1e-6
NEG_INF = -1e30

LANES = 128
ROW_TILE = 512
ATTN_Q_TILE = 128
MOE_BLOCK = 256
VMEM_LIMIT = 56 * 1024 * 1024

F32 = jnp.float32
BF16 = jnp.bfloat16


def _cparams(sem):
    return pltpu.CompilerParams(dimension_semantics=sem, vmem_limit_bytes=VMEM_LIMIT)


def _rms(x, gain):
    return x * lax.rsqrt(jnp.mean(x * x, axis=-1, keepdims=True) + EPS) * gain


def _qkv_kernel(x_ref, g_ref, w_ref, mseg_ref, qg_ref, kg_ref, cos_ref, sa_ref, sb_ref,
                q0, q1, q2, k0, k1, k2, v0, v1, v2):
    h = _rms(x_ref[...], g_ref[...]).astype(BF16)
    cos = cos_ref[...]
    sa = sa_ref[...]
    sb = sb_ref[...]

    def head_norm_rope(t, gain, outs):
        ms = jnp.dot((t * t).astype(BF16), mseg_ref[...], preferred_element_type=F32)
        tn = t * lax.rsqrt(ms + EPS) * gain
        for g in range(N_GROUPS):
            c = tn[:, g * GROUP_WIDTH:(g + 1) * GROUP_WIDTH]
            r = (c * cos + pltpu.roll(c, GROUP_WIDTH - ROPE_DIM // 2, 1) * sa
                 + pltpu.roll(c, ROPE_DIM // 2, 1) * sb)
            outs[g][...] = r.astype(BF16)

    q = jnp.dot(h, w_ref[:, 0:ATTN_WIDTH], preferred_element_type=F32)
    head_norm_rope(q, qg_ref[...], (q0, q1, q2))
    k = jnp.dot(h, w_ref[:, ATTN_WIDTH:2 * ATTN_WIDTH], preferred_element_type=F32)
    head_norm_rope(k, kg_ref[...], (k0, k1, k2))
    v = jnp.dot(h, w_ref[:, 2 * ATTN_WIDTH:3 * ATTN_WIDTH], preferred_element_type=F32)
    for g, o in enumerate((v0, v1, v2)):
        o[...] = v[:, g * GROUP_WIDTH:(g + 1) * GROUP_WIDTH].astype(BF16)


def _qkv_call(x2d, seq, gain, w_qkv, mseg, qg, kg, cos, sa, sb):
    n = x2d.shape[0]
    tm = ROW_TILE
    tiles_per_seq = seq // tm
    row = lambda i: (i, 0)
    const = lambda i: (0, 0)
    tab = lambda i: (i % tiles_per_seq, 0)
    out_sds = jax.ShapeDtypeStruct((n, GROUP_WIDTH), BF16)
    return pl.pallas_call(
        _qkv_kernel,
        grid=(n // tm,),
        in_specs=[
            pl.BlockSpec((tm, D_MODEL), row),
            pl.BlockSpec((1, D_MODEL), const),
            pl.BlockSpec((D_MODEL, 3 * ATTN_WIDTH), const),
            pl.BlockSpec((ATTN_WIDTH, ATTN_WIDTH), const),
            pl.BlockSpec((1, ATTN_WIDTH), const),
            pl.BlockSpec((1, ATTN_WIDTH), const),
            pl.BlockSpec((tm, GROUP_WIDTH), tab),
            pl.BlockSpec((tm, GROUP_WIDTH), tab),
            pl.BlockSpec((tm, GROUP_WIDTH), tab),
        ],
        out_specs=[pl.BlockSpec((tm, GROUP_WIDTH), row)] * 9,
        out_shape=[out_sds] * 9,
        compiler_params=_cparams(("parallel",)),
        name="qkv_proj",
    )(x2d, gain, w_qkv, mseg, qg, kg, cos, sa, sb)


def _attn_kernel(q_ref, k_ref, v_ref, o_ref, lse_ref, *, ls, tq, win):
    i = pl.program_id(2)
    q_start = i * tq
    k_start = jnp.clip(q_start - HALF_WINDOW, 0, ls - win)
    k_start = pl.multiple_of(k_start, HALF_WINDOW)
    kw = k_ref[0, pl.ds(k_start, win), :]
    vw = v_ref[0, pl.ds(k_start, win), :]
    qpos = q_start + lax.broadcasted_iota(jnp.int32, (tq, win), 0)
    kpos = k_start + lax.broadcasted_iota(jnp.int32, (tq, win), 1)
    mask = jnp.abs(qpos - kpos) <= HALF_WINDOW
    lane = lax.broadcasted_iota(jnp.int32, (tq, LANES), 1)
    lse_tile = jnp.zeros((tq, LANES), F32)
    for h in range(HEADS_PER_GROUP):
        cols = slice(h * HEAD_DIM, (h + 1) * HEAD_DIM)
        s = lax.dot_general(q_ref[0, :, cols], kw[:, cols], (((1,), (1,)), ((), ())),
                            preferred_element_type=F32)
        s = jnp.where(mask, s, NEG_INF)
        m = jnp.max(s, axis=-1, keepdims=True)
        p = jnp.exp(s - m)
        den = jnp.sum(p, axis=-1, keepdims=True)
        o = jnp.dot(p.astype(BF16), vw[:, cols], preferred_element_type=F32) / den
        o_ref[0, :, cols] = o.astype(BF16)
        lse_tile = jnp.where(lane == h, m + jnp.log(den), lse_tile)
    lse_ref[0] = lse_tile


def _attn_call(q, k, v, batch, seq, dil):
    ls = seq // dil
    tq = min(ATTN_Q_TILE, ls)
    win = min(tq + 2 * HALF_WINDOW, ls)
    view = lambda t: t.reshape(batch, ls, dil * GROUP_WIDTH)
    qmap = lambda b, r, i: (b, i, r)
    kvmap = lambda b, r, i: (b, 0, r)
    o, lse = pl.pallas_call(
        functools.partial(_attn_kernel, ls=ls, tq=tq, win=win),
        grid=(batch, dil, ls // tq),
        in_specs=[
            pl.BlockSpec((1, tq, GROUP_WIDTH), qmap),
            pl.BlockSpec((1, ls, GROUP_WIDTH), kvmap),
            pl.BlockSpec((1, ls, GROUP_WIDTH), kvmap),
        ],
        out_specs=[
            pl.BlockSpec((1, tq, GROUP_WIDTH), qmap),
            pl.BlockSpec((1, tq, LANES), qmap),
        ],
        out_shape=[
            jax.ShapeDtypeStruct((batch, ls, dil * GROUP_WIDTH), BF16),
            jax.ShapeDtypeStruct((batch, ls, dil * LANES), F32),
        ],
        compiler_params=_cparams(("parallel", "parallel", "arbitrary")),
        name=f"banded_attn_d{dil}",
    )(view(q), view(k), view(v))
    return o.reshape(batch * seq, GROUP_WIDTH), lse.reshape(batch * seq, LANES)


def _route(x, gain, wr_ref, br_ref, ri_ref, rg_ref):
    h = _rms(x, gain)
    h_hi = h.astype(BF16)
    h_lo = (h - h_hi.astype(F32)).astype(BF16)
    acc = (jnp.dot(h_hi, wr_ref[...], preferred_element_type=F32)
           + jnp.dot(h_lo, wr_ref[...], preferred_element_type=F32))
    logits = acc[:, :LANES] + acc[:, LANES:] + br_ref[...]
    lane = lax.broadcasted_iota(jnp.int32, logits.shape, 1)
    lane_f = lane.astype(F32)
    big = jnp.float32(2 * LANES)

    is_g = lane < N_EXPERT_GROUPS
    gl = jnp.where(is_g, logits, NEG_INF)
    gmax = jnp.max(gl, axis=-1, keepdims=True)
    grp = jnp.min(jnp.where(gl == gmax, lane_f, big), axis=-1, keepdims=True)
    p_grp = 1.0 / jnp.sum(jnp.where(is_g, jnp.exp(gl - gmax), 0.0), axis=-1, keepdims=True)

    lo = N_EXPERT_GROUPS + grp * EXPERTS_PER_GROUP
    in_grp = (lane_f >= lo) & (lane_f < lo + EXPERTS_PER_GROUP)
    el = jnp.where(in_grp, logits, NEG_INF)
    v1 = jnp.max(el, axis=-1, keepdims=True)
    i1 = jnp.min(jnp.where(el == v1, lane_f, big), axis=-1, keepdims=True)
    el2 = jnp.where(lane_f == i1, NEG_INF, el)
    v2 = jnp.max(el2, axis=-1, keepdims=True)
    i2 = jnp.min(jnp.where(el2 == v2, lane_f, big), axis=-1, keepdims=True)
    e21 = jnp.exp(v2 - v1)
    g1 = p_grp / (1.0 + e21)
    g2 = g1 * e21
    ri = jnp.where(lane == 0, i1 - N_EXPERT_GROUPS, jnp.where(lane == 1, i2 - N_EXPERT_GROUPS, 0.0))
    ri_ref[...] = ri.astype(jnp.int32)
    rg_ref[...] = jnp.where(lane == 0, g1, jnp.where(lane == 1, g2, 0.0))


def _attn_out_kernel(o0, o1, o2, l0, l1, l2, x_ref, wo_ref, ex_ref, g_ref, wr_ref, br_ref,
                     y_ref, ri_ref, rg_ref):
    ls = (l0[...], l1[...], l2[...])
    m = jnp.maximum(jnp.maximum(ls[0], ls[1]), ls[2])
    es = [jnp.exp(l - m) for l in ls]
    inv = 1.0 / (es[0] + es[1] + es[2])
    y = x_ref[...]
    for g, o_ref in enumerate((o0, o1, o2)):
        alpha = es[g] * inv
        a_hi = alpha.astype(BF16)
        a_lo = (alpha - a_hi.astype(F32)).astype(BF16)
        a_wide = (jnp.dot(a_hi, ex_ref[...], preferred_element_type=F32)
                  + jnp.dot(a_lo, ex_ref[...], preferred_element_type=F32))
        scaled = (o_ref[...].astype(F32) * a_wide).astype(BF16)
        y = y + jnp.dot(scaled, wo_ref[g * GROUP_WIDTH:(g + 1) * GROUP_WIDTH, :],
                        preferred_element_type=F32)
    y_ref[...] = y
    _route(y, g_ref[...], wr_ref, br_ref, ri_ref, rg_ref)


def _attn_out_call(os_, lses, x2d, w_o, expand, gain, wr, br):
    n = x2d.shape[0]
    tm = ROW_TILE
    row = lambda i: (i, 0)
    const = lambda i: (0, 0)
    return pl.pallas_call(
        _attn_out_kernel,
        grid=(n // tm,),
        in_specs=(
            [pl.BlockSpec((tm, GROUP_WIDTH), row)] * 3
            + [pl.BlockSpec((tm, LANES), row)] * 3
            + [
                pl.BlockSpec((tm, D_MODEL), row),
                pl.BlockSpec((ATTN_WIDTH, D_MODEL), const),
                pl.BlockSpec((LANES, GROUP_WIDTH), const),
                pl.BlockSpec((1, D_MODEL), const),
                pl.BlockSpec((D_MODEL, 2 * LANES), const),
                pl.BlockSpec((1, LANES), const),
            ]
        ),
        out_specs=[
            pl.BlockSpec((tm, D_MODEL), row),
            pl.BlockSpec((tm, LANES), row),
            pl.BlockSpec((tm, LANES), row),
        ],
        out_shape=[
            jax.ShapeDtypeStruct((n, D_MODEL), F32),
            jax.ShapeDtypeStruct((n, LANES), jnp.int32),
            jax.ShapeDtypeStruct((n, LANES), F32),
        ],
        compiler_params=_cparams(("parallel",)),
        name="attn_out_router",
    )(*os_, *lses, x2d, w_o, expand, gain, wr, br)


def _conv_kernel(xc_ref, xp_ref, xn_ref, gm_ref, w1_ref, b1_ref, wdw_ref, bdw_ref, lng_ref, lnb_ref,
                 w2_ref, b2_ref, gf_ref, wr_ref, br_ref, y_ref, ri_ref, rg_ref, x_scr, u_scr, c_scr,
                 *, seq, tm):
    i = pl.program_id(0)
    pos0 = (i * tm) % seq
    hl = CONV_HALO
    x_scr[0:hl, :] = xp_ref[...]
    x_scr[hl:hl + tm, :] = xc_ref[...]
    x_scr[hl + tm:hl + tm + hl, :] = xn_ref[...]
    h = _rms(x_scr[...], gm_ref[...]).astype(BF16)
    a = jnp.dot(h, w1_ref[...], preferred_element_type=F32) + b1_ref[...]
    u = a[:, :D_MODEL] * jax.nn.sigmoid(a[:, D_MODEL:])
    r = lax.broadcasted_iota(jnp.int32, (tm + 2 * hl, 1), 0)
    outside = ((r < hl) & (pos0 == 0)) | ((r >= hl + tm) & (pos0 + tm == seq))
    u_scr[...] = jnp.where(outside, 0.0, u)

    rc = 64
    lc = 256
    for j in range(tm // rc):
        for c in range(D_MODEL // lc):
            cols = slice(c * lc, (c + 1) * lc)
            acc = jnp.zeros((rc, lc), F32) + bdw_ref[:, cols]
            for t in range(CONV_WIDTH):
                off = j * rc + t + hl - CONV_WIDTH // 2
                acc = acc + u_scr[off:off + rc, cols] * wdw_ref[t:t + 1, cols]
            c_scr[j * rc:(j + 1) * rc, cols] = acc
    cv = c_scr[...]
    mu = jnp.mean(cv, axis=-1, keepdims=True)
    xc = cv - mu
    var = jnp.mean(xc * xc, axis=-1, keepdims=True)
    ln = xc * lax.rsqrt(var + EPS) * lng_ref[...] + lnb_ref[...]
    act = (ln * jax.nn.sigmoid(ln)).astype(BF16)
    y = xc_ref[...] + jnp.dot(act, w2_ref[...], preferred_element_type=F32) + b2_ref[...]
    y_ref[...] = y
    _route(y, gf_ref[...], wr_ref, br_ref, ri_ref, rg_ref)


def _conv_call(x2d, seq, gm, w1, b1, wdw, bdw, lng, lnb, w2, b2, gf, wr, br):
    n = x2d.shape[0]
    tm = ROW_TILE
    hl = CONV_HALO
    per = tm // hl
    last = n // hl - 1
    row = lambda i: (i, 0)
    const = lambda i: (0, 0)
    prev = lambda i: (jnp.maximum(i * per - 1, 0), 0)
    nxt = lambda i: (jnp.minimum((i + 1) * per, last), 0)
    return pl.pallas_call(
        functools.partial(_conv_kernel, seq=seq, tm=tm),
        grid=(n // tm,),
        in_specs=[
            pl.BlockSpec((tm, D_MODEL), row),
            pl.BlockSpec((hl, D_MODEL), prev),
            pl.BlockSpec((hl, D_MODEL), nxt),
            pl.BlockSpec((1, D_MODEL), const),
            pl.BlockSpec((D_MODEL, 2 * D_MODEL), const),
            pl.BlockSpec((1, 2 * D_MODEL), const),
            pl.BlockSpec((CONV_WIDTH, D_MODEL), const),
            pl.BlockSpec((1, D_MODEL), const),
            pl.BlockSpec((1, D_MODEL), const),
            pl.BlockSpec((1, D_MODEL), const),
            pl.BlockSpec((D_MODEL, D_MODEL), const),
            pl.BlockSpec((1, D_MODEL), const),
            pl.BlockSpec((1, D_MODEL), const),
            pl.BlockSpec((D_MODEL, 2 * LANES), const),
            pl.BlockSpec((1, LANES), const),
        ],
        out_specs=[
            pl.BlockSpec((tm, D_MODEL), row),
            pl.BlockSpec((tm, LANES), row),
            pl.BlockSpec((tm, LANES), row),
        ],
        out_shape=[
            jax.ShapeDtypeStruct((n, D_MODEL), F32),
            jax.ShapeDtypeStruct((n, LANES), jnp.int32),
            jax.ShapeDtypeStruct((n, LANES), F32),
        ],
        scratch_shapes=[
            pltpu.VMEM((tm + 2 * hl, D_MODEL), F32),
            pltpu.VMEM((tm + 2 * hl, D_MODEL), F32),
            pltpu.VMEM((tm, D_MODEL), F32),
        ],
        compiler_params=_cparams(("parallel",)),
        name="conv_module_router",
    )(x2d, x2d, x2d, gm, w1, b1, wdw, bdw, lng, lnb, w2, b2, gf, wr, br)


def _slot_index(idx_smem, k, t):
    flat = k * ROW_TILE + t
    return idx_smem[flat // LANES, flat % LANES]


def _dispatch_kernel(y_ref, g_ref, idx_hbm, xs_in, xs_out, h_scr, idx_smem, idx_sem, row_sem):
    del xs_in
    i = pl.program_id(0)
    idx_cp = pltpu.make_async_copy(idx_hbm.at[i], idx_smem, idx_sem)
    idx_cp.start()
    h_scr[...] = _rms(y_ref[...], g_ref[...])
    idx_cp.wait()

    def issue(t, carry):
        for k in range(TOP_K):
            s = _slot_index(idx_smem, k, t)
            pltpu.make_async_copy(h_scr.at[pl.ds(t, 1)], xs_out.at[pl.ds(s, 1)], row_sem).start()
        return carry

    lax.fori_loop(0, ROW_TILE, issue, 0)
    for _ in range(TOP_K):
        pltpu.make_async_copy(h_scr, xs_out.at[pl.ds(0, ROW_TILE)], row_sem).wait()


def _dispatch_call(y2d, gain, idx, p_total):
    n = y2d.shape[0]
    tm = ROW_TILE
    xs0 = jnp.zeros((p_total, D_MODEL), F32)
    return pl.pallas_call(
        _dispatch_kernel,
        grid=(n // tm,),
        in_specs=[
            pl.BlockSpec((tm, D_MODEL), lambda i: (i, 0)),
            pl.BlockSpec((1, D_MODEL), lambda i: (0, 0)),
            pl.BlockSpec(memory_space=pl.ANY),
            pl.BlockSpec(memory_space=pl.ANY),
        ],
        out_specs=pl.BlockSpec(memory_space=pl.ANY),
        out_shape=jax.ShapeDtypeStruct((p_total, D_MODEL), F32),
        scratch_shapes=[
            pltpu.VMEM((tm, D_MODEL), F32),
            pltpu.SMEM((TOP_K * tm // LANES, LANES), jnp.int32),
            pltpu.SemaphoreType.DMA,
            pltpu.SemaphoreType.DMA,
        ],
        input_output_aliases={3: 0},
        compiler_params=_cparams(("arbitrary",)),
        name="moe_dispatch",
    )(y2d, gain, idx, xs0)


def _expert_kernel(be_ref, xs_ref, wg_ref, wu_ref, wd_ref, yb_ref):
    del be_ref
    xb = xs_ref[...].astype(BF16)
    g = jnp.dot(xb, wg_ref[0], preferred_element_type=F32)
    u = jnp.dot(xb, wu_ref[0], preferred_element_type=F32)
    hid = (g * jax.nn.sigmoid(g) * u).astype(BF16)
    yb_ref[...] = jnp.dot(hid, wd_ref[0], preferred_element_type=F32)


def _expert_call(block_expert, xs, wg, wu, wd):
    p_total = xs.shape[0]
    blk = MOE_BLOCK
    wmap = lambda i, be: (be[i], 0, 0)
    return pl.pallas_call(
        _expert_kernel,
        grid_spec=pltpu.PrefetchScalarGridSpec(
            num_scalar_prefetch=1,
            grid=(p_total // blk,),
            in_specs=[
                pl.BlockSpec((blk, D_MODEL), lambda i, be: (i, 0)),
                pl.BlockSpec((1, D_MODEL, D_EXPERT), wmap),
                pl.BlockSpec((1, D_MODEL, D_EXPERT), wmap),
                pl.BlockSpec((1, D_EXPERT, D_MODEL), wmap),
            ],
            out_specs=pl.BlockSpec((blk, D_MODEL), lambda i, be: (i, 0)),
        ),
        out_shape=jax.ShapeDtypeStruct((p_total, D_MODEL), F32),
        compiler_params=_cparams(("arbitrary",)),
        name="moe_experts",
    )(block_expert, xs, wg, wu, wd)


def _combine_kernel(y_ref, rg_ref, idx_hbm, yb_hbm, out_ref, buf, idx_smem, idx_sem, row_sem):
    i = pl.program_id(0)
    idx_cp = pltpu.make_async_copy(idx_hbm.at[i], idx_smem, idx_sem)
    idx_cp.start()
    idx_cp.wait()

    def issue(t, carry):
        for k in range(TOP_K):
            s = _slot_index(idx_smem, k, t)
            pltpu.make_async_copy(yb_hbm.at[pl.ds(s, 1)], buf.at[k, pl.ds(t, 1)], row_sem).start()
        return carry

    lax.fori_loop(0, ROW_TILE, issue, 0)
    for k in range(TOP_K):
        pltpu.make_async_copy(yb_hbm.at[pl.ds(0, ROW_TILE)], buf.at[k], row_sem).wait()
    rg = rg_ref[...]
    out_ref[...] = y_ref[...] + rg[:, 0:1] * buf[0] + rg[:, 1:2] * buf[1]


def _combine_call(y2d, rg, idx, yb):
    n = y2d.shape[0]
    tm = ROW_TILE
    return pl.pallas_call(
        _combine_kernel,
        grid=(n // tm,),
        in_specs=[
            pl.BlockSpec((tm, D_MODEL), lambda i: (i, 0)),
            pl.BlockSpec((tm, LANES), lambda i: (i, 0)),
            pl.BlockSpec(memory_space=pl.ANY),
            pl.BlockSpec(memory_space=pl.ANY),
        ],
        out_specs=pl.BlockSpec((tm, D_MODEL), lambda i: (i, 0)),
        out_shape=jax.ShapeDtypeStruct((n, D_MODEL), F32),
        scratch_shapes=[
            pltpu.VMEM((TOP_K, tm, D_MODEL), F32),
            pltpu.SMEM((TOP_K * tm // LANES, LANES), jnp.int32),
            pltpu.SemaphoreType.DMA,
            pltpu.SemaphoreType.DMA,
        ],
        compiler_params=_cparams(("arbitrary",)),
        name="moe_combine",
    )(y2d, rg, idx, yb)


def _slot_tables(ri, n):
    e = ri[:, :TOP_K]
    onehot = (e[:, :, None] == jnp.arange(N_EXPERTS, dtype=jnp.int32)).astype(jnp.int32)
    tok_cnt = onehot.sum(axis=1)
    t = 256
    cnt_t = tok_cnt.reshape(n // t, t, N_EXPERTS)
    tri = (jnp.arange(t)[:, None] > jnp.arange(t)[None, :]).astype(F32)
    within = jnp.einsum("ts,nsc->ntc", tri, cnt_t.astype(F32)).astype(jnp.int32)
    tile_sum = cnt_t.sum(axis=1)
    tile_base = jnp.cumsum(tile_sum, axis=0) - tile_sum
    rank = (within + tile_base[:, None, :]).reshape(n, N_EXPERTS)
    counts = tile_sum.sum(axis=0)
    padded = ((counts + MOE_BLOCK - 1) // MOE_BLOCK) * MOE_BLOCK
    pad_end = jnp.cumsum(padded)
    pad_start = pad_end - padded
    slot = ((rank + pad_start)[:, None, :] * onehot).sum(axis=-1)
    n_blocks = (n * TOP_K) // MOE_BLOCK + N_EXPERTS
    block_expert = jnp.minimum(
        jnp.searchsorted(pad_end, jnp.arange(n_blocks, dtype=jnp.int32) * MOE_BLOCK, side="right"),
        N_EXPERTS - 1).astype(jnp.int32)
    tm = ROW_TILE
    idx = slot.reshape(n // tm, tm, TOP_K).transpose(0, 2, 1).reshape(n // tm, TOP_K * tm // LANES, LANES)
    return idx.astype(jnp.int32), block_expert, n_blocks * MOE_BLOCK


def _moe(y2d, ri, rg, gain, wg, wu, wd):
    n = y2d.shape[0]
    idx, block_expert, p_total = _slot_tables(ri, n)
    xs = _dispatch_call(y2d, gain, idx, p_total)
    yb = _expert_call(block_expert, xs, wg, wu, wd)
    return _combine_call(y2d, rg, idx, yb)


def _rope_tables(seq):
    pos = jnp.arange(seq, dtype=F32)
    inv_freq = ROPE_THETA ** (-jnp.arange(0, ROPE_DIM, 2, dtype=F32) / ROPE_DIM)
    ang = pos[:, None] * inv_freq[None, :]
    cos = jnp.cos(ang)
    sin = jnp.sin(ang)
    half = ROPE_DIM // 2
    rest = HEAD_DIM - ROPE_DIM
    cos_h = jnp.concatenate([cos, cos, jnp.ones((seq, rest), F32)], axis=1)
    sa_h = jnp.concatenate([-sin, jnp.zeros((seq, half + rest), F32)], axis=1)
    sb_h = jnp.concatenate([jnp.zeros((seq, half), F32), sin, jnp.zeros((seq, rest), F32)], axis=1)
    tile = lambda t: jnp.tile(t, (1, HEADS_PER_GROUP))
    return tile(cos_h), tile(sa_h), tile(sb_h)


def _router_weights(w_rg, b_rg, w_re, b_re):
    w = jnp.concatenate([w_rg, w_re], axis=1)
    w = jnp.pad(w, ((0, 0), (0, LANES - w.shape[1])))
    w_hi = w.astype(BF16)
    w_lo = (w - w_hi.astype(F32)).astype(BF16)
    b = jnp.pad(jnp.concatenate([b_rg, b_re]), (0, LANES - N_EXPERT_GROUPS - N_EXPERTS))
    return jnp.concatenate([w_hi, w_lo], axis=1), b.reshape(1, LANES)


def _trunk(x, p):
    batch, seq, d = x.shape
    n = batch * seq
    x2d = x.reshape(n, d)
    cos, sa, sb = _rope_tables(seq)

    qkv = _qkv_call(x2d, seq, p["ln_mix0"], p["w_qkv"], p["mseg"], p["qg"], p["kg"], cos, sa, sb)
    os_, lses = [], []
    for g, dil in enumerate(DILATIONS):
        o, lse = _attn_call(qkv[g], qkv[3 + g], qkv[6 + g], batch, seq, dil)
        os_.append(o)
        lses.append(lse)
    y, ri, rg = _attn_out_call(os_, lses, x2d, p["w_o"], p["expand"], p["ln_ffn0"], p["wr0"], p["br0"])
    x2d = _moe(y, ri, rg, p["ln_ffn0"], p["wg0"], p["wu0"], p["wd0"])

    y, ri, rg = _conv_call(x2d, seq, p["ln_mix1"], p["w_pw1"], p["b_pw1"], p["w_dw"], p["b_dw"],
                           p["conv_ln_g"], p["conv_ln_b"], p["w_pw2"], p["b_pw2"],
                           p["ln_ffn1"], p["wr1"], p["br1"])
    x2d = _moe(y, ri, rg, p["ln_ffn1"], p["wg1"], p["wu1"], p["wd1"])
    return x2d.reshape(batch, seq, d)


def kernel(x_prompt, x_sample, ln_mix, ln_ffn, w_qkv, q_gain, k_gain, w_o, w_pw1, b_pw1, w_dw, b_dw,
           conv_ln_g, conv_ln_b, w_pw2, b_pw2, w_router_group, b_router_group, w_router_expert,
           b_router_expert, w_gate, w_up, w_down):
    row = lambda v: v.reshape(1, -1).astype(F32)
    n_heads = N_GROUPS * HEADS_PER_GROUP
    head_id = np.arange(ATTN_WIDTH) // HEAD_DIM
    mseg = jnp.asarray((head_id[:, None] == head_id[None, :]) / HEAD_DIM, BF16)
    slot_id = np.arange(GROUP_WIDTH) // HEAD_DIM
    expand = jnp.asarray(np.arange(LANES)[:, None] == slot_id[None, :], BF16)
    wr0, br0 = _router_weights(w_router_group[0], b_router_group[0], w_router_expert[0], b_router_expert[0])
    wr1, br1 = _router_weights(w_router_group[1], b_router_group[1], w_router_expert[1], b_router_expert[1])
    p = dict(
        ln_mix0=row(ln_mix[0]), ln_mix1=row(ln_mix[1]), ln_ffn0=row(ln_ffn[0]), ln_ffn1=row(ln_ffn[1]),
        w_qkv=w_qkv[0].astype(BF16), mseg=mseg, expand=expand,
        qg=row(jnp.tile(q_gain[0], n_heads) * HEAD_DIM ** -0.5), kg=row(jnp.tile(k_gain[0], n_heads)),
        w_o=w_o[0].astype(BF16),
        w_pw1=w_pw1[0].astype(BF16), b_pw1=row(b_pw1[0]), w_dw=w_dw[0].astype(F32), b_dw=row(b_dw[0]),
        conv_ln_g=row(conv_ln_g[0]), conv_ln_b=row(conv_ln_b[0]),
        w_pw2=w_pw2[0].astype(BF16), b_pw2=row(b_pw2[0]),
        wr0=wr0, br0=br0, wr1=wr1, br1=br1,
        wg0=w_gate[0].astype(BF16), wu0=w_up[0].astype(BF16), wd0=w_down[0].astype(BF16),
        wg1=w_gate[1].astype(BF16), wu1=w_up[1].astype(BF16), wd1=w_down[1].astype(BF16),
    )
    return _trunk(x_prompt, p), _trunk(x_sample, p)
```

```python
import functools

import jax
import jax.numpy as jnp
import numpy as np
from jax import lax
from jax.experimental import pallas as pl
from jax.experimental.pallas import tpu as pltpu

D_MODEL = 1024
HEAD_DIM = 64
HEADS_PER_GROUP = 4
DILATIONS = (1, 4, 16)
HALF_WINDOW = 64
N_GROUPS = len(DILATIONS)
GROUP_WIDTH = HEADS_PER_GROUP * HEAD_DIM
ATTN_WIDTH = N_GROUPS * GROUP_WIDTH
ROPE_DIM = HEAD_DIM // 4
ROPE_THETA = 500000.0
CONV_WIDTH = 31
CONV_HALO = 16
N_EXPERT_GROUPS = 4
EXPERTS_PER_GROUP = 8
N_EXPERTS = N_EXPERT_GROUPS * EXPERTS_PER_GROUP
TOP_K = 2
D_EXPERT = 512
EPS = 1e-6
NEG_INF = -1e30

LANES = 128
SUBLANES = 8
ROW_TILE = 512
ATTN_Q_TILE = 128
MOE_BLOCK = 256
VMEM_LIMIT = 56 * 1024 * 1024

F32 = jnp.float32
BF16 = jnp.bfloat16


def _cparams(sem):
    return pltpu.CompilerParams(dimension_semantics=sem, vmem_limit_bytes=VMEM_LIMIT)


def _rms(x, gain):
    return x * lax.rsqrt(jnp.mean(x * x, axis=-1, keepdims=True) + EPS) * gain


def _store_by_residue(out_ref, val, dil, scr):
    if dil == 1:
        out_ref[...] = val.astype(out_ref.dtype)
        return
    rows, width = val.shape
    for c in range(width // LANES):
        scr[c] = val[:, c * LANES:(c + 1) * LANES]
    for r in range(dil):
        for c in range(width // LANES):
            lanes = slice(r * width + c * LANES, r * width + (c + 1) * LANES)
            out_ref[:, lanes] = scr[c, pl.ds(r, rows // dil, stride=dil), :].astype(out_ref.dtype)


def _qkv_kernel(x_ref, g_ref, w_ref, mseg_ref, qg_ref, kg_ref, cos_ref, sa_ref, sb_ref,
                q0, q1, q2, k0, k1, k2, v0, v1, v2, scr):
    h = _rms(x_ref[...], g_ref[...]).astype(BF16)
    cos = cos_ref[...]
    sa = sa_ref[...]
    sb = sb_ref[...]

    def head_norm_rope(t, gain, outs):
        ms = jnp.dot((t * t).astype(BF16), mseg_ref[...], preferred_element_type=F32)
        tn = t * lax.rsqrt(ms + EPS) * gain
        for g in range(N_GROUPS):
            c = tn[:, g * GROUP_WIDTH:(g + 1) * GROUP_WIDTH]
            r = (c * cos + pltpu.roll(c, GROUP_WIDTH - ROPE_DIM // 2, 1) * sa
                 + pltpu.roll(c, ROPE_DIM // 2, 1) * sb)
            _store_by_residue(outs[g], r, DILATIONS[g], scr)

    q = jnp.dot(h, w_ref[:, 0:ATTN_WIDTH], preferred_element_type=F32)
    head_norm_rope(q, qg_ref[...], (q0, q1, q2))
    k = jnp.dot(h, w_ref[:, ATTN_WIDTH:2 * ATTN_WIDTH], preferred_element_type=F32)
    head_norm_rope(k, kg_ref[...], (k0, k1, k2))
    v = jnp.dot(h, w_ref[:, 2 * ATTN_WIDTH:3 * ATTN_WIDTH], preferred_element_type=F32)
    for g, o in enumerate((v0, v1, v2)):
        _store_by_residue(o, v[:, g * GROUP_WIDTH:(g + 1) * GROUP_WIDTH], DILATIONS[g], scr)


def _qkv_call(x2d, seq, gain, w_qkv, mseg, qg, kg, cos, sa, sb):
    n = x2d.shape[0]
    tm = ROW_TILE
    tiles_per_seq = seq // tm
    row = lambda i: (i, 0)
    const = lambda i: (0, 0)
    tab = lambda i: (i % tiles_per_seq, 0)
    out_specs = [pl.BlockSpec((tm // dil, dil * GROUP_WIDTH), row) for dil in DILATIONS] * 3
    out_shape = [jax.ShapeDtypeStruct((n // dil, dil * GROUP_WIDTH), BF16) for dil in DILATIONS] * 3
    return pl.pallas_call(
        _qkv_kernel,
        grid=(n // tm,),
        in_specs=[
            pl.BlockSpec((tm, D_MODEL), row),
            pl.BlockSpec((1, D_MODEL), const),
            pl.BlockSpec((D_MODEL, 3 * ATTN_WIDTH), const),
            pl.BlockSpec((ATTN_WIDTH, ATTN_WIDTH), const),
            pl.BlockSpec((1, ATTN_WIDTH), const),
            pl.BlockSpec((1, ATTN_WIDTH), const),
            pl.BlockSpec((tm, GROUP_WIDTH), tab),
            pl.BlockSpec((tm, GROUP_WIDTH), tab),
            pl.BlockSpec((tm, GROUP_WIDTH), tab),
        ],
        out_specs=out_specs,
        out_shape=out_shape,
        scratch_shapes=[pltpu.VMEM((GROUP_WIDTH // LANES, tm, LANES), F32)],
        compiler_params=_cparams(("parallel",)),
        name="qkv_proj",
    )(x2d, gain, w_qkv, mseg, qg, kg, cos, sa, sb)


def _attn_kernel(q_ref, k_ref, v_ref, o_ref, lse_ref, *, ls, tq, win):
    i = pl.program_id(2)
    q_start = i * tq
    k_start = jnp.clip(q_start - HALF_WINDOW, 0, ls - win)
    k_start = pl.multiple_of(k_start, HALF_WINDOW)
    kw = k_ref[0, pl.ds(k_start, win), :]
    vw = v_ref[0, pl.ds(k_start, win), :]
    qpos = q_start + lax.broadcasted_iota(jnp.int32, (tq, win), 0)
    kpos = k_start + lax.broadcasted_iota(jnp.int32, (tq, win), 1)
    mask = jnp.abs(qpos - kpos) <= HALF_WINDOW
    lane = lax.broadcasted_iota(jnp.int32, (tq, LANES), 1)
    lse_tile = jnp.zeros((tq, LANES), F32)
    for h in range(HEADS_PER_GROUP):
        cols = slice(h * HEAD_DIM, (h + 1) * HEAD_DIM)
        s = lax.dot_general(q_ref[0, :, cols], kw[:, cols], (((1,), (1,)), ((), ())),
                            preferred_element_type=F32)
        s = jnp.where(mask, s, NEG_INF)
        m = jnp.max(s, axis=-1, keepdims=True)
        p = jnp.exp(s - m)
        den = jnp.sum(p, axis=-1, keepdims=True)
        o = jnp.dot(p.astype(BF16), vw[:, cols], preferred_element_type=F32) / den
        o_ref[0, :, cols] = o.astype(BF16)
        lse_tile = jnp.where(lane == h, m + jnp.log(den), lse_tile)
    lse_ref[0] = lse_tile


def _attn_call(q, k, v, batch, seq, dil):
    ls = seq // dil
    tq = min(ATTN_Q_TILE, ls)
    win = min(tq + 2 * HALF_WINDOW, ls)
    view = lambda t: t.reshape(batch, ls, dil * GROUP_WIDTH)
    qmap = lambda b, r, i: (b, i, r)
    kvmap = lambda b, r, i: (b, 0, r)
    o, lse = pl.pallas_call(
        functools.partial(_attn_kernel, ls=ls, tq=tq, win=win),
        grid=(batch, dil, ls // tq),
        in_specs=[
            pl.BlockSpec((1, tq, GROUP_WIDTH), qmap),
            pl.BlockSpec((1, ls, GROUP_WIDTH), kvmap),
            pl.BlockSpec((1, ls, GROUP_WIDTH), kvmap),
        ],
        out_specs=[
            pl.BlockSpec((1, tq, GROUP_WIDTH), qmap),
            pl.BlockSpec((1, tq, LANES), qmap),
        ],
        out_shape=[
            jax.ShapeDtypeStruct((batch, ls, dil * GROUP_WIDTH), BF16),
            jax.ShapeDtypeStruct((batch, ls, dil * LANES), F32),
        ],
        compiler_params=_cparams(("parallel", "parallel", "arbitrary")),
        name=f"banded_attn_d{dil}",
    )(view(q), view(k), view(v))
    return o.reshape(batch * ls, dil * GROUP_WIDTH), lse.reshape(batch * ls, dil * LANES)


def _route(x, gain, wr_ref, br_ref, ri_ref, rg_ref):
    h = _rms(x, gain)
    h_hi = h.astype(BF16)
    h_lo = (h - h_hi.astype(F32)).astype(BF16)
    acc = (jnp.dot(h_hi, wr_ref[...], preferred_element_type=F32)
           + jnp.dot(h_lo, wr_ref[...], preferred_element_type=F32))
    logits = acc[:, :LANES] + acc[:, LANES:] + br_ref[...]
    lane = lax.broadcasted_iota(jnp.int32, logits.shape, 1)
    lane_f = lane.astype(F32)
    big = jnp.float32(2 * LANES)

    is_g = lane < N_EXPERT_GROUPS
    gl = jnp.where(is_g, logits, NEG_INF)
    gmax = jnp.max(gl, axis=-1, keepdims=True)
    grp = jnp.min(jnp.where(gl == gmax, lane_f, big), axis=-1, keepdims=True)
    p_grp = 1.0 / jnp.sum(jnp.where(is_g, jnp.exp(gl - gmax), 0.0), axis=-1, keepdims=True)

    lo = N_EXPERT_GROUPS + grp * EXPERTS_PER_GROUP
    in_grp = (lane_f >= lo) & (lane_f < lo + EXPERTS_PER_GROUP)
    el = jnp.where(in_grp, logits, NEG_INF)
    v1 = jnp.max(el, axis=-1, keepdims=True)
    i1 = jnp.min(jnp.where(el == v1, lane_f, big), axis=-1, keepdims=True)
    el2 = jnp.where(lane_f == i1, NEG_INF, el)
    v2 = jnp.max(el2, axis=-1, keepdims=True)
    i2 = jnp.min(jnp.where(el2 == v2, lane_f, big), axis=-1, keepdims=True)
    e21 = jnp.exp(v2 - v1)
    g1 = p_grp / (1.0 + e21)
    g2 = g1 * e21
    ri = jnp.where(lane == 0, i1 - N_EXPERT_GROUPS, jnp.where(lane == 1, i2 - N_EXPERT_GROUPS, 0.0))
    ri_ref[...] = ri.astype(jnp.int32)
    rg_ref[...] = jnp.where(lane == 0, g1, jnp.where(lane == 1, g2, 0.0))


def _load_by_residue(in_ref, dil, scr):
    if dil == 1:
        return in_ref[...].astype(F32)
    chunks, rows, _ = scr.shape
    width = chunks * LANES
    for r in range(dil):
        for c in range(chunks):
            lanes = slice(r * width + c * LANES, r * width + (c + 1) * LANES)
            scr[c, pl.ds(r, rows // dil, stride=dil), :] = in_ref[:, lanes].astype(F32)
    return jnp.concatenate([scr[c] for c in range(chunks)], axis=1) if chunks > 1 else scr[0]


def _attn_out_kernel(o0, o1, o2, l0, l1, l2, x_ref, wo_ref, ex_ref, g_ref, wr_ref, br_ref,
                     y_ref, ri_ref, rg_ref, o_scr1, o_scr2, l_scr1, l_scr2):
    ls = [_load_by_residue(l, dil, scr)
          for l, dil, scr in zip((l0, l1, l2), DILATIONS, (None, l_scr1, l_scr2))]
    m = jnp.maximum(jnp.maximum(ls[0], ls[1]), ls[2])
    es = [jnp.exp(l - m) for l in ls]
    inv = 1.0 / (es[0] + es[1] + es[2])
    y = x_ref[...]
    for g, (o_ref, scr) in enumerate(zip((o0, o1, o2), (None, o_scr1, o_scr2))):
        alpha = es[g] * inv
        a_hi = alpha.astype(BF16)
        a_lo = (alpha - a_hi.astype(F32)).astype(BF16)
        a_wide = (jnp.dot(a_hi, ex_ref[...], preferred_element_type=F32)
                  + jnp.dot(a_lo, ex_ref[...], preferred_element_type=F32))
        scaled = (_load_by_residue(o_ref, DILATIONS[g], scr) * a_wide).astype(BF16)
        y = y + jnp.dot(scaled, wo_ref[g * GROUP_WIDTH:(g + 1) * GROUP_WIDTH, :],
                        preferred_element_type=F32)
    y_ref[...] = y
    _route(y, g_ref[...], wr_ref, br_ref, ri_ref, rg_ref)


def _attn_out_call(os_, lses, x2d, w_o, expand, gain, wr, br):
    n = x2d.shape[0]
    tm = ROW_TILE
    row = lambda i: (i, 0)
    const = lambda i: (0, 0)
    return pl.pallas_call(
        _attn_out_kernel,
        grid=(n // tm,),
        in_specs=(
            [pl.BlockSpec((tm // dil, dil * GROUP_WIDTH), row) for dil in DILATIONS]
            + [pl.BlockSpec((tm // dil, dil * LANES), row) for dil in DILATIONS]
            + [
                pl.BlockSpec((tm, D_MODEL), row),
                pl.BlockSpec((ATTN_WIDTH, D_MODEL), const),
                pl.BlockSpec((LANES, GROUP_WIDTH), const),
                pl.BlockSpec((1, D_MODEL), const),
                pl.BlockSpec((D_MODEL, 2 * LANES), const),
                pl.BlockSpec((1, LANES), const),
            ]
        ),
        out_specs=[
            pl.BlockSpec((tm, D_MODEL), row),
            pl.BlockSpec((tm, LANES), row),
            pl.BlockSpec((tm, LANES), row),
        ],
        out_shape=[
            jax.ShapeDtypeStruct((n, D_MODEL), F32),
            jax.ShapeDtypeStruct((n, LANES), jnp.int32),
            jax.ShapeDtypeStruct((n, LANES), F32),
        ],
        scratch_shapes=([pltpu.VMEM((GROUP_WIDTH // LANES, tm, LANES), F32)] * 2
                        + [pltpu.VMEM((1, tm, LANES), F32)] * 2),
        compiler_params=_cparams(("parallel",)),
        name="attn_out_router",
    )(*os_, *lses, x2d, w_o, expand, gain, wr, br)


def _conv_kernel(xc_ref, xp_ref, xn_ref, gm_ref, w1_ref, b1_ref, wdw_ref, bdw_ref, lng_ref, lnb_ref,
                 w2_ref, b2_ref, gf_ref, wr_ref, br_ref, y_ref, ri_ref, rg_ref, x_scr, u_scr, c_scr,
                 *, seq, tm):
    i = pl.program_id(0)
    pos0 = (i * tm) % seq
    hl = CONV_HALO
    x_scr[0:hl, :] = xp_ref[...]
    x_scr[hl:hl + tm, :] = xc_ref[...]
    x_scr[hl + tm:hl + tm + hl, :] = xn_ref[...]
    h = _rms(x_scr[...], gm_ref[...]).astype(BF16)
    a = jnp.dot(h, w1_ref[...], preferred_element_type=F32) + b1_ref[...]
    u = a[:, :D_MODEL] * jax.nn.sigmoid(a[:, D_MODEL:])
    r = lax.broadcasted_iota(jnp.int32, (tm + 2 * hl, 1), 0)
    outside = ((r < hl) & (pos0 == 0)) | ((r >= hl + tm) & (pos0 + tm == seq))
    u_scr[...] = jnp.where(outside, 0.0, u)

    rc = 64
    lc = LANES
    first = hl - CONV_WIDTH // 2
    for j in range(tm // rc):
        for c in range(D_MODEL // lc):
            cols = slice(c * lc, (c + 1) * lc)
            acc = jnp.zeros((rc, lc), F32) + bdw_ref[:, cols]
            for b in range(SUBLANES):
                part = None
                for t in range(CONV_WIDTH):
                    if (t + first) % SUBLANES != b:
                        continue
                    off = j * rc + (t + first) - b
                    term = u_scr[off:off + rc + SUBLANES, cols] * wdw_ref[t:t + 1, cols]
                    part = term if part is None else part + term
                acc = acc + part[b:b + rc]
            c_scr[j * rc:(j + 1) * rc, cols] = acc
    cv = c_scr[...]
    mu = jnp.mean(cv, axis=-1, keepdims=True)
    xc = cv - mu
    var = jnp.mean(xc * xc, axis=-1, keepdims=True)
    ln = xc * lax.rsqrt(var + EPS) * lng_ref[...] + lnb_ref[...]
    act = (ln * jax.nn.sigmoid(ln)).astype(BF16)
    y = xc_ref[...] + jnp.dot(act, w2_ref[...], preferred_element_type=F32) + b2_ref[...]
    y_ref[...] = y
    _route(y, gf_ref[...], wr_ref, br_ref, ri_ref, rg_ref)


def _conv_call(x2d, seq, gm, w1, b1, wdw, bdw, lng, lnb, w2, b2, gf, wr, br):
    n = x2d.shape[0]
    tm = ROW_TILE
    hl = CONV_HALO
    per = tm // hl
    last = n // hl - 1
    row = lambda i: (i, 0)
    const = lambda i: (0, 0)
    prev = lambda i: (jnp.maximum(i * per - 1, 0), 0)
    nxt = lambda i: (jnp.minimum((i + 1) * per, last), 0)
    return pl.pallas_call(
        functools.partial(_conv_kernel, seq=seq, tm=tm),
        grid=(n // tm,),
        in_specs=[
            pl.BlockSpec((tm, D_MODEL), row),
            pl.BlockSpec((hl, D_MODEL), prev),
            pl.BlockSpec((hl, D_MODEL), nxt),
            pl.BlockSpec((1, D_MODEL), const),
            pl.BlockSpec((D_MODEL, 2 * D_MODEL), const),
            pl.BlockSpec((1, 2 * D_MODEL), const),
            pl.BlockSpec((CONV_WIDTH, D_MODEL), const),
            pl.BlockSpec((1, D_MODEL), const),
            pl.BlockSpec((1, D_MODEL), const),
            pl.BlockSpec((1, D_MODEL), const),
            pl.BlockSpec((D_MODEL, D_MODEL), const),
            pl.BlockSpec((1, D_MODEL), const),
            pl.BlockSpec((1, D_MODEL), const),
            pl.BlockSpec((D_MODEL, 2 * LANES), const),
            pl.BlockSpec((1, LANES), const),
        ],
        out_specs=[
            pl.BlockSpec((tm, D_MODEL), row),
            pl.BlockSpec((tm, LANES), row),
            pl.BlockSpec((tm, LANES), row),
        ],
        out_shape=[
            jax.ShapeDtypeStruct((n, D_MODEL), F32),
            jax.ShapeDtypeStruct((n, LANES), jnp.int32),
            jax.ShapeDtypeStruct((n, LANES), F32),
        ],
        scratch_shapes=[
            pltpu.VMEM((tm + 2 * hl, D_MODEL), F32),
            pltpu.VMEM((tm + 2 * hl, D_MODEL), F32),
            pltpu.VMEM((tm, D_MODEL), F32),
        ],
        compiler_params=_cparams(("parallel",)),
        name="conv_module_router",
    )(x2d, x2d, x2d, gm, w1, b1, wdw, bdw, lng, lnb, w2, b2, gf, wr, br)


def _pack_rows(x):
    bits = pltpu.bitcast(x.astype(BF16).astype(F32), jnp.uint32)
    half = x.shape[1] // 2
    return bits[:, :half] | (bits[:, half:] >> 16)


def _unpack_rows(w):
    hi = pltpu.bitcast(w & jnp.uint32(0xFFFF0000), F32)
    lo = pltpu.bitcast(w << 16, F32)
    return hi, lo


def _for_each_slot(idx_smem, fn):
    def body(t8, carry):
        base = t8 * SUBLANES
        for r in range(SUBLANES):
            for k in range(TOP_K):
                fn(k, base + r, idx_smem[k * ROW_TILE + base + r])
        return carry
    lax.fori_loop(0, ROW_TILE // SUBLANES, body, 0)


def _dispatch_kernel(y_ref, g_ref, idx_hbm, xs_in, xs_out, h_scr, idx_smem, idx_sem, row_sem):
    del xs_in
    i = pl.program_id(0)
    idx_cp = pltpu.make_async_copy(idx_hbm.at[pl.ds(i * (TOP_K * ROW_TILE), TOP_K * ROW_TILE)], idx_smem, idx_sem)
    idx_cp.start()
    h_scr[...] = _pack_rows(_rms(y_ref[...], g_ref[...]))
    idx_cp.wait()

    def issue(k, t, s):
        pltpu.make_async_copy(h_scr.at[pl.ds(t, 1)], xs_out.at[pl.ds(s, 1)], row_sem).start()

    _for_each_slot(idx_smem, issue)
    for _ in range(TOP_K):
        pltpu.make_async_copy(h_scr, xs_out.at[pl.ds(0, ROW_TILE)], row_sem).wait()


def _dispatch_call(y2d, gain, idx, p_total):
    n = y2d.shape[0]
    tm = ROW_TILE
    xs0 = jnp.zeros((p_total, D_MODEL // 2), jnp.uint32)
    return pl.pallas_call(
        _dispatch_kernel,
        grid=(n // tm,),
        in_specs=[
            pl.BlockSpec((tm, D_MODEL), lambda i: (i, 0)),
            pl.BlockSpec((1, D_MODEL), lambda i: (0, 0)),
            pl.BlockSpec(memory_space=pl.ANY),
            pl.BlockSpec(memory_space=pl.ANY),
        ],
        out_specs=pl.BlockSpec(memory_space=pl.ANY),
        out_shape=jax.ShapeDtypeStruct((p_total, D_MODEL // 2), jnp.uint32),
        scratch_shapes=[
            pltpu.VMEM((tm, D_MODEL // 2), jnp.uint32),
            pltpu.SMEM((TOP_K * tm,), jnp.int32),
            pltpu.SemaphoreType.DMA,
            pltpu.SemaphoreType.DMA,
        ],
        input_output_aliases={3: 0},
        compiler_params=_cparams(("arbitrary",)),
        name="moe_dispatch",
    )(y2d, gain, idx, xs0)


def _expert_kernel(be_ref, xs_ref, wg_ref, wu_ref, wd_ref, yb_ref):
    del be_ref
    hi, lo = _unpack_rows(xs_ref[...])
    xb = jnp.concatenate([hi.astype(BF16), lo.astype(BF16)], axis=1)
    g = jnp.dot(xb, wg_ref[0], preferred_element_type=F32)
    u = jnp.dot(xb, wu_ref[0], preferred_element_type=F32)
    hid = (g * jax.nn.sigmoid(g) * u).astype(BF16)
    yb_ref[...] = _pack_rows(jnp.dot(hid, wd_ref[0], preferred_element_type=F32))


def _expert_call(block_expert, xs, wg, wu, wd):
    p_total = xs.shape[0]
    blk = MOE_BLOCK
    wmap = lambda i, be: (be[i], 0, 0)
    return pl.pallas_call(
        _expert_kernel,
        grid_spec=pltpu.PrefetchScalarGridSpec(
            num_scalar_prefetch=1,
            grid=(p_total // blk,),
            in_specs=[
                pl.BlockSpec((blk, D_MODEL // 2), lambda i, be: (i, 0)),
                pl.BlockSpec((1, D_MODEL, D_EXPERT), wmap),
                pl.BlockSpec((1, D_MODEL, D_EXPERT), wmap),
                pl.BlockSpec((1, D_EXPERT, D_MODEL), wmap),
            ],
            out_specs=pl.BlockSpec((blk, D_MODEL // 2), lambda i, be: (i, 0)),
        ),
        out_shape=jax.ShapeDtypeStruct((p_total, D_MODEL // 2), jnp.uint32),
        compiler_params=_cparams(("arbitrary",)),
        name="moe_experts",
    )(block_expert, xs, wg, wu, wd)


def _combine_kernel(y_ref, rg_ref, idx_hbm, yb_hbm, out_ref, buf, idx_smem, idx_sem, row_sem):
    i = pl.program_id(0)
    idx_cp = pltpu.make_async_copy(idx_hbm.at[pl.ds(i * (TOP_K * ROW_TILE), TOP_K * ROW_TILE)], idx_smem, idx_sem)
    idx_cp.start()
    idx_cp.wait()

    def issue(k, t, s):
        pltpu.make_async_copy(yb_hbm.at[pl.ds(s, 1)], buf.at[k, pl.ds(t, 1)], row_sem).start()

    _for_each_slot(idx_smem, issue)
    for k in range(TOP_K):
        pltpu.make_async_copy(yb_hbm.at[pl.ds(0, ROW_TILE)], buf.at[k], row_sem).wait()
    rg = rg_ref[...]
    hi0, lo0 = _unpack_rows(buf[0])
    hi1, lo1 = _unpack_rows(buf[1])
    half = D_MODEL // 2
    out_ref[:, :half] = y_ref[:, :half] + rg[:, 0:1] * hi0 + rg[:, 1:2] * hi1
    out_ref[:, half:] = y_ref[:, half:] + rg[:, 0:1] * lo0 + rg[:, 1:2] * lo1


def _combine_call(y2d, rg, idx, yb):
    n = y2d.shape[0]
    tm = ROW_TILE
    return pl.pallas_call(
        _combine_kernel,
        grid=(n // tm,),
        in_specs=[
            pl.BlockSpec((tm, D_MODEL), lambda i: (i, 0)),
            pl.BlockSpec((tm, LANES), lambda i: (i, 0)),
            pl.BlockSpec(memory_space=pl.ANY),
            pl.BlockSpec(memory_space=pl.ANY),
        ],
        out_specs=pl.BlockSpec((tm, D_MODEL), lambda i: (i, 0)),
        out_shape=jax.ShapeDtypeStruct((n, D_MODEL), F32),
        scratch_shapes=[
            pltpu.VMEM((TOP_K, tm, D_MODEL // 2), jnp.uint32),
            pltpu.SMEM((TOP_K * tm,), jnp.int32),
            pltpu.SemaphoreType.DMA,
            pltpu.SemaphoreType.DMA,
        ],
        compiler_params=_cparams(("arbitrary",)),
        name="moe_combine",
    )(y2d, rg, idx, yb)


def _slot_tables(ri, n):
    e = ri[:, :TOP_K]
    onehot = (e[:, :, None] == jnp.arange(N_EXPERTS, dtype=jnp.int32)).astype(jnp.int32)
    tok_cnt = onehot.sum(axis=1)
    t = 256
    cnt_t = tok_cnt.reshape(n // t, t, N_EXPERTS)
    tri = (jnp.arange(t)[:, None] > jnp.arange(t)[None, :]).astype(F32)
    within = jnp.einsum("ts,nsc->ntc", tri, cnt_t.astype(F32)).astype(jnp.int32)
    tile_sum = cnt_t.sum(axis=1)
    tile_base = jnp.cumsum(tile_sum, axis=0) - tile_sum
    rank = (within + tile_base[:, None, :]).reshape(n, N_EXPERTS)
    counts = tile_sum.sum(axis=0)
    padded = ((counts + MOE_BLOCK - 1) // MOE_BLOCK) * MOE_BLOCK
    pad_end = jnp.cumsum(padded)
    pad_start = pad_end - padded
    slot = ((rank + pad_start)[:, None, :] * onehot).sum(axis=-1)
    n_blocks = (n * TOP_K) // MOE_BLOCK + N_EXPERTS
    block_start = jnp.arange(n_blocks, dtype=jnp.int32) * MOE_BLOCK
    block_expert = jnp.minimum((pad_end[None, :] <= block_start[:, None]).sum(axis=-1),
                               N_EXPERTS - 1).astype(jnp.int32)
    tm = ROW_TILE
    idx = slot.reshape(n // tm, tm, TOP_K).transpose(0, 2, 1).reshape(n * TOP_K)
    return idx.astype(jnp.int32), block_expert, n_blocks * MOE_BLOCK


def _moe(y2d, ri, rg, gain, wg, wu, wd):
    n = y2d.shape[0]
    idx, block_expert, p_total = _slot_tables(ri, n)
    xs = _dispatch_call(y2d, gain, idx, p_total)
    yb = _expert_call(block_expert, xs, wg, wu, wd)
    return _combine_call(y2d, rg, idx, yb)


def _rope_tables(seq):
    pos = jnp.arange(seq, dtype=F32)
    inv_freq = ROPE_THETA ** (-jnp.arange(0, ROPE_DIM, 2, dtype=F32) / ROPE_DIM)
    ang = pos[:, None] * inv_freq[None, :]
    cos = jnp.cos(ang)
    sin = jnp.sin(ang)
    half = ROPE_DIM // 2
    rest = HEAD_DIM - ROPE_DIM
    cos_h = jnp.concatenate([cos, cos, jnp.ones((seq, rest), F32)], axis=1)
    sa_h = jnp.concatenate([-sin, jnp.zeros((seq, half + rest), F32)], axis=1)
    sb_h = jnp.concatenate([jnp.zeros((seq, half), F32), sin, jnp.zeros((seq, rest), F32)], axis=1)
    tile = lambda t: jnp.tile(t, (1, HEADS_PER_GROUP))
    return tile(cos_h), tile(sa_h), tile(sb_h)


def _router_weights(w_rg, b_rg, w_re, b_re):
    w = jnp.concatenate([w_rg, w_re], axis=1)
    w = jnp.pad(w, ((0, 0), (0, LANES - w.shape[1])))
    w_hi = w.astype(BF16)
    w_lo = (w - w_hi.astype(F32)).astype(BF16)
    b = jnp.pad(jnp.concatenate([b_rg, b_re]), (0, LANES - N_EXPERT_GROUPS - N_EXPERTS))
    return jnp.concatenate([w_hi, w_lo], axis=1), b.reshape(1, LANES)


def _trunk(x, p):
    batch, seq, d = x.shape
    n = batch * seq
    x2d = x.reshape(n, d)
    cos, sa, sb = _rope_tables(seq)

    qkv = _qkv_call(x2d, seq, p["ln_mix0"], p["w_qkv"], p["mseg"], p["qg"], p["kg"], cos, sa, sb)
    os_, lses = [], []
    for g, dil in enumerate(DILATIONS):
        o, lse = _attn_call(qkv[g], qkv[3 + g], qkv[6 + g], batch, seq, dil)
        os_.append(o)
        lses.append(lse)
    y, ri, rg = _attn_out_call(os_, lses, x2d, p["w_o"], p["expand"], p["ln_ffn0"], p["wr0"], p["br0"])
    x2d = _moe(y, ri, rg, p["ln_ffn0"], p["wg0"], p["wu0"], p["wd0"])

    y, ri, rg = _conv_call(x2d, seq, p["ln_mix1"], p["w_pw1"], p["b_pw1"], p["w_dw"], p["b_dw"],
                           p["conv_ln_g"], p["conv_ln_b"], p["w_pw2"], p["b_pw2"],
                           p["ln_ffn1"], p["wr1"], p["br1"])
    x2d = _moe(y, ri, rg, p["ln_ffn1"], p["wg1"], p["wu1"], p["wd1"])
    return x2d.reshape(batch, seq, d)


def kernel(x_prompt, x_sample, ln_mix, ln_ffn, w_qkv, q_gain, k_gain, w_o, w_pw1, b_pw1, w_dw, b_dw,
           conv_ln_g, conv_ln_b, w_pw2, b_pw2, w_router_group, b_router_group, w_router_expert,
           b_router_expert, w_gate, w_up, w_down):
    row = lambda v: v.reshape(1, -1).astype(F32)
    n_heads = N_GROUPS * HEADS_PER_GROUP
    head_id = np.arange(ATTN_WIDTH) // HEAD_DIM
    mseg = jnp.asarray((head_id[:, None] == head_id[None, :]) / HEAD_DIM, BF16)
    slot_id = np.arange(GROUP_WIDTH) // HEAD_DIM
    expand = jnp.asarray(np.arange(LANES)[:, None] == slot_id[None, :], BF16)
    wr0, br0 = _router_weights(w_router_group[0], b_router_group[0], w_router_expert[0], b_router_expert[0])
    wr1, br1 = _router_weights(w_router_group[1], b_router_group[1], w_router_expert[1], b_router_expert[1])
    p = dict(
        ln_mix0=row(ln_mix[0]), ln_mix1=row(ln_mix[1]), ln_ffn0=row(ln_ffn[0]), ln_ffn1=row(ln_ffn[1]),
        w_qkv=w_qkv[0].astype(BF16), mseg=mseg, expand=expand,
        qg=row(jnp.tile(q_gain[0], n_heads) * HEAD_DIM ** -0.5), kg=row(jnp.tile(k_gain[0], n_heads)),
        w_o=w_o[0].astype(BF16),
        w_pw1=w_pw1[0].astype(BF16), b_pw1=row(b_pw1[0]), w_dw=w_dw[0].astype(F32), b_dw=row(b_dw[0]),
        conv_ln_g=row(conv_ln_g[0]), conv_ln_b=row(conv_ln_b[0]),
        w_pw2=w_pw2[0].astype(BF16), b_pw2=row(b_pw2[0]),
        wr0=wr0, br0=br0, wr1=wr1, br1=br1,
        wg0=w_gate[0].astype(BF16), wu0=w_up[0].astype(BF16), wd0=w_down[0].astype(BF16),
        wg1=w_gate[1].astype(BF16), wu1=w_up[1].astype(BF16), wd1=w_down[1].astype(BF16),
    )
    return _trunk(x_prompt, p), _trunk(x_sample, p)
```

```python
import functools

import jax
import jax.numpy as jnp
import numpy as np
from jax import lax
from jax.experimental import pallas as pl
from jax.experimental.pallas import tpu as pltpu
from jax.experimental.pallas import tpu_sc as plsc

D_MODEL = 1024
HEAD_DIM = 64
HEADS_PER_GROUP = 4
DILATIONS = (1, 4, 16)
HALF_WINDOW = 64
N_GROUPS = len(DILATIONS)
GROUP_WIDTH = HEADS_PER_GROUP * HEAD_DIM
ATTN_WIDTH = N_GROUPS * GROUP_WIDTH
ROPE_DIM = HEAD_DIM // 4
ROPE_THETA = 500000.0
CONV_WIDTH = 31
CONV_HALO = 16
N_EXPERT_GROUPS = 4
EXPERTS_PER_GROUP = 8
N_EXPERTS = N_EXPERT_GROUPS * EXPERTS_PER_GROUP
TOP_K = 2
D_EXPERT = 512
EPS = 1e-6
NEG_INF = -1e30

LANES = 128
SUBLANES = 8
ROW_TILE = 512
ATTN_Q_TILE = 128
ATTN_STEP_ROWS = 512
MOE_BLOCK = 256
VMEM_LIMIT = 56 * 1024 * 1024

PACKED_WIDTH = D_MODEL // 2
SC_WINDOW = 128
SC_ROW_SPLIT = 2
SC_ROW_WORDS = PACKED_WIDTH // SC_ROW_SPLIT

F32 = jnp.float32
BF16 = jnp.bfloat16


def _cparams(sem):
    return pltpu.CompilerParams(dimension_semantics=sem, vmem_limit_bytes=VMEM_LIMIT)


def _rms(x, gain):
    return x * lax.rsqrt(jnp.mean(x * x, axis=-1, keepdims=True) + EPS) * gain


def _pack_rows(x):
    bits = pltpu.bitcast(x.astype(BF16).astype(F32), jnp.uint32)
    half = x.shape[1] // 2
    return bits[:, :half] | (bits[:, half:] >> 16)


def _unpack_rows(w):
    hi = pltpu.bitcast(w & jnp.uint32(0xFFFF0000), F32)
    lo = pltpu.bitcast(w << 16, F32)
    return hi, lo


def _store_by_residue(out_ref, val, dil, scr):
    if dil == 1:
        out_ref[...] = val.astype(out_ref.dtype)
        return
    rows, width = val.shape
    for c in range(width // LANES):
        scr[c] = val[:, c * LANES:(c + 1) * LANES]
    for r in range(dil):
        for c in range(width // LANES):
            lanes = slice(r * width + c * LANES, r * width + (c + 1) * LANES)
            out_ref[:, lanes] = scr[c, pl.ds(r, rows // dil, stride=dil), :].astype(out_ref.dtype)


def _qkv_kernel(x_ref, g_ref, w_ref, mseg_ref, qg_ref, kg_ref, cos_ref, sa_ref, sb_ref,
                q0, q1, q2, k0, k1, k2, v0, v1, v2, scr):
    h = _rms(x_ref[...], g_ref[...]).astype(BF16)
    cos = cos_ref[...]
    sa = sa_ref[...]
    sb = sb_ref[...]

    def head_norm_rope(t, gain, outs):
        ms = jnp.dot((t * t).astype(BF16), mseg_ref[...], preferred_element_type=F32)
        tn = t * lax.rsqrt(ms + EPS) * gain
        for g in range(N_GROUPS):
            c = tn[:, g * GROUP_WIDTH:(g + 1) * GROUP_WIDTH]
            r = (c * cos + pltpu.roll(c, GROUP_WIDTH - ROPE_DIM // 2, 1) * sa
                 + pltpu.roll(c, ROPE_DIM // 2, 1) * sb)
            _store_by_residue(outs[g], r, DILATIONS[g], scr)

    q = jnp.dot(h, w_ref[:, 0:ATTN_WIDTH], preferred_element_type=F32)
    head_norm_rope(q, qg_ref[...], (q0, q1, q2))
    k = jnp.dot(h, w_ref[:, ATTN_WIDTH:2 * ATTN_WIDTH], preferred_element_type=F32)
    head_norm_rope(k, kg_ref[...], (k0, k1, k2))
    v = jnp.dot(h, w_ref[:, 2 * ATTN_WIDTH:3 * ATTN_WIDTH], preferred_element_type=F32)
    for g, o in enumerate((v0, v1, v2)):
        _store_by_residue(o, v[:, g * GROUP_WIDTH:(g + 1) * GROUP_WIDTH], DILATIONS[g], scr)


def _qkv_call(x2d, seq, gain, w_qkv, mseg, qg, kg, cos, sa, sb):
    n = x2d.shape[0]
    tm = ROW_TILE
    tiles_per_seq = seq // tm
    row = lambda i: (i, 0)
    const = lambda i: (0, 0)
    tab = lambda i: (i % tiles_per_seq, 0)
    out_specs = [pl.BlockSpec((tm // dil, dil * GROUP_WIDTH), row) for dil in DILATIONS] * 3
    out_shape = [jax.ShapeDtypeStruct((n // dil, dil * GROUP_WIDTH), BF16) for dil in DILATIONS] * 3
    return pl.pallas_call(
        _qkv_kernel,
        grid=(n // tm,),
        in_specs=[
            pl.BlockSpec((tm, D_MODEL), row),
            pl.BlockSpec((1, D_MODEL), const),
            pl.BlockSpec((D_MODEL, 3 * ATTN_WIDTH), const),
            pl.BlockSpec((ATTN_WIDTH, ATTN_WIDTH), const),
            pl.BlockSpec((1, ATTN_WIDTH), const),
            pl.BlockSpec((1, ATTN_WIDTH), const),
            pl.BlockSpec((tm, GROUP_WIDTH), tab),
            pl.BlockSpec((tm, GROUP_WIDTH), tab),
            pl.BlockSpec((tm, GROUP_WIDTH), tab),
        ],
        out_specs=out_specs,
        out_shape=out_shape,
        scratch_shapes=[pltpu.VMEM((GROUP_WIDTH // LANES, tm, LANES), F32)],
        compiler_params=_cparams(("parallel",)),
        name="qkv_proj",
    )(x2d, gain, w_qkv, mseg, qg, kg, cos, sa, sb)


def _attn_kernel(q_ref, k_ref, v_ref, o_ref, lse_ref, *, ls, rows, res, win):
    i = pl.program_id(2)
    tq = ATTN_Q_TILE
    rel = (lax.broadcasted_iota(jnp.int32, (tq, win), 0) - lax.broadcasted_iota(jnp.int32, (tq, win), 1))
    lane = lax.broadcasted_iota(jnp.int32, (tq, LANES), 1)
    for r in range(res):
        for sb in range(rows // tq):
            q_rows = slice(sb * tq, (sb + 1) * tq)
            q_start = i * rows + sb * tq
            k_start = jnp.clip(q_start - HALF_WINDOW, 0, ls - win)
            k_start = pl.multiple_of(k_start, HALF_WINDOW)
            mask = jnp.abs(rel + (q_start - k_start)) <= HALF_WINDOW
            lse_tile = jnp.zeros((tq, LANES), F32)
            for h in range(HEADS_PER_GROUP):
                cols = slice(r * GROUP_WIDTH + h * HEAD_DIM, r * GROUP_WIDTH + (h + 1) * HEAD_DIM)
                kw = k_ref[0, pl.ds(k_start, win), cols]
                vw = v_ref[0, pl.ds(k_start, win), cols]
                s = lax.dot_general(q_ref[0, q_rows, cols], kw, (((1,), (1,)), ((), ())),
                                    preferred_element_type=F32)
                s = jnp.where(mask, s, NEG_INF)
                m = jnp.max(s, axis=-1, keepdims=True)
                p = jnp.exp(s - m)
                den = jnp.sum(p, axis=-1, keepdims=True)
                o = jnp.dot(p.astype(BF16), vw, preferred_element_type=F32) / den
                o_ref[0, q_rows, cols] = o.astype(BF16)
                lse_tile = jnp.where(lane == h, m + jnp.log(den), lse_tile)
            lse_ref[0, q_rows, r * LANES:(r + 1) * LANES] = lse_tile


def _attn_call(q, k, v, batch, seq, dil):
    ls = seq // dil
    rows = min(ATTN_STEP_ROWS, ls)
    res = max(1, min(dil, ATTN_STEP_ROWS // rows))
    win = min(ATTN_Q_TILE + 2 * HALF_WINDOW, ls)
    view = lambda t: t.reshape(batch, ls, dil * GROUP_WIDTH)
    qmap = lambda b, r, i: (b, i, r)
    kvmap = lambda b, r, i: (b, 0, r)
    o, lse = pl.pallas_call(
        functools.partial(_attn_kernel, ls=ls, rows=rows, res=res, win=win),
        grid=(batch, dil // res, ls // rows),
        in_specs=[
            pl.BlockSpec((1, rows, res * GROUP_WIDTH), qmap),
            pl.BlockSpec((1, ls, res * GROUP_WIDTH), kvmap),
            pl.BlockSpec((1, ls, res * GROUP_WIDTH), kvmap),
        ],
        out_specs=[
            pl.BlockSpec((1, rows, res * GROUP_WIDTH), qmap),
            pl.BlockSpec((1, rows, res * LANES), qmap),
        ],
        out_shape=[
            jax.ShapeDtypeStruct((batch, ls, dil * GROUP_WIDTH), BF16),
            jax.ShapeDtypeStruct((batch, ls, dil * LANES), F32),
        ],
        compiler_params=_cparams(("parallel", "parallel", "arbitrary")),
        name=f"banded_attn_d{dil}",
    )(view(q), view(k), view(v))
    return o.reshape(batch * ls, dil * GROUP_WIDTH), lse.reshape(batch * ls, dil * LANES)


def _route(x, gain, wr_ref, br_ref, ri_ref, rg_ref, hp_ref):
    h = _rms(x, gain)
    packed = _pack_rows(h)
    for c in range(SC_ROW_SPLIT):
        hp_ref[c] = packed[:, c * SC_ROW_WORDS:(c + 1) * SC_ROW_WORDS]
    h_hi = h.astype(BF16)
    h_lo = (h - h_hi.astype(F32)).astype(BF16)
    acc = (jnp.dot(h_hi, wr_ref[...], preferred_element_type=F32)
           + jnp.dot(h_lo, wr_ref[...], preferred_element_type=F32))
    logits = acc[:, :LANES] + acc[:, LANES:] + br_ref[...]
    lane = lax.broadcasted_iota(jnp.int32, logits.shape, 1)
    lane_f = lane.astype(F32)
    big = jnp.float32(2 * LANES)

    is_g = lane < N_EXPERT_GROUPS
    gl = jnp.where(is_g, logits, NEG_INF)
    gmax = jnp.max(gl, axis=-1, keepdims=True)
    grp = jnp.min(jnp.where(gl == gmax, lane_f, big), axis=-1, keepdims=True)
    p_grp = 1.0 / jnp.sum(jnp.where(is_g, jnp.exp(gl - gmax), 0.0), axis=-1, keepdims=True)

    lo = N_EXPERT_GROUPS + grp * EXPERTS_PER_GROUP
    in_grp = (lane_f >= lo) & (lane_f < lo + EXPERTS_PER_GROUP)
    el = jnp.where(in_grp, logits, NEG_INF)
    v1 = jnp.max(el, axis=-1, keepdims=True)
    i1 = jnp.min(jnp.where(el == v1, lane_f, big), axis=-1, keepdims=True)
    el2 = jnp.where(lane_f == i1, NEG_INF, el)
    v2 = jnp.max(el2, axis=-1, keepdims=True)
    i2 = jnp.min(jnp.where(el2 == v2, lane_f, big), axis=-1, keepdims=True)
    e21 = jnp.exp(v2 - v1)
    g1 = p_grp / (1.0 + e21)
    g2 = g1 * e21
    ri = jnp.where(lane == 0, i1 - N_EXPERT_GROUPS, jnp.where(lane == 1, i2 - N_EXPERT_GROUPS, 0.0))
    ri_ref[...] = ri.astype(jnp.int32)
    rg_ref[...] = jnp.where(lane == 0, g1, jnp.where(lane == 1, g2, 0.0))


def _route_out_specs(n, tm):
    row = lambda i: (i, 0)
    specs = [
        pl.BlockSpec((tm, D_MODEL), row),
        pl.BlockSpec((tm, LANES), row),
        pl.BlockSpec((tm, LANES), row),
        pl.BlockSpec((SC_ROW_SPLIT, tm, SC_ROW_WORDS), lambda i: (0, i, 0)),
    ]
    shapes = [
        jax.ShapeDtypeStruct((n, D_MODEL), F32),
        jax.ShapeDtypeStruct((n, LANES), jnp.int32),
        jax.ShapeDtypeStruct((n, LANES), F32),
        jax.ShapeDtypeStruct((SC_ROW_SPLIT, n, SC_ROW_WORDS), jnp.uint32),
    ]
    return specs, shapes


def _load_by_residue(in_ref, dil, scr):
    if dil == 1:
        return in_ref[...].astype(F32)
    chunks, rows, _ = scr.shape
    width = chunks * LANES
    for r in range(dil):
        for c in range(chunks):
            lanes = slice(r * width + c * LANES, r * width + (c + 1) * LANES)
            scr[c, pl.ds(r, rows // dil, stride=dil), :] = in_ref[:, lanes].astype(F32)
    return jnp.concatenate([scr[c] for c in range(chunks)], axis=1) if chunks > 1 else scr[0]


def _attn_out_kernel(o0, o1, o2, l0, l1, l2, x_ref, wo_ref, ex_ref, g_ref, wr_ref, br_ref,
                     y_ref, ri_ref, rg_ref, hp_ref, o_scr1, o_scr2, l_scr1, l_scr2):
    ls = [_load_by_residue(l, dil, scr)
          for l, dil, scr in zip((l0, l1, l2), DILATIONS, (None, l_scr1, l_scr2))]
    m = jnp.maximum(jnp.maximum(ls[0], ls[1]), ls[2])
    es = [jnp.exp(l - m) for l in ls]
    inv = 1.0 / (es[0] + es[1] + es[2])
    y = x_ref[...]
    for g, (o_ref, scr) in enumerate(zip((o0, o1, o2), (None, o_scr1, o_scr2))):
        alpha = es[g] * inv
        a_hi = alpha.astype(BF16)
        a_lo = (alpha - a_hi.astype(F32)).astype(BF16)
        a_wide = (jnp.dot(a_hi, ex_ref[...], preferred_element_type=F32)
                  + jnp.dot(a_lo, ex_ref[...], preferred_element_type=F32))
        scaled = (_load_by_residue(o_ref, DILATIONS[g], scr) * a_wide).astype(BF16)
        y = y + jnp.dot(scaled, wo_ref[g * GROUP_WIDTH:(g + 1) * GROUP_WIDTH, :],
                        preferred_element_type=F32)
    y_ref[...] = y
    _route(y, g_ref[...], wr_ref, br_ref, ri_ref, rg_ref, hp_ref)


def _attn_out_call(os_, lses, x2d, w_o, expand, gain, wr, br):
    n = x2d.shape[0]
    tm = ROW_TILE
    row = lambda i: (i, 0)
    const = lambda i: (0, 0)
    out_specs, out_shape = _route_out_specs(n, tm)
    return pl.pallas_call(
        _attn_out_kernel,
        grid=(n // tm,),
        in_specs=(
            [pl.BlockSpec((tm // dil, dil * GROUP_WIDTH), row) for dil in DILATIONS]
            + [pl.BlockSpec((tm // dil, dil * LANES), row) for dil in DILATIONS]
            + [
                pl.BlockSpec((tm, D_MODEL), row),
                pl.BlockSpec((ATTN_WIDTH, D_MODEL), const),
                pl.BlockSpec((LANES, GROUP_WIDTH), const),
                pl.BlockSpec((1, D_MODEL), const),
                pl.BlockSpec((D_MODEL, 2 * LANES), const),
                pl.BlockSpec((1, LANES), const),
            ]
        ),
        out_specs=out_specs,
        out_shape=out_shape,
        scratch_shapes=([pltpu.VMEM((GROUP_WIDTH // LANES, tm, LANES), F32)] * 2
                        + [pltpu.VMEM((1, tm, LANES), F32)] * 2),
        compiler_params=_cparams(("parallel",)),
        name="attn_out_router",
    )(*os_, *lses, x2d, w_o, expand, gain, wr, br)


def _conv_kernel(xc_ref, xp_ref, xn_ref, gm_ref, w1_ref, b1_ref, wdw_ref, bdw_ref, lng_ref, lnb_ref,
                 w2_ref, b2_ref, gf_ref, wr_ref, br_ref, y_ref, ri_ref, rg_ref, hp_ref,
                 x_scr, u_scr, c_scr, *, seq, tm):
    i = pl.program_id(0)
    pos0 = (i * tm) % seq
    hl = CONV_HALO
    x_scr[0:hl, :] = xp_ref[...]
    x_scr[hl:hl + tm, :] = xc_ref[...]
    x_scr[hl + tm:hl + tm + hl, :] = xn_ref[...]
    h = _rms(x_scr[...], gm_ref[...]).astype(BF16)
    a = jnp.dot(h, w1_ref[...], preferred_element_type=F32) + b1_ref[...]
    u = a[:, :D_MODEL] * jax.nn.sigmoid(a[:, D_MODEL:])
    r = lax.broadcasted_iota(jnp.int32, (tm + 2 * hl, 1), 0)
    outside = ((r < hl) & (pos0 == 0)) | ((r >= hl + tm) & (pos0 + tm == seq))
    u_scr[...] = jnp.where(outside, 0.0, u)

    rc = 64
    lc = LANES
    first = hl - CONV_WIDTH // 2
    for j in range(tm // rc):
        for c in range(D_MODEL // lc):
            cols = slice(c * lc, (c + 1) * lc)
            acc = jnp.zeros((rc, lc), F32) + bdw_ref[:, cols]
            for b in range(SUBLANES):
                part = None
                for t in range(CONV_WIDTH):
                    if (t + first) % SUBLANES != b:
                        continue
                    off = j * rc + (t + first) - b
                    term = u_scr[off:off + rc + SUBLANES, cols] * wdw_ref[t:t + 1, cols]
                    part = term if part is None else part + term
                acc = acc + part[b:b + rc]
            c_scr[j * rc:(j + 1) * rc, cols] = acc
    cv = c_scr[...]
    mu = jnp.mean(cv, axis=-1, keepdims=True)
    xc = cv - mu
    var = jnp.mean(xc * xc, axis=-1, keepdims=True)
    ln = xc * lax.rsqrt(var + EPS) * lng_ref[...] + lnb_ref[...]
    act = (ln * jax.nn.sigmoid(ln)).astype(BF16)
    y = xc_ref[...] + jnp.dot(act, w2_ref[...], preferred_element_type=F32) + b2_ref[...]
    y_ref[...] = y
    _route(y, gf_ref[...], wr_ref, br_ref, ri_ref, rg_ref, hp_ref)


def _conv_call(x2d, seq, gm, w1, b1, wdw, bdw, lng, lnb, w2, b2, gf, wr, br):
    n = x2d.shape[0]
    tm = ROW_TILE
    hl = CONV_HALO
    per = tm // hl
    last = n // hl - 1
    row = lambda i: (i, 0)
    const = lambda i: (0, 0)
    prev = lambda i: (jnp.maximum(i * per - 1, 0), 0)
    nxt = lambda i: (jnp.minimum((i + 1) * per, last), 0)
    out_specs, out_shape = _route_out_specs(n, tm)
    return pl.pallas_call(
        functools.partial(_conv_kernel, seq=seq, tm=tm),
        grid=(n // tm,),
        in_specs=[
            pl.BlockSpec((tm, D_MODEL), row),
            pl.BlockSpec((hl, D_MODEL), prev),
            pl.BlockSpec((hl, D_MODEL), nxt),
            pl.BlockSpec((1, D_MODEL), const),
            pl.BlockSpec((D_MODEL, 2 * D_MODEL), const),
            pl.BlockSpec((1, 2 * D_MODEL), const),
            pl.BlockSpec((CONV_WIDTH, D_MODEL), const),
            pl.BlockSpec((1, D_MODEL), const),
            pl.BlockSpec((1, D_MODEL), const),
            pl.BlockSpec((1, D_MODEL), const),
            pl.BlockSpec((D_MODEL, D_MODEL), const),
            pl.BlockSpec((1, D_MODEL), const),
            pl.BlockSpec((1, D_MODEL), const),
            pl.BlockSpec((D_MODEL, 2 * LANES), const),
            pl.BlockSpec((1, LANES), const),
        ],
        out_specs=out_specs,
        out_shape=out_shape,
        scratch_shapes=[
            pltpu.VMEM((tm + 2 * hl, D_MODEL), F32),
            pltpu.VMEM((tm + 2 * hl, D_MODEL), F32),
            pltpu.VMEM((tm, D_MODEL), F32),
        ],
        compiler_params=_cparams(("parallel",)),
        name="conv_module_router",
    )(x2d, x2d, x2d, gm, w1, b1, wdw, bdw, lng, lnb, w2, b2, gf, wr, br)


def _sc_mesh():
    return plsc.VectorSubcoreMesh(core_axis_name="core", subcore_axis_name="subcore")


def _sc_scatter_rows(src, idx, n_src_windows, out_rows):
    m = idx.shape[1]
    width = src.shape[1]
    nsw = n_src_windows

    @functools.partial(pl.kernel, out_type=jax.ShapeDtypeStruct((out_rows, width), src.dtype),
                       mesh=_sc_mesh(), name="sc_scatter_rows")
    def scatter(src_hbm, idx_hbm, out_hbm):
        def body(src_vmem, idx_vmem):
            pltpu.sync_copy(src_vmem, out_hbm.at[idx_vmem.at[0]])

        pltpu.emit_pipeline(
            body,
            grid=(m // SC_WINDOW,),
            in_specs=[
                pl.BlockSpec((SC_WINDOW, width), lambda w: ((w // (TOP_K * nsw)) * nsw + w % nsw, 0)),
                pl.BlockSpec((1, SC_WINDOW), lambda w: (0, w)),
            ],
            out_specs=[],
            core_axis_name=("core", "subcore"),
            dimension_semantics=(pltpu.PARALLEL,),
        )(src_hbm, idx_hbm)

    return scatter(src, idx)


def _sc_gather_rows(table, idx):
    m = idx.shape[1]
    width = table.shape[1]

    @functools.partial(pl.kernel, out_type=jax.ShapeDtypeStruct((m, width), table.dtype),
                       mesh=_sc_mesh(), name="sc_gather_rows")
    def gather(table_hbm, idx_hbm, out_hbm):
        def body(idx_vmem, out_vmem):
            pltpu.sync_copy(table_hbm.at[idx_vmem.at[0]], out_vmem)

        pltpu.emit_pipeline(
            body,
            grid=(m // SC_WINDOW,),
            in_specs=[pl.BlockSpec((1, SC_WINDOW), lambda w: (0, w))],
            out_specs=[pl.BlockSpec((SC_WINDOW, width), lambda w: (w, 0))],
            core_axis_name=("core", "subcore"),
            dimension_semantics=(pltpu.PARALLEL,),
        )(idx_hbm, out_hbm)

    return gather(table, idx)


def _expert_kernel(be_ref, nv_ref, xs_ref, wg_ref, wu_ref, wd_ref, yb_ref):
    del be_ref
    i = pl.program_id(0)
    words = jnp.concatenate([xs_ref[c] for c in range(SC_ROW_SPLIT)], axis=1)
    row = lax.broadcasted_iota(jnp.int32, (words.shape[0], 1), 0)
    words = jnp.where(row < nv_ref[i], words, jnp.uint32(0))
    hi, lo = _unpack_rows(words)
    xb = jnp.concatenate([hi.astype(BF16), lo.astype(BF16)], axis=1)
    g = jnp.dot(xb, wg_ref[0], preferred_element_type=F32)
    u = jnp.dot(xb, wu_ref[0], preferred_element_type=F32)
    hid = (g * jax.nn.sigmoid(g) * u).astype(BF16)
    packed = _pack_rows(jnp.dot(hid, wd_ref[0], preferred_element_type=F32))
    for c in range(SC_ROW_SPLIT):
        yb_ref[c] = packed[:, c * SC_ROW_WORDS:(c + 1) * SC_ROW_WORDS]


def _expert_call(block_expert, block_valid, xs, wg, wu, wd):
    p_total = xs.shape[1]
    blk = MOE_BLOCK
    wmap = lambda i, be, nv: (be[i], 0, 0)
    rows = lambda i, be, nv: (0, i, 0)
    return pl.pallas_call(
        _expert_kernel,
        grid_spec=pltpu.PrefetchScalarGridSpec(
            num_scalar_prefetch=2,
            grid=(p_total // blk,),
            in_specs=[
                pl.BlockSpec((SC_ROW_SPLIT, blk, SC_ROW_WORDS), rows),
                pl.BlockSpec((1, D_MODEL, D_EXPERT), wmap),
                pl.BlockSpec((1, D_MODEL, D_EXPERT), wmap),
                pl.BlockSpec((1, D_EXPERT, D_MODEL), wmap),
            ],
            out_specs=pl.BlockSpec((SC_ROW_SPLIT, blk, SC_ROW_WORDS), rows),
        ),
        out_shape=jax.ShapeDtypeStruct((SC_ROW_SPLIT, p_total, SC_ROW_WORDS), jnp.uint32),
        compiler_params=_cparams(("arbitrary",)),
        name="moe_experts",
    )(block_expert, block_valid, xs, wg, wu, wd)


def _combine_kernel(y_ref, rg_ref, *refs):
    out_ref = refs[-1]
    rg = rg_ref[...]
    half = D_MODEL // 2
    for c in range(SC_ROW_SPLIT):
        hi0, lo0 = _unpack_rows(refs[c * TOP_K][...])
        hi1, lo1 = _unpack_rows(refs[c * TOP_K + 1][...])
        a = slice(c * SC_ROW_WORDS, (c + 1) * SC_ROW_WORDS)
        b = slice(half + c * SC_ROW_WORDS, half + (c + 1) * SC_ROW_WORDS)
        out_ref[:, a] = y_ref[:, a] + rg[:, 0:1] * hi0 + rg[:, 1:2] * hi1
        out_ref[:, b] = y_ref[:, b] + rg[:, 0:1] * lo0 + rg[:, 1:2] * lo1


def _combine_call(y2d, rg, rows):
    n = y2d.shape[0]
    tm = ROW_TILE
    tiles = n // tm
    return pl.pallas_call(
        _combine_kernel,
        grid=(tiles,),
        in_specs=[
            pl.BlockSpec((tm, D_MODEL), lambda i: (i, 0)),
            pl.BlockSpec((tm, LANES), lambda i: (i, 0)),
        ] + [
            pl.BlockSpec((tm, SC_ROW_WORDS), functools.partial(lambda i, j: (i + j * tiles, 0), j=j))
            for j in range(SC_ROW_SPLIT * TOP_K)
        ],
        out_specs=pl.BlockSpec((tm, D_MODEL), lambda i: (i, 0)),
        out_shape=jax.ShapeDtypeStruct((n, D_MODEL), F32),
        compiler_params=_cparams(("parallel",)),
        name="moe_combine",
    )(y2d, rg, *([rows] * (SC_ROW_SPLIT * TOP_K)))


def _slot_tables(ri, n):
    e = ri[:, :TOP_K]
    onehot = (e[:, :, None] == jnp.arange(N_EXPERTS, dtype=jnp.int32)).astype(jnp.int32)
    tok_cnt = onehot.sum(axis=1)
    t = 256
    cnt_t = tok_cnt.reshape(n // t, t, N_EXPERTS)
    tri = (jnp.arange(t)[:, None] > jnp.arange(t)[None, :]).astype(F32)
    within = jnp.einsum("ts,nsc->ntc", tri, cnt_t.astype(F32)).astype(jnp.int32)
    tile_sum = cnt_t.sum(axis=1)
    tile_base = jnp.cumsum(tile_sum, axis=0) - tile_sum
    rank = (within + tile_base[:, None, :]).reshape(n, N_EXPERTS)
    counts = tile_sum.sum(axis=0)
    padded = ((counts + MOE_BLOCK - 1) // MOE_BLOCK) * MOE_BLOCK
    pad_end = jnp.cumsum(padded)
    pad_start = pad_end - padded
    slot = ((rank + pad_start)[:, None, :] * onehot).sum(axis=-1)
    n_blocks = (n * TOP_K) // MOE_BLOCK + N_EXPERTS
    block_start = jnp.arange(n_blocks, dtype=jnp.int32) * MOE_BLOCK
    block_expert = jnp.minimum((pad_end[None, :] <= block_start[:, None]).sum(axis=-1),
                               N_EXPERTS - 1).astype(jnp.int32)
    used_end = (pad_start + counts)[block_expert]
    block_valid = jnp.clip(used_end - block_start, 0, MOE_BLOCK).astype(jnp.int32)
    p_total = n_blocks * MOE_BLOCK
    slot_km = slot.T.reshape(1, TOP_K * n).astype(jnp.int32)
    idx = jnp.concatenate([slot_km + c * p_total for c in range(SC_ROW_SPLIT)], axis=1)
    return idx, block_expert, block_valid, p_total


def _moe(y2d, ri, rg, hp, wg, wu, wd):
    n = y2d.shape[0]
    idx, block_expert, block_valid, p_total = _slot_tables(ri, n)
    xs = _sc_scatter_rows(hp.reshape(SC_ROW_SPLIT * n, SC_ROW_WORDS), idx, n // SC_WINDOW,
                          SC_ROW_SPLIT * p_total)
    yb = _expert_call(block_expert, block_valid, xs.reshape(SC_ROW_SPLIT, p_total, SC_ROW_WORDS),
                      wg, wu, wd)
    rows = _sc_gather_rows(yb.reshape(SC_ROW_SPLIT * p_total, SC_ROW_WORDS), idx)
    return _combine_call(y2d, rg, rows)


def _rope_tables(seq):
    pos = jnp.arange(seq, dtype=F32)
    inv_freq = ROPE_THETA ** (-jnp.arange(0, ROPE_DIM, 2, dtype=F32) / ROPE_DIM)
    ang = pos[:, None] * inv_freq[None, :]
    cos = jnp.cos(ang)
    sin = jnp.sin(ang)
    half = ROPE_DIM // 2
    rest = HEAD_DIM - ROPE_DIM
    cos_h = jnp.concatenate([cos, cos, jnp.ones((seq, rest), F32)], axis=1)
    sa_h = jnp.concatenate([-sin, jnp.zeros((seq, half + rest), F32)], axis=1)
    sb_h = jnp.concatenate([jnp.zeros((seq, half), F32), sin, jnp.zeros((seq, rest), F32)], axis=1)
    tile = lambda t: jnp.tile(t, (1, HEADS_PER_GROUP))
    return tile(cos_h), tile(sa_h), tile(sb_h)


def _router_weights(w_rg, b_rg, w_re, b_re):
    w = jnp.concatenate([w_rg, w_re], axis=1)
    w = jnp.pad(w, ((0, 0), (0, LANES - w.shape[1])))
    w_hi = w.astype(BF16)
    w_lo = (w - w_hi.astype(F32)).astype(BF16)
    b = jnp.pad(jnp.concatenate([b_rg, b_re]), (0, LANES - N_EXPERT_GROUPS - N_EXPERTS))
    return jnp.concatenate([w_hi, w_lo], axis=1), b.reshape(1, LANES)


def _trunk(x, p):
    batch, seq, d = x.shape
    n = batch * seq
    x2d = x.reshape(n, d)
    cos, sa, sb = _rope_tables(seq)

    qkv = _qkv_call(x2d, seq, p["ln_mix0"], p["w_qkv"], p["mseg"], p["qg"], p["kg"], cos, sa, sb)
    os_, lses = [], []
    for g, dil in enumerate(DILATIONS):
        o, lse = _attn_call(qkv[g], qkv[3 + g], qkv[6 + g], batch, seq, dil)
        os_.append(o)
        lses.append(lse)
    y, ri, rg, hp = _attn_out_call(os_, lses, x2d, p["w_o"], p["expand"], p["ln_ffn0"], p["wr0"], p["br0"])
    x2d = _moe(y, ri, rg, hp, p["wg0"], p["wu0"], p["wd0"])

    y, ri, rg, hp = _conv_call(x2d, seq, p["ln_mix1"], p["w_pw1"], p["b_pw1"], p["w_dw"], p["b_dw"],
                               p["conv_ln_g"], p["conv_ln_b"], p["w_pw2"], p["b_pw2"],
                               p["ln_ffn1"], p["wr1"], p["br1"])
    x2d = _moe(y, ri, rg, hp, p["wg1"], p["wu1"], p["wd1"])
    return x2d.reshape(batch, seq, d)


def kernel(x_prompt, x_sample, ln_mix, ln_ffn, w_qkv, q_gain, k_gain, w_o, w_pw1, b_pw1, w_dw, b_dw,
           conv_ln_g, conv_ln_b, w_pw2, b_pw2, w_router_group, b_router_group, w_router_expert,
           b_router_expert, w_gate, w_up, w_down):
    row = lambda v: v.reshape(1, -1).astype(F32)
    n_heads = N_GROUPS * HEADS_PER_GROUP
    head_id = np.arange(ATTN_WIDTH) // HEAD_DIM
    mseg = jnp.asarray((head_id[:, None] == head_id[None, :]) / HEAD_DIM, BF16)
    slot_id = np.arange(GROUP_WIDTH) // HEAD_DIM
    expand = jnp.asarray(np.arange(LANES)[:, None] == slot_id[None, :], BF16)
    wr0, br0 = _router_weights(w_router_group[0], b_router_group[0], w_router_expert[0], b_router_expert[0])
    wr1, br1 = _router_weights(w_router_group[1], b_router_group[1], w_router_expert[1], b_router_expert[1])
    p = dict(
        ln_mix0=row(ln_mix[0]), ln_mix1=row(ln_mix[1]), ln_ffn0=row(ln_ffn[0]), ln_ffn1=row(ln_ffn[1]),
        w_qkv=w_qkv[0].astype(BF16), mseg=mseg, expand=expand,
        qg=row(jnp.tile(q_gain[0], n_heads) * HEAD_DIM ** -0.5), kg=row(jnp.tile(k_gain[0], n_heads)),
        w_o=w_o[0].astype(BF16),
        w_pw1=w_pw1[0].astype(BF16), b_pw1=row(b_pw1[0]), w_dw=w_dw[0].astype(F32), b_dw=row(b_dw[0]),
        conv_ln_g=row(conv_ln_g[0]), conv_ln_b=row(conv_ln_b[0]),
        w_pw2=w_pw2[0].astype(BF16), b_pw2=row(b_pw2[0]),
        wr0=wr0, br0=br0, wr1=wr1, br1=br1,
        wg0=w_gate[0].astype(BF16), wu0=w_up[0].astype(BF16), wd0=w_down[0].astype(BF16),
        wg1=w_gate[1].astype(BF16), wu1=w_up[1].astype(BF16), wd1=w_down[1].astype(BF16),
    )
    return _trunk(x_prompt, p), _trunk(x_sample, p)
```

```python
import functools

import jax
import jax.numpy as jnp
import numpy as np
from jax import lax
from jax.experimental import pallas as pl
from jax.experimental.pallas import tpu as pltpu
from jax.experimental.pallas import tpu_sc as plsc

D_MODEL = 1024
HEAD_DIM = 64
HEADS_PER_GROUP = 4
DILATIONS = (1, 4, 16)
HALF_WINDOW = 64
N_GROUPS = len(DILATIONS)
GROUP_WIDTH = HEADS_PER_GROUP * HEAD_DIM
ATTN_WIDTH = N_GROUPS * GROUP_WIDTH
ROPE_DIM = HEAD_DIM // 4
ROPE_THETA = 500000.0
CONV_WIDTH = 31
CONV_HALO = 16
N_EXPERT_GROUPS = 4
EXPERTS_PER_GROUP = 8
N_EXPERTS = N_EXPERT_GROUPS * EXPERTS_PER_GROUP
TOP_K = 2
D_EXPERT = 512
EPS = 1e-6
NEG_INF = -1e30

LANES = 128
SUBLANES = 8
ROW_TILE = 512
ATTN_Q_TILE = 128
ATTN_STEP_ROWS = 512
MOE_BLOCK = 512
VMEM_LIMIT = 56 * 1024 * 1024

PACKED_WIDTH = D_MODEL // 2
SC_WINDOW = 128
SC_ROW_SPLIT = 2
SC_ROW_WORDS = PACKED_WIDTH // SC_ROW_SPLIT

F32 = jnp.float32
BF16 = jnp.bfloat16


def _cparams(sem):
    return pltpu.CompilerParams(dimension_semantics=sem, vmem_limit_bytes=VMEM_LIMIT)


def _rms(x, gain):
    return x * lax.rsqrt(jnp.mean(x * x, axis=-1, keepdims=True) + EPS) * gain


def _pack_rows(x):
    bits = pltpu.bitcast(x.astype(BF16).astype(F32), jnp.uint32)
    half = x.shape[1] // 2
    return bits[:, :half] | (bits[:, half:] >> 16)


def _unpack_rows(w):
    hi = pltpu.bitcast(w & jnp.uint32(0xFFFF0000), F32)
    lo = pltpu.bitcast(w << 16, F32)
    return hi, lo


def _store_by_residue(out_ref, val, dil, scr):
    if dil == 1:
        out_ref[...] = val.astype(out_ref.dtype)
        return
    rows, width = val.shape
    for c in range(width // LANES):
        scr[c] = val[:, c * LANES:(c + 1) * LANES]
    for r in range(dil):
        for c in range(width // LANES):
            lanes = slice(r * width + c * LANES, r * width + (c + 1) * LANES)
            out_ref[:, lanes] = scr[c, pl.ds(r, rows // dil, stride=dil), :].astype(out_ref.dtype)


def _qkv_kernel(x_ref, g_ref, w_ref, mseg_ref, qg_ref, kg_ref, cos_ref, sa_ref, sb_ref,
                q0, q1, q2, k0, k1, k2, v0, v1, v2, scr):
    h = _rms(x_ref[...], g_ref[...]).astype(BF16)
    cos = cos_ref[...]
    sa = sa_ref[...]
    sb = sb_ref[...]

    def head_norm_rope(t, gain, outs):
        ms = jnp.dot((t * t).astype(BF16), mseg_ref[...], preferred_element_type=F32)
        tn = t * lax.rsqrt(ms + EPS) * gain
        for g in range(N_GROUPS):
            c = tn[:, g * GROUP_WIDTH:(g + 1) * GROUP_WIDTH]
            r = (c * cos + pltpu.roll(c, GROUP_WIDTH - ROPE_DIM // 2, 1) * sa
                 + pltpu.roll(c, ROPE_DIM // 2, 1) * sb)
            _store_by_residue(outs[g], r, DILATIONS[g], scr)

    q = jnp.dot(h, w_ref[:, 0:ATTN_WIDTH], preferred_element_type=F32)
    head_norm_rope(q, qg_ref[...], (q0, q1, q2))
    k = jnp.dot(h, w_ref[:, ATTN_WIDTH:2 * ATTN_WIDTH], preferred_element_type=F32)
    head_norm_rope(k, kg_ref[...], (k0, k1, k2))
    v = jnp.dot(h, w_ref[:, 2 * ATTN_WIDTH:3 * ATTN_WIDTH], preferred_element_type=F32)
    for g, o in enumerate((v0, v1, v2)):
        _store_by_residue(o, v[:, g * GROUP_WIDTH:(g + 1) * GROUP_WIDTH], DILATIONS[g], scr)


def _qkv_call(x2d, seq, gain, w_qkv, mseg, qg, kg, cos, sa, sb):
    n = x2d.shape[0]
    tm = ROW_TILE
    tiles_per_seq = seq // tm
    row = lambda i: (i, 0)
    const = lambda i: (0, 0)
    tab = lambda i: (i % tiles_per_seq, 0)
    out_specs = [pl.BlockSpec((tm // dil, dil * GROUP_WIDTH), row) for dil in DILATIONS] * 3
    out_shape = [jax.ShapeDtypeStruct((n // dil, dil * GROUP_WIDTH), BF16) for dil in DILATIONS] * 3
    return pl.pallas_call(
        _qkv_kernel,
        grid=(n // tm,),
        in_specs=[
            pl.BlockSpec((tm, D_MODEL), row),
            pl.BlockSpec((1, D_MODEL), const),
            pl.BlockSpec((D_MODEL, 3 * ATTN_WIDTH), const),
            pl.BlockSpec((ATTN_WIDTH, ATTN_WIDTH), const),
            pl.BlockSpec((1, ATTN_WIDTH), const),
            pl.BlockSpec((1, ATTN_WIDTH), const),
            pl.BlockSpec((tm, GROUP_WIDTH), tab),
            pl.BlockSpec((tm, GROUP_WIDTH), tab),
            pl.BlockSpec((tm, GROUP_WIDTH), tab),
        ],
        out_specs=out_specs,
        out_shape=out_shape,
        scratch_shapes=[pltpu.VMEM((GROUP_WIDTH // LANES, tm, LANES), F32)],
        compiler_params=_cparams(("parallel",)),
        name="qkv_proj",
    )(x2d, gain, w_qkv, mseg, qg, kg, cos, sa, sb)


def _attn_kernel(q_ref, k_ref, v_ref, o_ref, lse_ref, *, ls, rows, res, win):
    i = pl.program_id(2)
    tq = ATTN_Q_TILE
    rel = (lax.broadcasted_iota(jnp.int32, (tq, win), 0) - lax.broadcasted_iota(jnp.int32, (tq, win), 1))
    lane = lax.broadcasted_iota(jnp.int32, (tq, LANES), 1)
    for r in range(res):
        for sb in range(rows // tq):
            q_rows = slice(sb * tq, (sb + 1) * tq)
            q_start = i * rows + sb * tq
            k_start = jnp.clip(q_start - HALF_WINDOW, 0, ls - win)
            k_start = pl.multiple_of(k_start, HALF_WINDOW)
            mask = jnp.abs(rel + (q_start - k_start)) <= HALF_WINDOW
            lse_tile = jnp.zeros((tq, LANES), F32)
            for h in range(HEADS_PER_GROUP):
                cols = slice(r * GROUP_WIDTH + h * HEAD_DIM, r * GROUP_WIDTH + (h + 1) * HEAD_DIM)
                kw = k_ref[0, pl.ds(k_start, win), cols]
                vw = v_ref[0, pl.ds(k_start, win), cols]
                s = lax.dot_general(q_ref[0, q_rows, cols], kw, (((1,), (1,)), ((), ())),
                                    preferred_element_type=F32)
                s = jnp.where(mask, s, NEG_INF)
                m = jnp.max(s, axis=-1, keepdims=True)
                p = jnp.exp(s - m)
                den = jnp.sum(p, axis=-1, keepdims=True)
                o = jnp.dot(p.astype(BF16), vw, preferred_element_type=F32) / den
                o_ref[0, q_rows, cols] = o.astype(BF16)
                lse_tile = jnp.where(lane == h, m + jnp.log(den), lse_tile)
            lse_ref[0, q_rows, r * LANES:(r + 1) * LANES] = lse_tile


def _attn_call(q, k, v, batch, seq, dil):
    ls = seq // dil
    rows = min(ATTN_STEP_ROWS, ls)
    res = max(1, min(dil, ATTN_STEP_ROWS // rows))
    win = min(ATTN_Q_TILE + 2 * HALF_WINDOW, ls)
    view = lambda t: t.reshape(batch, ls, dil * GROUP_WIDTH)
    qmap = lambda b, r, i: (b, i, r)
    kvmap = lambda b, r, i: (b, 0, r)
    o, lse = pl.pallas_call(
        functools.partial(_attn_kernel, ls=ls, rows=rows, res=res, win=win),
        grid=(batch, dil // res, ls // rows),
        in_specs=[
            pl.BlockSpec((1, rows, res * GROUP_WIDTH), qmap),
            pl.BlockSpec((1, ls, res * GROUP_WIDTH), kvmap),
            pl.BlockSpec((1, ls, res * GROUP_WIDTH), kvmap),
        ],
        out_specs=[
            pl.BlockSpec((1, rows, res * GROUP_WIDTH), qmap),
            pl.BlockSpec((1, rows, res * LANES), qmap),
        ],
        out_shape=[
            jax.ShapeDtypeStruct((batch, ls, dil * GROUP_WIDTH), BF16),
            jax.ShapeDtypeStruct((batch, ls, dil * LANES), F32),
        ],
        compiler_params=_cparams(("parallel", "parallel", "arbitrary")),
        name=f"banded_attn_d{dil}",
    )(view(q), view(k), view(v))
    return o.reshape(batch * ls, dil * GROUP_WIDTH), lse.reshape(batch * ls, dil * LANES)


def _route(x, gain, wr_ref, br_ref, ri_ref, rg_ref, hp_ref):
    h = _rms(x, gain)
    packed = _pack_rows(h)
    for c in range(SC_ROW_SPLIT):
        hp_ref[c] = packed[:, c * SC_ROW_WORDS:(c + 1) * SC_ROW_WORDS]
    h_hi = h.astype(BF16)
    h_lo = (h - h_hi.astype(F32)).astype(BF16)
    acc = (jnp.dot(h_hi, wr_ref[...], preferred_element_type=F32)
           + jnp.dot(h_lo, wr_ref[...], preferred_element_type=F32))
    logits = acc[:, :LANES] + acc[:, LANES:] + br_ref[...]
    lane = lax.broadcasted_iota(jnp.int32, logits.shape, 1)
    lane_f = lane.astype(F32)
    big = jnp.float32(2 * LANES)

    is_g = lane < N_EXPERT_GROUPS
    gl = jnp.where(is_g, logits, NEG_INF)
    gmax = jnp.max(gl, axis=-1, keepdims=True)
    grp = jnp.min(jnp.where(gl == gmax, lane_f, big), axis=-1, keepdims=True)
    p_grp = 1.0 / jnp.sum(jnp.where(is_g, jnp.exp(gl - gmax), 0.0), axis=-1, keepdims=True)

    lo = N_EXPERT_GROUPS + grp * EXPERTS_PER_GROUP
    in_grp = (lane_f >= lo) & (lane_f < lo + EXPERTS_PER_GROUP)
    el = jnp.where(in_grp, logits, NEG_INF)
    v1 = jnp.max(el, axis=-1, keepdims=True)
    i1 = jnp.min(jnp.where(el == v1, lane_f, big), axis=-1, keepdims=True)
    el2 = jnp.where(lane_f == i1, NEG_INF, el)
    v2 = jnp.max(el2, axis=-1, keepdims=True)
    i2 = jnp.min(jnp.where(el2 == v2, lane_f, big), axis=-1, keepdims=True)
    e21 = jnp.exp(v2 - v1)
    g1 = p_grp / (1.0 + e21)
    g2 = g1 * e21
    ri = jnp.where(lane == 0, i1 - N_EXPERT_GROUPS, jnp.where(lane == 1, i2 - N_EXPERT_GROUPS, 0.0))
    ri_ref[...] = ri.astype(jnp.int32)
    rg_ref[...] = jnp.where(lane == 0, g1, jnp.where(lane == 1, g2, 0.0))


def _route_out_specs(n, tm):
    row = lambda i: (i, 0)
    specs = [
        pl.BlockSpec((tm, D_MODEL), row),
        pl.BlockSpec((tm, LANES), row),
        pl.BlockSpec((tm, LANES), row),
        pl.BlockSpec((SC_ROW_SPLIT, tm, SC_ROW_WORDS), lambda i: (0, i, 0)),
    ]
    shapes = [
        jax.ShapeDtypeStruct((n, D_MODEL), F32),
        jax.ShapeDtypeStruct((n, LANES), jnp.int32),
        jax.ShapeDtypeStruct((n, LANES), F32),
        jax.ShapeDtypeStruct((SC_ROW_SPLIT, n, SC_ROW_WORDS), jnp.uint32),
    ]
    return specs, shapes


def _load_by_residue(in_ref, dil, scr):
    if dil == 1:
        return in_ref[...].astype(F32)
    chunks, rows, _ = scr.shape
    width = chunks * LANES
    for r in range(dil):
        for c in range(chunks):
            lanes = slice(r * width + c * LANES, r * width + (c + 1) * LANES)
            scr[c, pl.ds(r, rows // dil, stride=dil), :] = in_ref[:, lanes].astype(F32)
    return jnp.concatenate([scr[c] for c in range(chunks)], axis=1) if chunks > 1 else scr[0]


def _attn_out_kernel(o0, o1, o2, l0, l1, l2, x_ref, wo_ref, ex_ref, g_ref, wr_ref, br_ref,
                     y_ref, ri_ref, rg_ref, hp_ref, o_scr1, o_scr2, l_scr1, l_scr2):
    ls = [_load_by_residue(l, dil, scr)
          for l, dil, scr in zip((l0, l1, l2), DILATIONS, (None, l_scr1, l_scr2))]
    m = jnp.maximum(jnp.maximum(ls[0], ls[1]), ls[2])
    es = [jnp.exp(l - m) for l in ls]
    inv = 1.0 / (es[0] + es[1] + es[2])
    y = x_ref[...]
    for g, (o_ref, scr) in enumerate(zip((o0, o1, o2), (None, o_scr1, o_scr2))):
        alpha = es[g] * inv
        a_hi = alpha.astype(BF16)
        a_lo = (alpha - a_hi.astype(F32)).astype(BF16)
        a_wide = (jnp.dot(a_hi, ex_ref[...], preferred_element_type=F32)
                  + jnp.dot(a_lo, ex_ref[...], preferred_element_type=F32))
        scaled = (_load_by_residue(o_ref, DILATIONS[g], scr) * a_wide).astype(BF16)
        y = y + jnp.dot(scaled, wo_ref[g * GROUP_WIDTH:(g + 1) * GROUP_WIDTH, :],
                        preferred_element_type=F32)
    y_ref[...] = y
    _route(y, g_ref[...], wr_ref, br_ref, ri_ref, rg_ref, hp_ref)


def _attn_out_call(os_, lses, x2d, w_o, expand, gain, wr, br):
    n = x2d.shape[0]
    tm = ROW_TILE
    row = lambda i: (i, 0)
    const = lambda i: (0, 0)
    out_specs, out_shape = _route_out_specs(n, tm)
    return pl.pallas_call(
        _attn_out_kernel,
        grid=(n // tm,),
        in_specs=(
            [pl.BlockSpec((tm // dil, dil * GROUP_WIDTH), row) for dil in DILATIONS]
            + [pl.BlockSpec((tm // dil, dil * LANES), row) for dil in DILATIONS]
            + [
                pl.BlockSpec((tm, D_MODEL), row),
                pl.BlockSpec((ATTN_WIDTH, D_MODEL), const),
                pl.BlockSpec((LANES, GROUP_WIDTH), const),
                pl.BlockSpec((1, D_MODEL), const),
                pl.BlockSpec((D_MODEL, 2 * LANES), const),
                pl.BlockSpec((1, LANES), const),
            ]
        ),
        out_specs=out_specs,
        out_shape=out_shape,
        scratch_shapes=([pltpu.VMEM((GROUP_WIDTH // LANES, tm, LANES), F32)] * 2
                        + [pltpu.VMEM((1, tm, LANES), F32)] * 2),
        compiler_params=_cparams(("parallel",)),
        name="attn_out_router",
    )(*os_, *lses, x2d, w_o, expand, gain, wr, br)


N_MOE_REFS = 2 + SC_ROW_SPLIT * TOP_K


def _conv_kernel(*refs, seq, tm):
    cur, prev, nxt = (refs[b * N_MOE_REFS:(b + 1) * N_MOE_REFS] for b in range(3))
    (gm_ref, w1_ref, b1_ref, wdw_ref, bdw_ref, lng_ref, lnb_ref, w2_ref, b2_ref, gf_ref, wr_ref, br_ref,
     y_ref, ri_ref, rg_ref, hp_ref, x_scr, u_scr, c_scr) = refs[3 * N_MOE_REFS:]
    i = pl.program_id(0)
    pos0 = (i * tm) % seq
    hl = CONV_HALO
    x_scr[0:hl, :] = _combined_rows(prev[0], prev[1], prev[2:])
    x_scr[hl:hl + tm, :] = _combined_rows(cur[0], cur[1], cur[2:])
    x_scr[hl + tm:hl + tm + hl, :] = _combined_rows(nxt[0], nxt[1], nxt[2:])
    h = _rms(x_scr[...], gm_ref[...]).astype(BF16)
    a = jnp.dot(h, w1_ref[...], preferred_element_type=F32) + b1_ref[...]
    u = a[:, :D_MODEL] * jax.nn.sigmoid(a[:, D_MODEL:])
    r = lax.broadcasted_iota(jnp.int32, (tm + 2 * hl, 1), 0)
    outside = ((r < hl) & (pos0 == 0)) | ((r >= hl + tm) & (pos0 + tm == seq))
    u_scr[...] = jnp.where(outside, 0.0, u)

    rc = 64
    lc = LANES
    first = hl - CONV_WIDTH // 2
    for j in range(tm // rc):
        for c in range(D_MODEL // lc):
            cols = slice(c * lc, (c + 1) * lc)
            acc = jnp.zeros((rc, lc), F32) + bdw_ref[:, cols]
            for b in range(SUBLANES):
                part = None
                for t in range(CONV_WIDTH):
                    if (t + first) % SUBLANES != b:
                        continue
                    off = j * rc + (t + first) - b
                    term = u_scr[off:off + rc + SUBLANES, cols] * wdw_ref[t:t + 1, cols]
                    part = term if part is None else part + term
                acc = acc + part[b:b + rc]
            c_scr[j * rc:(j + 1) * rc, cols] = acc
    cv = c_scr[...]
    mu = jnp.mean(cv, axis=-1, keepdims=True)
    xc = cv - mu
    var = jnp.mean(xc * xc, axis=-1, keepdims=True)
    ln = xc * lax.rsqrt(var + EPS) * lng_ref[...] + lnb_ref[...]
    act = (ln * jax.nn.sigmoid(ln)).astype(BF16)
    y = x_scr[hl:hl + tm, :] + jnp.dot(act, w2_ref[...], preferred_element_type=F32) + b2_ref[...]
    y_ref[...] = y
    _route(y, gf_ref[...], wr_ref, br_ref, ri_ref, rg_ref, hp_ref)


def _conv_call(y2d, rg, rows, seq, gm, w1, b1, wdw, bdw, lng, lnb, w2, b2, gf, wr, br):
    n = y2d.shape[0]
    tm = ROW_TILE
    hl = CONV_HALO
    per = tm // hl
    tiles = n // tm
    const = lambda i: (0, 0)
    cur_blk = lambda i: i
    prev_blk = lambda i: jnp.maximum(i * per - 1, 0)
    next_blk = lambda i: jnp.minimum((i + 1) * per, n // hl - 1)

    def moe_specs(rows_blk, blk_of, blocks_per_piece):
        at = lambda off: (lambda i: (blk_of(i) + off, 0))
        return ([pl.BlockSpec((rows_blk, D_MODEL), at(0)), pl.BlockSpec((rows_blk, LANES), at(0))]
                + [pl.BlockSpec((rows_blk, SC_ROW_WORDS), at(j * blocks_per_piece))
                   for j in range(SC_ROW_SPLIT * TOP_K)])

    moe_args = [y2d, rg] + [rows] * (SC_ROW_SPLIT * TOP_K)
    out_specs, out_shape = _route_out_specs(n, tm)
    return pl.pallas_call(
        functools.partial(_conv_kernel, seq=seq, tm=tm),
        grid=(tiles,),
        in_specs=(
            moe_specs(tm, cur_blk, tiles) + moe_specs(hl, prev_blk, tiles * per)
            + moe_specs(hl, next_blk, tiles * per)
            + [
                pl.BlockSpec((1, D_MODEL), const),
                pl.BlockSpec((D_MODEL, 2 * D_MODEL), const),
                pl.BlockSpec((1, 2 * D_MODEL), const),
                pl.BlockSpec((CONV_WIDTH, D_MODEL), const),
                pl.BlockSpec((1, D_MODEL), const),
                pl.BlockSpec((1, D_MODEL), const),
                pl.BlockSpec((1, D_MODEL), const),
                pl.BlockSpec((D_MODEL, D_MODEL), const),
                pl.BlockSpec((1, D_MODEL), const),
                pl.BlockSpec((1, D_MODEL), const),
                pl.BlockSpec((D_MODEL, 2 * LANES), const),
                pl.BlockSpec((1, LANES), const),
            ]
        ),
        out_specs=out_specs,
        out_shape=out_shape,
        scratch_shapes=[
            pltpu.VMEM((tm + 2 * hl, D_MODEL), F32),
            pltpu.VMEM((tm + 2 * hl, D_MODEL), F32),
            pltpu.VMEM((tm, D_MODEL), F32),
        ],
        compiler_params=_cparams(("parallel",)),
        name="conv_module_router",
    )(*moe_args, *moe_args, *moe_args, gm, w1, b1, wdw, bdw, lng, lnb, w2, b2, gf, wr, br)


def _sc_mesh():
    return plsc.VectorSubcoreMesh(core_axis_name="core", subcore_axis_name="subcore")


def _sc_scatter_rows(src, idx, n_src_windows, out_rows):
    m = idx.shape[1]
    width = src.shape[1]
    nsw = n_src_windows

    @functools.partial(pl.kernel, out_type=jax.ShapeDtypeStruct((out_rows, width), src.dtype),
                       mesh=_sc_mesh(), name="sc_scatter_rows")
    def scatter(src_hbm, idx_hbm, out_hbm):
        def body(src_vmem, idx_vmem):
            pltpu.sync_copy(src_vmem, out_hbm.at[idx_vmem.at[0]])

        pltpu.emit_pipeline(
            body,
            grid=(m // SC_WINDOW,),
            in_specs=[
                pl.BlockSpec((SC_WINDOW, width), lambda w: ((w // (TOP_K * nsw)) * nsw + w % nsw, 0)),
                pl.BlockSpec((1, SC_WINDOW), lambda w: (0, w)),
            ],
            out_specs=[],
            core_axis_name=("core", "subcore"),
            dimension_semantics=(pltpu.PARALLEL,),
        )(src_hbm, idx_hbm)

    return scatter(src, idx)


def _sc_gather_rows(table, idx):
    m = idx.shape[1]
    width = table.shape[1]

    @functools.partial(pl.kernel, out_type=jax.ShapeDtypeStruct((m, width), table.dtype),
                       mesh=_sc_mesh(), name="sc_gather_rows")
    def gather(table_hbm, idx_hbm, out_hbm):
        def body(idx_vmem, out_vmem):
            pltpu.sync_copy(table_hbm.at[idx_vmem.at[0]], out_vmem)

        pltpu.emit_pipeline(
            body,
            grid=(m // SC_WINDOW,),
            in_specs=[pl.BlockSpec((1, SC_WINDOW), lambda w: (0, w))],
            out_specs=[pl.BlockSpec((SC_WINDOW, width), lambda w: (w, 0))],
            core_axis_name=("core", "subcore"),
            dimension_semantics=(pltpu.PARALLEL,),
        )(idx_hbm, out_hbm)

    return gather(table, idx)


def _expert_kernel(be_ref, nv_ref, xs_ref, wg_ref, wu_ref, wd_ref, yb_ref, wg_s, wu_s, wd_s):
    i = pl.program_id(0)

    @pl.when((i == 0) | (be_ref[i] != be_ref[jnp.maximum(i - 1, 0)]))
    def _():
        wg_s[...] = wg_ref[0, 0].astype(BF16)
        wu_s[...] = wu_ref[0, 0].astype(BF16)
        wd_s[...] = wd_ref[0, 0].astype(BF16)

    @pl.when(nv_ref[i] > 0)
    def _():
        words = jnp.concatenate([xs_ref[c] for c in range(SC_ROW_SPLIT)], axis=1)
        row = lax.broadcasted_iota(jnp.int32, (words.shape[0], 1), 0)
        words = jnp.where(row < nv_ref[i], words, jnp.uint32(0))
        hi, lo = _unpack_rows(words)
        xb = jnp.concatenate([hi.astype(BF16), lo.astype(BF16)], axis=1)
        g = jnp.dot(xb, wg_s[...], preferred_element_type=F32)
        u = jnp.dot(xb, wu_s[...], preferred_element_type=F32)
        hid = (g * jax.nn.sigmoid(g) * u).astype(BF16)
        packed = _pack_rows(jnp.dot(hid, wd_s[...], preferred_element_type=F32))
        for c in range(SC_ROW_SPLIT):
            yb_ref[c] = packed[:, c * SC_ROW_WORDS:(c + 1) * SC_ROW_WORDS]


def _expert_call(block_expert, block_valid, xs, layer, w_gate, w_up, w_down):
    p_total = xs.shape[1]
    blk = MOE_BLOCK
    wmap = lambda i, be, nv: (layer, be[i], 0, 0)
    rows = lambda i, be, nv: (0, i, 0)
    return pl.pallas_call(
        _expert_kernel,
        grid_spec=pltpu.PrefetchScalarGridSpec(
            num_scalar_prefetch=2,
            grid=(p_total // blk,),
            in_specs=[
                pl.BlockSpec((SC_ROW_SPLIT, blk, SC_ROW_WORDS), rows),
                pl.BlockSpec((1, 1, D_MODEL, D_EXPERT), wmap),
                pl.BlockSpec((1, 1, D_MODEL, D_EXPERT), wmap),
                pl.BlockSpec((1, 1, D_EXPERT, D_MODEL), wmap),
            ],
            out_specs=pl.BlockSpec((SC_ROW_SPLIT, blk, SC_ROW_WORDS), rows),
            scratch_shapes=[
                pltpu.VMEM((D_MODEL, D_EXPERT), BF16),
                pltpu.VMEM((D_MODEL, D_EXPERT), BF16),
                pltpu.VMEM((D_EXPERT, D_MODEL), BF16),
            ],
        ),
        out_shape=jax.ShapeDtypeStruct((SC_ROW_SPLIT, p_total, SC_ROW_WORDS), jnp.uint32),
        compiler_params=_cparams(("arbitrary",)),
        name="moe_experts",
    )(block_expert, block_valid, xs, w_gate, w_up, w_down)


def _combined_rows(y_ref, rg_ref, piece_refs):
    rg = rg_ref[...]
    his, los = [], []
    for c in range(SC_ROW_SPLIT):
        hi0, lo0 = _unpack_rows(piece_refs[c * TOP_K][...])
        hi1, lo1 = _unpack_rows(piece_refs[c * TOP_K + 1][...])
        his.append(rg[:, 0:1] * hi0 + rg[:, 1:2] * hi1)
        los.append(rg[:, 0:1] * lo0 + rg[:, 1:2] * lo1)
    return y_ref[...] + jnp.concatenate(his + los, axis=1)


def _combine_kernel(y_ref, rg_ref, *refs):
    refs[-1][...] = _combined_rows(y_ref, rg_ref, refs[:-1])


def _combine_call(y2d, rg, rows):
    n = y2d.shape[0]
    tm = ROW_TILE
    tiles = n // tm
    return pl.pallas_call(
        _combine_kernel,
        grid=(tiles,),
        in_specs=[
            pl.BlockSpec((tm, D_MODEL), lambda i: (i, 0)),
            pl.BlockSpec((tm, LANES), lambda i: (i, 0)),
        ] + [
            pl.BlockSpec((tm, SC_ROW_WORDS), functools.partial(lambda i, j: (i + j * tiles, 0), j=j))
            for j in range(SC_ROW_SPLIT * TOP_K)
        ],
        out_specs=pl.BlockSpec((tm, D_MODEL), lambda i: (i, 0)),
        out_shape=jax.ShapeDtypeStruct((n, D_MODEL), F32),
        compiler_params=_cparams(("parallel",)),
        name="moe_combine",
    )(y2d, rg, *([rows] * (SC_ROW_SPLIT * TOP_K)))


def _slot_tables(ri, n):
    e = ri[:, :TOP_K]
    onehot = (e[:, :, None] == jnp.arange(N_EXPERTS, dtype=jnp.int32)).astype(jnp.int32)
    tok_cnt = onehot.sum(axis=1)
    t = 256
    cnt_t = tok_cnt.reshape(n // t, t, N_EXPERTS)
    tri = (jnp.arange(t)[:, None] > jnp.arange(t)[None, :]).astype(F32)
    within = jnp.einsum("ts,nsc->ntc", tri, cnt_t.astype(F32)).astype(jnp.int32)
    tile_sum = cnt_t.sum(axis=1)
    tile_base = jnp.cumsum(tile_sum, axis=0) - tile_sum
    rank = (within + tile_base[:, None, :]).reshape(n, N_EXPERTS)
    counts = tile_sum.sum(axis=0)
    padded = ((counts + MOE_BLOCK - 1) // MOE_BLOCK) * MOE_BLOCK
    pad_end = jnp.cumsum(padded)
    pad_start = pad_end - padded
    slot = ((rank + pad_start)[:, None, :] * onehot).sum(axis=-1)
    n_blocks = (n * TOP_K) // MOE_BLOCK + N_EXPERTS
    block_start = jnp.arange(n_blocks, dtype=jnp.int32) * MOE_BLOCK
    block_expert = jnp.minimum((pad_end[None, :] <= block_start[:, None]).sum(axis=-1),
                               N_EXPERTS - 1).astype(jnp.int32)
    used_end = (pad_start + counts)[block_expert]
    block_valid = jnp.clip(used_end - block_start, 0, MOE_BLOCK).astype(jnp.int32)
    p_total = n_blocks * MOE_BLOCK
    slot_km = slot.T.reshape(1, TOP_K * n).astype(jnp.int32)
    idx = jnp.concatenate([slot_km + c * p_total for c in range(SC_ROW_SPLIT)], axis=1)
    return idx, block_expert, block_valid, p_total


def _moe_rows(n, ri, hp, layer, experts):
    idx, block_expert, block_valid, p_total = _slot_tables(ri, n)
    xs = _sc_scatter_rows(hp.reshape(SC_ROW_SPLIT * n, SC_ROW_WORDS), idx, n // SC_WINDOW,
                          SC_ROW_SPLIT * p_total)
    yb = _expert_call(block_expert, block_valid, xs.reshape(SC_ROW_SPLIT, p_total, SC_ROW_WORDS),
                      layer, *experts)
    return _sc_gather_rows(yb.reshape(SC_ROW_SPLIT * p_total, SC_ROW_WORDS), idx)


def _rope_tables(seq):
    pos = jnp.arange(seq, dtype=F32)
    inv_freq = ROPE_THETA ** (-jnp.arange(0, ROPE_DIM, 2, dtype=F32) / ROPE_DIM)
    ang = pos[:, None] * inv_freq[None, :]
    cos = jnp.cos(ang)
    sin = jnp.sin(ang)
    half = ROPE_DIM // 2
    rest = HEAD_DIM - ROPE_DIM
    cos_h = jnp.concatenate([cos, cos, jnp.ones((seq, rest), F32)], axis=1)
    sa_h = jnp.concatenate([-sin, jnp.zeros((seq, half + rest), F32)], axis=1)
    sb_h = jnp.concatenate([jnp.zeros((seq, half), F32), sin, jnp.zeros((seq, rest), F32)], axis=1)
    tile = lambda t: jnp.tile(t, (1, HEADS_PER_GROUP))
    return tile(cos_h), tile(sa_h), tile(sb_h)


def _router_weights(w_rg, b_rg, w_re, b_re):
    w = jnp.concatenate([w_rg, w_re], axis=1)
    w = jnp.pad(w, ((0, 0), (0, LANES - w.shape[1])))
    w_hi = w.astype(BF16)
    w_lo = (w - w_hi.astype(F32)).astype(BF16)
    b = jnp.pad(jnp.concatenate([b_rg, b_re]), (0, LANES - N_EXPERT_GROUPS - N_EXPERTS))
    return jnp.concatenate([w_hi, w_lo], axis=1), b.reshape(1, LANES)


def _trunk(x, p):
    batch, seq, d = x.shape
    n = batch * seq
    x2d = x.reshape(n, d)
    cos, sa, sb = _rope_tables(seq)

    qkv = _qkv_call(x2d, seq, p["ln_mix0"], p["w_qkv"], p["mseg"], p["qg"], p["kg"], cos, sa, sb)
    os_, lses = [], []
    for g, dil in enumerate(DILATIONS):
        o, lse = _attn_call(qkv[g], qkv[3 + g], qkv[6 + g], batch, seq, dil)
        os_.append(o)
        lses.append(lse)
    y, ri, rg, hp = _attn_out_call(os_, lses, x2d, p["w_o"], p["expand"], p["ln_ffn0"], p["wr0"], p["br0"])
    rows = _moe_rows(n, ri, hp, 0, p["experts"])

    y, ri, rg, hp = _conv_call(y, rg, rows, seq, p["ln_mix1"], p["w_pw1"], p["b_pw1"], p["w_dw"], p["b_dw"],
                               p["conv_ln_g"], p["conv_ln_b"], p["w_pw2"], p["b_pw2"],
                               p["ln_ffn1"], p["wr1"], p["br1"])
    rows = _moe_rows(n, ri, hp, 1, p["experts"])
    return _combine_call(y, rg, rows).reshape(batch, seq, d)


def kernel(x_prompt, x_sample, ln_mix, ln_ffn, w_qkv, q_gain, k_gain, w_o, w_pw1, b_pw1, w_dw, b_dw,
           conv_ln_g, conv_ln_b, w_pw2, b_pw2, w_router_group, b_router_group, w_router_expert,
           b_router_expert, w_gate, w_up, w_down):
    row = lambda v: v.reshape(1, -1).astype(F32)
    n_heads = N_GROUPS * HEADS_PER_GROUP
    head_id = np.arange(ATTN_WIDTH) // HEAD_DIM
    mseg = jnp.asarray((head_id[:, None] == head_id[None, :]) / HEAD_DIM, BF16)
    slot_id = np.arange(GROUP_WIDTH) // HEAD_DIM
    expand = jnp.asarray(np.arange(LANES)[:, None] == slot_id[None, :], BF16)
    wr0, br0 = _router_weights(w_router_group[0], b_router_group[0], w_router_expert[0], b_router_expert[0])
    wr1, br1 = _router_weights(w_router_group[1], b_router_group[1], w_router_expert[1], b_router_expert[1])
    p = dict(
        ln_mix0=row(ln_mix[0]), ln_mix1=row(ln_mix[1]), ln_ffn0=row(ln_ffn[0]), ln_ffn1=row(ln_ffn[1]),
        w_qkv=w_qkv[0].astype(BF16), mseg=mseg, expand=expand,
        qg=row(jnp.tile(q_gain[0], n_heads) * HEAD_DIM ** -0.5), kg=row(jnp.tile(k_gain[0], n_heads)),
        w_o=w_o[0].astype(BF16),
        w_pw1=w_pw1[0].astype(BF16), b_pw1=row(b_pw1[0]), w_dw=w_dw[0].astype(F32), b_dw=row(b_dw[0]),
        conv_ln_g=row(conv_ln_g[0]), conv_ln_b=row(conv_ln_b[0]),
        w_pw2=w_pw2[0].astype(BF16), b_pw2=row(b_pw2[0]),
        wr0=wr0, br0=br0, wr1=wr1, br1=br1,
        experts=(w_gate, w_up, w_down),
    )
    return _trunk(x_prompt, p), _trunk(x_sample, p)
```

```python
import functools

import jax
import jax.numpy as jnp
import numpy as np
from jax import lax
from jax.experimental import pallas as pl
from jax.experimental.pallas import tpu as pltpu
from jax.experimental.pallas import tpu_sc as plsc

D_MODEL = 1024
HEAD_DIM = 64
HEADS_PER_GROUP = 4
DILATIONS = (1, 4, 16)
HALF_WINDOW = 64
N_GROUPS = len(DILATIONS)
GROUP_WIDTH = HEADS_PER_GROUP * HEAD_DIM
ATTN_WIDTH = N_GROUPS * GROUP_WIDTH
ROPE_DIM = HEAD_DIM // 4
ROPE_THETA = 500000.0
CONV_WIDTH = 31
CONV_HALO = 16
N_EXPERT_GROUPS = 4
EXPERTS_PER_GROUP = 8
N_EXPERTS = N_EXPERT_GROUPS * EXPERTS_PER_GROUP
TOP_K = 2
D_EXPERT = 512
EPS = 1e-6
NEG_INF = -1e30
LOG2E = 1.4426950408889634
LN2 = 0.6931471805599453

LANES = 128
SUBLANES = 8
MXU_WIDTH = 256
ROW_TILE = 512
ATTN_Q_TILE = 128
ATTN_STEP_ROWS = 512
MOE_BLOCK = 512
VMEM_LIMIT = 56 * 1024 * 1024

PACKED_WIDTH = D_MODEL // 2
SC_WINDOW = 128
SC_ROW_SPLIT = 2
SC_ROW_WORDS = PACKED_WIDTH // SC_ROW_SPLIT

F32 = jnp.float32
BF16 = jnp.bfloat16


def _cparams(sem):
    return pltpu.CompilerParams(dimension_semantics=sem, vmem_limit_bytes=VMEM_LIMIT)


def _rms(x, gain):
    return x * lax.rsqrt(jnp.mean(x * x, axis=-1, keepdims=True) + EPS) * gain


def _pack_rows(x):
    bits = pltpu.bitcast(x.astype(BF16).astype(F32), jnp.uint32)
    half = x.shape[1] // 2
    return bits[:, :half] | (bits[:, half:] >> 16)


def _unpack_rows(w):
    hi = pltpu.bitcast(w & jnp.uint32(0xFFFF0000), F32)
    lo = pltpu.bitcast(w << 16, F32)
    return hi, lo


def _store_by_residue(out_ref, val, dil, scr):
    if dil == 1:
        out_ref[...] = val.astype(out_ref.dtype)
        return
    rows, width = val.shape
    for c in range(width // LANES):
        scr[c] = val[:, c * LANES:(c + 1) * LANES]
    for r in range(dil):
        for c in range(width // LANES):
            lanes = slice(r * width + c * LANES, r * width + (c + 1) * LANES)
            out_ref[:, lanes] = scr[c, pl.ds(r, rows // dil, stride=dil), :].astype(out_ref.dtype)


def _qkv_kernel(x_ref, g_ref, w_ref, mseg_ref, qg_ref, kg_ref, cos_ref, sa_ref, sb_ref,
                q0, q1, q2, k0, k1, k2, v0, v1, v2, scr):
    h = _rms(x_ref[...], g_ref[...]).astype(BF16)
    cos = cos_ref[...]
    sa = sa_ref[...]
    sb = sb_ref[...]

    def head_norm_rope(t, gain, outs):
        ms = jnp.dot((t * t).astype(BF16), mseg_ref[...], preferred_element_type=F32)
        tn = t * lax.rsqrt(ms + EPS) * gain
        for g in range(N_GROUPS):
            c = tn[:, g * GROUP_WIDTH:(g + 1) * GROUP_WIDTH]
            r = (c * cos + pltpu.roll(c, GROUP_WIDTH - ROPE_DIM // 2, 1) * sa
                 + pltpu.roll(c, ROPE_DIM // 2, 1) * sb)
            _store_by_residue(outs[g], r, DILATIONS[g], scr)

    q = jnp.dot(h, w_ref[:, 0:ATTN_WIDTH], preferred_element_type=F32)
    head_norm_rope(q, qg_ref[...], (q0, q1, q2))
    k = jnp.dot(h, w_ref[:, ATTN_WIDTH:2 * ATTN_WIDTH], preferred_element_type=F32)
    head_norm_rope(k, kg_ref[...], (k0, k1, k2))
    v = jnp.dot(h, w_ref[:, 2 * ATTN_WIDTH:3 * ATTN_WIDTH], preferred_element_type=F32)
    for g, o in enumerate((v0, v1, v2)):
        _store_by_residue(o, v[:, g * GROUP_WIDTH:(g + 1) * GROUP_WIDTH], DILATIONS[g], scr)


def _qkv_call(x2d, seq, gain, w_qkv, mseg, qg, kg, cos, sa, sb):
    n = x2d.shape[0]
    tm = ROW_TILE
    tiles_per_seq = seq // tm
    row = lambda i: (i, 0)
    const = lambda i: (0, 0)
    tab = lambda i: (i % tiles_per_seq, 0)
    out_specs = [pl.BlockSpec((tm // dil, dil * GROUP_WIDTH), row) for dil in DILATIONS] * 3
    out_shape = [jax.ShapeDtypeStruct((n // dil, dil * GROUP_WIDTH), BF16) for dil in DILATIONS] * 3
    return pl.pallas_call(
        _qkv_kernel,
        grid=(n // tm,),
        in_specs=[
            pl.BlockSpec((tm, D_MODEL), row),
            pl.BlockSpec((1, D_MODEL), const),
            pl.BlockSpec((D_MODEL, 3 * ATTN_WIDTH), const),
            pl.BlockSpec((ATTN_WIDTH, ATTN_WIDTH), const),
            pl.BlockSpec((1, ATTN_WIDTH), const),
            pl.BlockSpec((1, ATTN_WIDTH), const),
            pl.BlockSpec((tm, GROUP_WIDTH), tab),
            pl.BlockSpec((tm, GROUP_WIDTH), tab),
            pl.BlockSpec((tm, GROUP_WIDTH), tab),
        ],
        out_specs=out_specs,
        out_shape=out_shape,
        scratch_shapes=[pltpu.VMEM((GROUP_WIDTH // LANES, tm, LANES), F32)],
        compiler_params=_cparams(("parallel",)),
        name="qkv_proj",
    )(x2d, gain, w_qkv, mseg, qg, kg, cos, sa, sb)


def _attn_kernel(q_ref, k_ref, v_ref, o_ref, lse_ref, *, ls, rows, res, win):
    i = pl.program_id(2)
    tq = ATTN_Q_TILE
    rel = (lax.broadcasted_iota(jnp.int32, (tq, win), 0) - lax.broadcasted_iota(jnp.int32, (tq, win), 1))
    lane = lax.broadcasted_iota(jnp.int32, (tq, LANES), 1)
    lane_kv = lax.broadcasted_iota(jnp.int32, (win, LANES), 1)

    blocks = []
    for r in range(res):
        for sb in range(rows // tq):
            q_start = i * rows + sb * tq
            k_start = pl.multiple_of(jnp.clip(q_start - HALF_WINDOW, 0, ls - win), HALF_WINDOW)
            mask = jnp.abs(rel + (q_start - k_start)) <= HALF_WINDOW
            blocks.append((r, slice(sb * tq, (sb + 1) * tq), k_start, mask))
    chains = [(blk, h) for blk in blocks for h in range(HEADS_PER_GROUP)]

    def cols(r, h):
        return slice(r * GROUP_WIDTH + h * HEAD_DIM, r * GROUP_WIDTH + (h + 1) * HEAD_DIM)

    scores = [lax.dot_general(q_ref[0, q_rows, cols(r, h)], k_ref[0, pl.ds(k_start, win), cols(r, h)],
                              (((1,), (1,)), ((), ())), preferred_element_type=F32)
              for (r, q_rows, k_start, _), h in chains]
    maxes, probs = [], []
    for ((_, _, _, mask), _), s in zip(chains, scores):
        s = jnp.where(mask, s, NEG_INF)
        m = jnp.max(s, axis=-1, keepdims=True)
        maxes.append(m)
        probs.append(jnp.exp2(s - m).astype(BF16))
    outs = []
    for ((r, _, k_start, _), h), p in zip(chains, probs):
        pair = slice(r * GROUP_WIDTH + (h // 2) * LANES, r * GROUP_WIDTH + (h // 2 + 1) * LANES)
        own_half = (lane_kv < HEAD_DIM) if h % 2 == 0 else (lane_kv >= HEAD_DIM)
        v_ext = jnp.where(own_half, v_ref[0, pl.ds(k_start, win), pair], jnp.ones((), BF16))
        outs.append(jnp.dot(p, v_ext, preferred_element_type=F32))
    lse_tiles = {}
    for ((r, q_rows, _, _), h), o_ext, m in zip(chains, outs, maxes):
        den_ext = pltpu.roll(o_ext, HEAD_DIM, 1)
        own = slice((h % 2) * HEAD_DIM, (h % 2 + 1) * HEAD_DIM)
        o_ref[0, q_rows, cols(r, h)] = (o_ext / den_ext)[:, own].astype(BF16)
        den_at_h = den_ext if h % 2 == 0 else o_ext
        key = (r, q_rows.start)
        lse_tiles[key] = jnp.where(lane == h, (m + jnp.log2(den_at_h)) * LN2,
                                   lse_tiles.get(key, jnp.zeros((tq, LANES), F32)))
    for (r, q0), tile in lse_tiles.items():
        lse_ref[0, q0:q0 + tq, r * LANES:(r + 1) * LANES] = tile


def _attn_call(q, k, v, batch, seq, dil):
    ls = seq // dil
    rows = min(ATTN_STEP_ROWS, ls)
    res = max(1, min(dil, ATTN_STEP_ROWS // rows))
    win = min(ATTN_Q_TILE + 2 * HALF_WINDOW, ls)
    view = lambda t: t.reshape(batch, ls, dil * GROUP_WIDTH)
    qmap = lambda b, r, i: (b, i, r)
    kvmap = lambda b, r, i: (b, 0, r)
    o, lse = pl.pallas_call(
        functools.partial(_attn_kernel, ls=ls, rows=rows, res=res, win=win),
        grid=(batch, dil // res, ls // rows),
        in_specs=[
            pl.BlockSpec((1, rows, res * GROUP_WIDTH), qmap),
            pl.BlockSpec((1, ls, res * GROUP_WIDTH), kvmap),
            pl.BlockSpec((1, ls, res * GROUP_WIDTH), kvmap),
        ],
        out_specs=[
            pl.BlockSpec((1, rows, res * GROUP_WIDTH), qmap),
            pl.BlockSpec((1, rows, res * LANES), qmap),
        ],
        out_shape=[
            jax.ShapeDtypeStruct((batch, ls, dil * GROUP_WIDTH), BF16),
            jax.ShapeDtypeStruct((batch, ls, dil * LANES), F32),
        ],
        compiler_params=_cparams(("parallel", "parallel", "arbitrary")),
        name=f"banded_attn_d{dil}",
    )(view(q), view(k), view(v))
    return o.reshape(batch * ls, dil * GROUP_WIDTH), lse.reshape(batch * ls, dil * LANES)


def _route(x, gain, wr_ref, br_ref, ri_ref, rg_ref, hp_ref):
    h = _rms(x, gain)
    packed = _pack_rows(h)
    for c in range(SC_ROW_SPLIT):
        hp_ref[c] = packed[:, c * SC_ROW_WORDS:(c + 1) * SC_ROW_WORDS]
    h_hi = h.astype(BF16)
    h_lo = (h - h_hi.astype(F32)).astype(BF16)
    acc = (jnp.dot(h_hi, wr_ref[...], preferred_element_type=F32)
           + jnp.dot(h_lo, wr_ref[...], preferred_element_type=F32))
    logits = acc[:, :LANES] + acc[:, LANES:] + br_ref[...]
    lane = lax.broadcasted_iota(jnp.int32, logits.shape, 1)
    lane_f = lane.astype(F32)
    big = jnp.float32(2 * LANES)

    is_g = lane < N_EXPERT_GROUPS
    gl = jnp.where(is_g, logits, NEG_INF)
    gmax = jnp.max(gl, axis=-1, keepdims=True)
    grp = jnp.min(jnp.where(gl == gmax, lane_f, big), axis=-1, keepdims=True)
    p_grp = 1.0 / jnp.sum(jnp.where(is_g, jnp.exp(gl - gmax), 0.0), axis=-1, keepdims=True)

    lo = N_EXPERT_GROUPS + grp * EXPERTS_PER_GROUP
    in_grp = (lane_f >= lo) & (lane_f < lo + EXPERTS_PER_GROUP)
    el = jnp.where(in_grp, logits, NEG_INF)
    v1 = jnp.max(el, axis=-1, keepdims=True)
    i1 = jnp.min(jnp.where(el == v1, lane_f, big), axis=-1, keepdims=True)
    el2 = jnp.where(lane_f == i1, NEG_INF, el)
    v2 = jnp.max(el2, axis=-1, keepdims=True)
    i2 = jnp.min(jnp.where(el2 == v2, lane_f, big), axis=-1, keepdims=True)
    e21 = jnp.exp(v2 - v1)
    g1 = p_grp / (1.0 + e21)
    g2 = g1 * e21
    ri = jnp.where(lane == 0, i1 - N_EXPERT_GROUPS, jnp.where(lane == 1, i2 - N_EXPERT_GROUPS, 0.0))
    ri_ref[...] = ri.astype(jnp.int32)
    rg_ref[...] = jnp.where(lane == 0, g1, jnp.where(lane == 1, g2, 0.0))


def _route_out_specs(n, tm):
    row = lambda i: (i, 0)
    specs = [
        pl.BlockSpec((tm, D_MODEL), row),
        pl.BlockSpec((tm, LANES), row),
        pl.BlockSpec((tm, LANES), row),
        pl.BlockSpec((SC_ROW_SPLIT, tm, SC_ROW_WORDS), lambda i: (0, i, 0)),
    ]
    shapes = [
        jax.ShapeDtypeStruct((n, D_MODEL), F32),
        jax.ShapeDtypeStruct((n, LANES), jnp.int32),
        jax.ShapeDtypeStruct((n, LANES), F32),
        jax.ShapeDtypeStruct((SC_ROW_SPLIT, n, SC_ROW_WORDS), jnp.uint32),
    ]
    return specs, shapes


def _load_by_residue(in_ref, dil, scr):
    if dil == 1:
        return in_ref[...].astype(F32)
    chunks, rows, _ = scr.shape
    width = chunks * LANES
    for r in range(dil):
        for c in range(chunks):
            lanes = slice(r * width + c * LANES, r * width + (c + 1) * LANES)
            scr[c, pl.ds(r, rows // dil, stride=dil), :] = in_ref[:, lanes].astype(F32)
    return jnp.concatenate([scr[c] for c in range(chunks)], axis=1) if chunks > 1 else scr[0]


def _attn_out_kernel(o0, o1, o2, l0, l1, l2, x_ref, wo_ref, ex_ref, g_ref, wr_ref, br_ref,
                     y_ref, ri_ref, rg_ref, hp_ref, o_scr1, o_scr2, l_scr1, l_scr2):
    ls = [_load_by_residue(l, dil, scr)
          for l, dil, scr in zip((l0, l1, l2), DILATIONS, (None, l_scr1, l_scr2))]
    m = jnp.maximum(jnp.maximum(ls[0], ls[1]), ls[2])
    es = [jnp.exp(l - m) for l in ls]
    inv = 1.0 / (es[0] + es[1] + es[2])
    y = x_ref[...]
    for g, (o_ref, scr) in enumerate(zip((o0, o1, o2), (None, o_scr1, o_scr2))):
        alpha = es[g] * inv
        a_hi = alpha.astype(BF16)
        a_lo = (alpha - a_hi.astype(F32)).astype(BF16)
        a_wide = (jnp.dot(a_hi, ex_ref[...], preferred_element_type=F32)
                  + jnp.dot(a_lo, ex_ref[...], preferred_element_type=F32))
        scaled = (_load_by_residue(o_ref, DILATIONS[g], scr) * a_wide).astype(BF16)
        y = y + jnp.dot(scaled, wo_ref[g * GROUP_WIDTH:(g + 1) * GROUP_WIDTH, :],
                        preferred_element_type=F32)
    y_ref[...] = y
    _route(y, g_ref[...], wr_ref, br_ref, ri_ref, rg_ref, hp_ref)


def _attn_out_call(os_, lses, x2d, w_o, expand, gain, wr, br):
    n = x2d.shape[0]
    tm = ROW_TILE
    row = lambda i: (i, 0)
    const = lambda i: (0, 0)
    out_specs, out_shape = _route_out_specs(n, tm)
    return pl.pallas_call(
        _attn_out_kernel,
        grid=(n // tm,),
        in_specs=(
            [pl.BlockSpec((tm // dil, dil * GROUP_WIDTH), row) for dil in DILATIONS]
            + [pl.BlockSpec((tm // dil, dil * LANES), row) for dil in DILATIONS]
            + [
                pl.BlockSpec((tm, D_MODEL), row),
                pl.BlockSpec((ATTN_WIDTH, D_MODEL), const),
                pl.BlockSpec((LANES, GROUP_WIDTH), const),
                pl.BlockSpec((1, D_MODEL), const),
                pl.BlockSpec((D_MODEL, 2 * LANES), const),
                pl.BlockSpec((1, LANES), const),
            ]
        ),
        out_specs=out_specs,
        out_shape=out_shape,
        scratch_shapes=([pltpu.VMEM((GROUP_WIDTH // LANES, tm, LANES), F32)] * 2
                        + [pltpu.VMEM((1, tm, LANES), F32)] * 2),
        compiler_params=_cparams(("parallel",)),
        name="attn_out_router",
    )(*os_, *lses, x2d, w_o, expand, gain, wr, br)


N_MOE_REFS = 2 + SC_ROW_SPLIT * TOP_K


def _conv_kernel(*refs, seq, tm):
    cur, prev, nxt = (refs[b * N_MOE_REFS:(b + 1) * N_MOE_REFS] for b in range(3))
    (gm_ref, w1_ref, b1_ref, wdw_ref, bdw_ref, lng_ref, lnb_ref, w2_ref, b2_ref, gf_ref, wr_ref, br_ref,
     y_ref, ri_ref, rg_ref, hp_ref, x_scr, u_scr, c_scr) = refs[3 * N_MOE_REFS:]
    i = pl.program_id(0)
    pos0 = (i * tm) % seq
    hl = CONV_HALO
    x_scr[0:hl, :] = _combined_rows(prev[0], prev[1], prev[2:])
    x_scr[hl:hl + tm, :] = _combined_rows(cur[0], cur[1], cur[2:])
    x_scr[hl + tm:hl + tm + hl, :] = _combined_rows(nxt[0], nxt[1], nxt[2:])
    h = _rms(x_scr[...], gm_ref[...]).astype(BF16)
    a = jnp.dot(h, w1_ref[...], preferred_element_type=F32) + b1_ref[...]
    u = a[:, :D_MODEL] * jax.nn.sigmoid(a[:, D_MODEL:])
    r = lax.broadcasted_iota(jnp.int32, (tm + 2 * hl, 1), 0)
    outside = ((r < hl) & (pos0 == 0)) | ((r >= hl + tm) & (pos0 + tm == seq))
    u_scr[...] = jnp.where(outside, 0.0, u)

    rc = 64
    lc = LANES
    first = hl - CONV_WIDTH // 2
    for j in range(tm // rc):
        for c in range(D_MODEL // lc):
            cols = slice(c * lc, (c + 1) * lc)
            acc = jnp.zeros((rc, lc), F32) + bdw_ref[:, cols]
            for b in range(SUBLANES):
                part = None
                for t in range(CONV_WIDTH):
                    if (t + first) % SUBLANES != b:
                        continue
                    off = j * rc + (t + first) - b
                    term = u_scr[off:off + rc + SUBLANES, cols] * wdw_ref[t:t + 1, cols]
                    part = term if part is None else part + term
                acc = acc + part[b:b + rc]
            c_scr[j * rc:(j + 1) * rc, cols] = acc
    cv = c_scr[...]
    mu = jnp.mean(cv, axis=-1, keepdims=True)
    xc = cv - mu
    var = jnp.mean(xc * xc, axis=-1, keepdims=True)
    ln = xc * lax.rsqrt(var + EPS) * lng_ref[...] + lnb_ref[...]
    act = (ln * jax.nn.sigmoid(ln)).astype(BF16)
    y = x_scr[hl:hl + tm, :] + jnp.dot(act, w2_ref[...], preferred_element_type=F32) + b2_ref[...]
    y_ref[...] = y
    _route(y, gf_ref[...], wr_ref, br_ref, ri_ref, rg_ref, hp_ref)


def _conv_call(y2d, rg, rows, seq, gm, w1, b1, wdw, bdw, lng, lnb, w2, b2, gf, wr, br):
    n = y2d.shape[0]
    tm = ROW_TILE
    hl = CONV_HALO
    per = tm // hl
    tiles = n // tm
    const = lambda i: (0, 0)
    cur_blk = lambda i: i
    prev_blk = lambda i: jnp.maximum(i * per - 1, 0)
    next_blk = lambda i: jnp.minimum((i + 1) * per, n // hl - 1)

    def moe_specs(rows_blk, blk_of, blocks_per_piece):
        at = lambda off: (lambda i: (blk_of(i) + off, 0))
        return ([pl.BlockSpec((rows_blk, D_MODEL), at(0)), pl.BlockSpec((rows_blk, LANES), at(0))]
                + [pl.BlockSpec((rows_blk, SC_ROW_WORDS), at(j * blocks_per_piece))
                   for j in range(SC_ROW_SPLIT * TOP_K)])

    moe_args = [y2d, rg] + [rows] * (SC_ROW_SPLIT * TOP_K)
    out_specs, out_shape = _route_out_specs(n, tm)
    return pl.pallas_call(
        functools.partial(_conv_kernel, seq=seq, tm=tm),
        grid=(tiles,),
        in_specs=(
            moe_specs(tm, cur_blk, tiles) + moe_specs(hl, prev_blk, tiles * per)
            + moe_specs(hl, next_blk, tiles * per)
            + [
                pl.BlockSpec((1, D_MODEL), const),
                pl.BlockSpec((D_MODEL, 2 * D_MODEL), const),
                pl.BlockSpec((1, 2 * D_MODEL), const),
                pl.BlockSpec((CONV_WIDTH, D_MODEL), const),
                pl.BlockSpec((1, D_MODEL), const),
                pl.BlockSpec((1, D_MODEL), const),
                pl.BlockSpec((1, D_MODEL), const),
                pl.BlockSpec((D_MODEL, D_MODEL), const),
                pl.BlockSpec((1, D_MODEL), const),
                pl.BlockSpec((1, D_MODEL), const),
                pl.BlockSpec((D_MODEL, 2 * LANES), const),
                pl.BlockSpec((1, LANES), const),
            ]
        ),
        out_specs=out_specs,
        out_shape=out_shape,
        scratch_shapes=[
            pltpu.VMEM((tm + 2 * hl, D_MODEL), F32),
            pltpu.VMEM((tm + 2 * hl, D_MODEL), F32),
            pltpu.VMEM((tm, D_MODEL), F32),
        ],
        compiler_params=_cparams(("parallel",)),
        name="conv_module_router",
    )(*moe_args, *moe_args, *moe_args, gm, w1, b1, wdw, bdw, lng, lnb, w2, b2, gf, wr, br)


def _sc_mesh():
    return plsc.VectorSubcoreMesh(core_axis_name="core", subcore_axis_name="subcore")


def _sc_scatter_rows(src, idx, n_src_windows, out_rows):
    m = idx.shape[1]
    width = src.shape[1]
    nsw = n_src_windows

    @functools.partial(pl.kernel, out_type=jax.ShapeDtypeStruct((out_rows, width), src.dtype),
                       mesh=_sc_mesh(), name="sc_scatter_rows")
    def scatter(src_hbm, idx_hbm, out_hbm):
        def body(src_vmem, idx_vmem):
            pltpu.sync_copy(src_vmem, out_hbm.at[idx_vmem.at[0]])

        pltpu.emit_pipeline(
            body,
            grid=(m // SC_WINDOW,),
            in_specs=[
                pl.BlockSpec((SC_WINDOW, width), lambda w: ((w // (TOP_K * nsw)) * nsw + w % nsw, 0)),
                pl.BlockSpec((1, SC_WINDOW), lambda w: (0, w)),
            ],
            out_specs=[],
            core_axis_name=("core", "subcore"),
            dimension_semantics=(pltpu.PARALLEL,),
        )(src_hbm, idx_hbm)

    return scatter(src, idx)


def _sc_gather_rows(table, idx):
    m = idx.shape[1]
    width = table.shape[1]

    @functools.partial(pl.kernel, out_type=jax.ShapeDtypeStruct((m, width), table.dtype),
                       mesh=_sc_mesh(), name="sc_gather_rows")
    def gather(table_hbm, idx_hbm, out_hbm):
        def body(idx_vmem, out_vmem):
            pltpu.sync_copy(table_hbm.at[idx_vmem.at[0]], out_vmem)

        pltpu.emit_pipeline(
            body,
            grid=(m // SC_WINDOW,),
            in_specs=[pl.BlockSpec((1, SC_WINDOW), lambda w: (0, w))],
            out_specs=[pl.BlockSpec((SC_WINDOW, width), lambda w: (w, 0))],
            core_axis_name=("core", "subcore"),
            dimension_semantics=(pltpu.PARALLEL,),
        )(idx_hbm, out_hbm)

    return gather(table, idx)


def _expert_kernel(be_ref, nv_ref, xs_ref, wg_ref, wu_ref, wd_ref, yb_ref, wg_s, wu_s, wd_s):
    i = pl.program_id(0)

    @pl.when((i == 0) | (be_ref[i] != be_ref[jnp.maximum(i - 1, 0)]))
    def _():
        wg_s[...] = wg_ref[0, 0].astype(BF16)
        wu_s[...] = wu_ref[0, 0].astype(BF16)
        wd_s[...] = wd_ref[0, 0].astype(BF16)

    @pl.when(nv_ref[i] > 0)
    def _():
        words = jnp.concatenate([xs_ref[c] for c in range(SC_ROW_SPLIT)], axis=1)
        row = lax.broadcasted_iota(jnp.int32, (words.shape[0], 1), 0)
        words = jnp.where(row < nv_ref[i], words, jnp.uint32(0))
        hi, lo = _unpack_rows(words)
        xb = jnp.concatenate([hi.astype(BF16), lo.astype(BF16)], axis=1)
        chunks = [slice(c, c + MXU_WIDTH) for c in range(0, D_EXPERT, MXU_WIDTH)]
        gu = [(jnp.dot(xb, wg_s[:, c], preferred_element_type=F32),
               jnp.dot(xb, wu_s[:, c], preferred_element_type=F32)) for c in chunks]
        y = None
        for c, (g, u) in zip(chunks, gu):
            hid = (g * jax.nn.sigmoid(g) * u).astype(BF16)
            part = jnp.dot(hid, wd_s[c, :], preferred_element_type=F32)
            y = part if y is None else y + part
        packed = _pack_rows(y)
        for c in range(SC_ROW_SPLIT):
            yb_ref[c] = packed[:, c * SC_ROW_WORDS:(c + 1) * SC_ROW_WORDS]


def _expert_call(block_expert, block_valid, xs, layer, w_gate, w_up, w_down):
    p_total = xs.shape[1]
    blk = MOE_BLOCK
    wmap = lambda i, be, nv: (layer, be[i], 0, 0)
    rows = lambda i, be, nv: (0, i, 0)
    return pl.pallas_call(
        _expert_kernel,
        grid_spec=pltpu.PrefetchScalarGridSpec(
            num_scalar_prefetch=2,
            grid=(p_total // blk,),
            in_specs=[
                pl.BlockSpec((SC_ROW_SPLIT, blk, SC_ROW_WORDS), rows),
                pl.BlockSpec((1, 1, D_MODEL, D_EXPERT), wmap),
                pl.BlockSpec((1, 1, D_MODEL, D_EXPERT), wmap),
                pl.BlockSpec((1, 1, D_EXPERT, D_MODEL), wmap),
            ],
            out_specs=pl.BlockSpec((SC_ROW_SPLIT, blk, SC_ROW_WORDS), rows),
            scratch_shapes=[
                pltpu.VMEM((D_MODEL, D_EXPERT), BF16),
                pltpu.VMEM((D_MODEL, D_EXPERT), BF16),
                pltpu.VMEM((D_EXPERT, D_MODEL), BF16),
            ],
        ),
        out_shape=jax.ShapeDtypeStruct((SC_ROW_SPLIT, p_total, SC_ROW_WORDS), jnp.uint32),
        compiler_params=_cparams(("arbitrary",)),
        name="moe_experts",
    )(block_expert, block_valid, xs, w_gate, w_up, w_down)


def _combined_rows(y_ref, rg_ref, piece_refs):
    rg = rg_ref[...]
    his, los = [], []
    for c in range(SC_ROW_SPLIT):
        hi0, lo0 = _unpack_rows(piece_refs[c * TOP_K][...])
        hi1, lo1 = _unpack_rows(piece_refs[c * TOP_K + 1][...])
        his.append(rg[:, 0:1] * hi0 + rg[:, 1:2] * hi1)
        los.append(rg[:, 0:1] * lo0 + rg[:, 1:2] * lo1)
    return y_ref[...] + jnp.concatenate(his + los, axis=1)


def _combine_kernel(y_ref, rg_ref, *refs):
    refs[-1][...] = _combined_rows(y_ref, rg_ref, refs[:-1])


def _combine_call(y2d, rg, rows):
    n = y2d.shape[0]
    tm = ROW_TILE
    tiles = n // tm
    return pl.pallas_call(
        _combine_kernel,
        grid=(tiles,),
        in_specs=[
            pl.BlockSpec((tm, D_MODEL), lambda i: (i, 0)),
            pl.BlockSpec((tm, LANES), lambda i: (i, 0)),
        ] + [
            pl.BlockSpec((tm, SC_ROW_WORDS), functools.partial(lambda i, j: (i + j * tiles, 0), j=j))
            for j in range(SC_ROW_SPLIT * TOP_K)
        ],
        out_specs=pl.BlockSpec((tm, D_MODEL), lambda i: (i, 0)),
        out_shape=jax.ShapeDtypeStruct((n, D_MODEL), F32),
        compiler_params=_cparams(("parallel",)),
        name="moe_combine",
    )(y2d, rg, *([rows] * (SC_ROW_SPLIT * TOP_K)))


def _slot_tables(ri, n):
    e = ri[:, :TOP_K]
    onehot = (e[:, :, None] == jnp.arange(N_EXPERTS, dtype=jnp.int32)).astype(jnp.int32)
    tok_cnt = onehot.sum(axis=1)
    t = 256
    cnt_t = tok_cnt.reshape(n // t, t, N_EXPERTS)
    tri = (jnp.arange(t)[:, None] > jnp.arange(t)[None, :]).astype(F32)
    within = jnp.einsum("ts,nsc->ntc", tri, cnt_t.astype(F32)).astype(jnp.int32)
    tile_sum = cnt_t.sum(axis=1)
    tile_base = jnp.cumsum(tile_sum, axis=0) - tile_sum
    rank = (within + tile_base[:, None, :]).reshape(n, N_EXPERTS)
    counts = tile_sum.sum(axis=0)
    padded = ((counts + MOE_BLOCK - 1) // MOE_BLOCK) * MOE_BLOCK
    pad_end = jnp.cumsum(padded)
    pad_start = pad_end - padded
    slot = ((rank + pad_start)[:, None, :] * onehot).sum(axis=-1)
    n_blocks = (n * TOP_K) // MOE_BLOCK + N_EXPERTS
    block_start = jnp.arange(n_blocks, dtype=jnp.int32) * MOE_BLOCK
    block_expert = jnp.minimum((pad_end[None, :] <= block_start[:, None]).sum(axis=-1),
                               N_EXPERTS - 1).astype(jnp.int32)
    used_end = (pad_start + counts)[block_expert]
    block_valid = jnp.clip(used_end - block_start, 0, MOE_BLOCK).astype(jnp.int32)
    p_total = n_blocks * MOE_BLOCK
    slot_km = slot.T.reshape(1, TOP_K * n).astype(jnp.int32)
    idx = jnp.concatenate([slot_km + c * p_total for c in range(SC_ROW_SPLIT)], axis=1)
    return idx, block_expert, block_valid, p_total


def _moe_rows(n, ri, hp, layer, experts):
    idx, block_expert, block_valid, p_total = _slot_tables(ri, n)
    xs = _sc_scatter_rows(hp.reshape(SC_ROW_SPLIT * n, SC_ROW_WORDS), idx, n // SC_WINDOW,
                          SC_ROW_SPLIT * p_total)
    yb = _expert_call(block_expert, block_valid, xs.reshape(SC_ROW_SPLIT, p_total, SC_ROW_WORDS),
                      layer, *experts)
    return _sc_gather_rows(yb.reshape(SC_ROW_SPLIT * p_total, SC_ROW_WORDS), idx)


def _rope_tables(seq):
    pos = jnp.arange(seq, dtype=F32)
    inv_freq = ROPE_THETA ** (-jnp.arange(0, ROPE_DIM, 2, dtype=F32) / ROPE_DIM)
    ang = pos[:, None] * inv_freq[None, :]
    cos = jnp.cos(ang)
    sin = jnp.sin(ang)
    half = ROPE_DIM // 2
    rest = HEAD_DIM - ROPE_DIM
    cos_h = jnp.concatenate([cos, cos, jnp.ones((seq, rest), F32)], axis=1)
    sa_h = jnp.concatenate([-sin, jnp.zeros((seq, half + rest), F32)], axis=1)
    sb_h = jnp.concatenate([jnp.zeros((seq, half), F32), sin, jnp.zeros((seq, rest), F32)], axis=1)
    tile = lambda t: jnp.tile(t, (1, HEADS_PER_GROUP))
    return tile(cos_h), tile(sa_h), tile(sb_h)


def _router_weights(w_rg, b_rg, w_re, b_re):
    w = jnp.concatenate([w_rg, w_re], axis=1)
    w = jnp.pad(w, ((0, 0), (0, LANES - w.shape[1])))
    w_hi = w.astype(BF16)
    w_lo = (w - w_hi.astype(F32)).astype(BF16)
    b = jnp.pad(jnp.concatenate([b_rg, b_re]), (0, LANES - N_EXPERT_GROUPS - N_EXPERTS))
    return jnp.concatenate([w_hi, w_lo], axis=1), b.reshape(1, LANES)


def _trunk(x, p):
    batch, seq, d = x.shape
    n = batch * seq
    x2d = x.reshape(n, d)
    cos, sa, sb = _rope_tables(seq)

    qkv = _qkv_call(x2d, seq, p["ln_mix0"], p["w_qkv"], p["mseg"], p["qg"], p["kg"], cos, sa, sb)
    os_, lses = [], []
    for g, dil in enumerate(DILATIONS):
        o, lse = _attn_call(qkv[g], qkv[3 + g], qkv[6 + g], batch, seq, dil)
        os_.append(o)
        lses.append(lse)
    y, ri, rg, hp = _attn_out_call(os_, lses, x2d, p["w_o"], p["expand"], p["ln_ffn0"], p["wr0"], p["br0"])
    rows = _moe_rows(n, ri, hp, 0, p["experts"])

    y, ri, rg, hp = _conv_call(y, rg, rows, seq, p["ln_mix1"], p["w_pw1"], p["b_pw1"], p["w_dw"], p["b_dw"],
                               p["conv_ln_g"], p["conv_ln_b"], p["w_pw2"], p["b_pw2"],
                               p["ln_ffn1"], p["wr1"], p["br1"])
    rows = _moe_rows(n, ri, hp, 1, p["experts"])
    return _combine_call(y, rg, rows).reshape(batch, seq, d)


def kernel(x_prompt, x_sample, ln_mix, ln_ffn, w_qkv, q_gain, k_gain, w_o, w_pw1, b_pw1, w_dw, b_dw,
           conv_ln_g, conv_ln_b, w_pw2, b_pw2, w_router_group, b_router_group, w_router_expert,
           b_router_expert, w_gate, w_up, w_down):
    row = lambda v: v.reshape(1, -1).astype(F32)
    n_heads = N_GROUPS * HEADS_PER_GROUP
    head_id = np.arange(ATTN_WIDTH) // HEAD_DIM
    mseg = jnp.asarray((head_id[:, None] == head_id[None, :]) / HEAD_DIM, BF16)
    slot_id = np.arange(GROUP_WIDTH) // HEAD_DIM
    expand = jnp.asarray(np.arange(LANES)[:, None] == slot_id[None, :], BF16)
    wr0, br0 = _router_weights(w_router_group[0], b_router_group[0], w_router_expert[0], b_router_expert[0])
    wr1, br1 = _router_weights(w_router_group[1], b_router_group[1], w_router_expert[1], b_router_expert[1])
    p = dict(
        ln_mix0=row(ln_mix[0]), ln_mix1=row(ln_mix[1]), ln_ffn0=row(ln_ffn[0]), ln_ffn1=row(ln_ffn[1]),
        w_qkv=w_qkv[0].astype(BF16), mseg=mseg, expand=expand,
        qg=row(jnp.tile(q_gain[0], n_heads) * (HEAD_DIM ** -0.5 * LOG2E)), kg=row(jnp.tile(k_gain[0], n_heads)),
        w_o=w_o[0].astype(BF16),
        w_pw1=w_pw1[0].astype(BF16), b_pw1=row(b_pw1[0]), w_dw=w_dw[0].astype(F32), b_dw=row(b_dw[0]),
        conv_ln_g=row(conv_ln_g[0]), conv_ln_b=row(conv_ln_b[0]),
        w_pw2=w_pw2[0].astype(BF16), b_pw2=row(b_pw2[0]),
        wr0=wr0, br0=br0, wr1=wr1, br1=br1,
        experts=(w_gate, w_up, w_down),
    )
    return _trunk(x_prompt, p), _trunk(x_sample, p)
```

```python
import functools

import jax
import jax.numpy as jnp
import numpy as np
from jax import lax
from jax.experimental import pallas as pl
from jax.experimental.pallas import tpu as pltpu
from jax.experimental.pallas import tpu_sc as plsc

D_MODEL = 1024
HEAD_DIM = 64
HEADS_PER_GROUP = 4
DILATIONS = (1, 4, 16)
HALF_WINDOW = 64
N_GROUPS = len(DILATIONS)
GROUP_WIDTH = HEADS_PER_GROUP * HEAD_DIM
ATTN_WIDTH = N_GROUPS * GROUP_WIDTH
ROPE_DIM = HEAD_DIM // 4
ROPE_THETA = 500000.0
CONV_WIDTH = 31
CONV_HALO = 16
N_EXPERT_GROUPS = 4
EXPERTS_PER_GROUP = 8
N_EXPERTS = N_EXPERT_GROUPS * EXPERTS_PER_GROUP
TOP_K = 2
D_EXPERT = 512
EPS = 1e-6
NEG_INF = -1e30
LOG2E = 1.4426950408889634
LN2 = 0.6931471805599453

LANES = 128
SUBLANES = 8
MXU_WIDTH = 256
ROW_TILE = 512
ATTN_Q_TILE = 128
ATTN_STEP_ROWS = 512
MOE_BLOCK = 512
VMEM_LIMIT = 56 * 1024 * 1024

PACKED_WIDTH = D_MODEL // 2
SC_WINDOW = 128
SC_ROW_SPLIT = 2
SC_ROW_WORDS = PACKED_WIDTH // SC_ROW_SPLIT

F32 = jnp.float32
BF16 = jnp.bfloat16


def _cparams(sem):
    return pltpu.CompilerParams(dimension_semantics=sem, vmem_limit_bytes=VMEM_LIMIT)


def _rms(x, gain):
    return x * lax.rsqrt(jnp.mean(x * x, axis=-1, keepdims=True) + EPS) * gain


def _pack_rows(x):
    bits = pltpu.bitcast(x.astype(BF16).astype(F32), jnp.uint32)
    half = x.shape[1] // 2
    return bits[:, :half] | (bits[:, half:] >> 16)


def _unpack_rows(w):
    hi = pltpu.bitcast(w & jnp.uint32(0xFFFF0000), F32)
    lo = pltpu.bitcast(w << 16, F32)
    return hi, lo


def _store_by_residue(out_ref, val, dil, scr):
    if dil == 1:
        out_ref[...] = val.astype(out_ref.dtype)
        return
    rows, width = val.shape
    for c in range(width // LANES):
        scr[c] = val[:, c * LANES:(c + 1) * LANES]
    for r in range(dil):
        for c in range(width // LANES):
            lanes = slice(r * width + c * LANES, r * width + (c + 1) * LANES)
            out_ref[:, lanes] = scr[c, pl.ds(r, rows // dil, stride=dil), :].astype(out_ref.dtype)


def _qkv_kernel(x_ref, g_ref, w_ref, mseg_ref, qg_ref, kg_ref, cos_ref, sa_ref, sb_ref,
                q0, q1, q2, k0, k1, k2, v0, v1, v2, *scrs):
    h = _rms(x_ref[...], g_ref[...]).astype(BF16)
    cos = cos_ref[...]
    sa = sa_ref[...]
    sb = sb_ref[...]
    scrs = iter(scrs)

    def store(out_ref, val, g):
        _store_by_residue(out_ref, val, DILATIONS[g], next(scrs) if DILATIONS[g] > 1 else None)

    q = jnp.dot(h, w_ref[:, 0:ATTN_WIDTH], preferred_element_type=F32)
    k = jnp.dot(h, w_ref[:, ATTN_WIDTH:2 * ATTN_WIDTH], preferred_element_type=F32)
    v = jnp.dot(h, w_ref[:, 2 * ATTN_WIDTH:3 * ATTN_WIDTH], preferred_element_type=F32)
    ms_q = jnp.dot((q * q).astype(BF16), mseg_ref[...], preferred_element_type=F32)
    ms_k = jnp.dot((k * k).astype(BF16), mseg_ref[...], preferred_element_type=F32)
    for g, o in enumerate((v0, v1, v2)):
        store(o, v[:, g * GROUP_WIDTH:(g + 1) * GROUP_WIDTH], g)
    for t, ms, gain, outs in ((q, ms_q, qg_ref[...], (q0, q1, q2)), (k, ms_k, kg_ref[...], (k0, k1, k2))):
        tn = t * lax.rsqrt(ms + EPS) * gain
        for g in range(N_GROUPS):
            c = tn[:, g * GROUP_WIDTH:(g + 1) * GROUP_WIDTH]
            r = (c * cos + pltpu.roll(c, GROUP_WIDTH - ROPE_DIM // 2, 1) * sa
                 + pltpu.roll(c, ROPE_DIM // 2, 1) * sb)
            store(outs[g], r, g)


def _qkv_call(x2d, seq, gain, w_qkv, mseg, qg, kg, cos, sa, sb):
    n = x2d.shape[0]
    tm = ROW_TILE
    tiles_per_seq = seq // tm
    row = lambda i: (i, 0)
    const = lambda i: (0, 0)
    tab = lambda i: (i % tiles_per_seq, 0)
    out_specs = [pl.BlockSpec((tm // dil, dil * GROUP_WIDTH), row) for dil in DILATIONS] * 3
    out_shape = [jax.ShapeDtypeStruct((n // dil, dil * GROUP_WIDTH), BF16) for dil in DILATIONS] * 3
    return pl.pallas_call(
        _qkv_kernel,
        grid=(n // tm,),
        in_specs=[
            pl.BlockSpec((tm, D_MODEL), row),
            pl.BlockSpec((1, D_MODEL), const),
            pl.BlockSpec((D_MODEL, 3 * ATTN_WIDTH), const),
            pl.BlockSpec((ATTN_WIDTH, ATTN_WIDTH), const),
            pl.BlockSpec((1, ATTN_WIDTH), const),
            pl.BlockSpec((1, ATTN_WIDTH), const),
            pl.BlockSpec((tm, GROUP_WIDTH), tab),
            pl.BlockSpec((tm, GROUP_WIDTH), tab),
            pl.BlockSpec((tm, GROUP_WIDTH), tab),
        ],
        out_specs=out_specs,
        out_shape=out_shape,
        scratch_shapes=[pltpu.VMEM((GROUP_WIDTH // LANES, tm, LANES), F32)]
        * (3 * sum(dil > 1 for dil in DILATIONS)),
        compiler_params=_cparams(("parallel",)),
        name="qkv_proj",
    )(x2d, gain, w_qkv, mseg, qg, kg, cos, sa, sb)


def _attn_kernel(q_ref, k_ref, v_ref, o_ref, lse_ref, *, ls, rows, res, win):
    i = pl.program_id(2)
    tq = ATTN_Q_TILE
    rel = (lax.broadcasted_iota(jnp.int32, (tq, win), 0) - lax.broadcasted_iota(jnp.int32, (tq, win), 1))
    lane = lax.broadcasted_iota(jnp.int32, (tq, LANES), 1)
    lane_kv = lax.broadcasted_iota(jnp.int32, (win, LANES), 1)

    blocks = []
    for r in range(res):
        for sb in range(rows // tq):
            q_start = i * rows + sb * tq
            k_start = pl.multiple_of(jnp.clip(q_start - HALF_WINDOW, 0, ls - win), HALF_WINDOW)
            mask = jnp.abs(rel + (q_start - k_start)) <= HALF_WINDOW
            blocks.append((r, slice(sb * tq, (sb + 1) * tq), k_start, mask))
    chains = [(blk, h) for blk in blocks for h in range(HEADS_PER_GROUP)]

    def cols(r, h):
        return slice(r * GROUP_WIDTH + h * HEAD_DIM, r * GROUP_WIDTH + (h + 1) * HEAD_DIM)

    scores = [lax.dot_general(q_ref[0, q_rows, cols(r, h)], k_ref[0, pl.ds(k_start, win), cols(r, h)],
                              (((1,), (1,)), ((), ())), preferred_element_type=F32)
              for (r, q_rows, k_start, _), h in chains]
    maxes, probs = [], []
    for ((_, _, _, mask), _), s in zip(chains, scores):
        s = jnp.where(mask, s, NEG_INF)
        m = jnp.max(s, axis=-1, keepdims=True)
        maxes.append(m)
        probs.append(jnp.exp2(s - m).astype(BF16))
    outs = []
    for ((r, _, k_start, _), h), p in zip(chains, probs):
        pair = slice(r * GROUP_WIDTH + (h // 2) * LANES, r * GROUP_WIDTH + (h // 2 + 1) * LANES)
        own_half = (lane_kv < HEAD_DIM) if h % 2 == 0 else (lane_kv >= HEAD_DIM)
        v_ext = jnp.where(own_half, v_ref[0, pl.ds(k_start, win), pair], jnp.ones((), BF16))
        outs.append(jnp.dot(p, v_ext, preferred_element_type=F32))
    lse_tiles = {}
    for ((r, q_rows, _, _), h), o_ext, m in zip(chains, outs, maxes):
        den_ext = pltpu.roll(o_ext, HEAD_DIM, 1)
        own = slice((h % 2) * HEAD_DIM, (h % 2 + 1) * HEAD_DIM)
        o_ref[0, q_rows, cols(r, h)] = (o_ext / den_ext)[:, own].astype(BF16)
        den_at_h = den_ext if h % 2 == 0 else o_ext
        key = (r, q_rows.start)
        lse_tiles[key] = jnp.where(lane == h, (m + jnp.log2(den_at_h)) * LN2,
                                   lse_tiles.get(key, jnp.zeros((tq, LANES), F32)))
    for (r, q0), tile in lse_tiles.items():
        lse_ref[0, q0:q0 + tq, r * LANES:(r + 1) * LANES] = tile


def _attn_call(q, k, v, batch, seq, dil):
    ls = seq // dil
    rows = min(ATTN_STEP_ROWS, ls)
    res = max(1, min(dil, ATTN_STEP_ROWS // rows))
    win = min(ATTN_Q_TILE + 2 * HALF_WINDOW, ls)
    view = lambda t: t.reshape(batch, ls, dil * GROUP_WIDTH)
    qmap = lambda b, r, i: (b, i, r)
    kvmap = lambda b, r, i: (b, 0, r)
    o, lse = pl.pallas_call(
        functools.partial(_attn_kernel, ls=ls, rows=rows, res=res, win=win),
        grid=(batch, dil // res, ls // rows),
        in_specs=[
            pl.BlockSpec((1, rows, res * GROUP_WIDTH), qmap),
            pl.BlockSpec((1, ls, res * GROUP_WIDTH), kvmap),
            pl.BlockSpec((1, ls, res * GROUP_WIDTH), kvmap),
        ],
        out_specs=[
            pl.BlockSpec((1, rows, res * GROUP_WIDTH), qmap),
            pl.BlockSpec((1, rows, res * LANES), qmap),
        ],
        out_shape=[
            jax.ShapeDtypeStruct((batch, ls, dil * GROUP_WIDTH), BF16),
            jax.ShapeDtypeStruct((batch, ls, dil * LANES), F32),
        ],
        compiler_params=_cparams(("parallel", "parallel", "arbitrary")),
        name=f"banded_attn_d{dil}",
    )(view(q), view(k), view(v))
    return o.reshape(batch * ls, dil * GROUP_WIDTH), lse.reshape(batch * ls, dil * LANES)


def _route_logits(x, gain, wr_ref, br_ref, hp_ref, rows):
    h = _rms(x, gain)
    packed = _pack_rows(h)
    for c in range(SC_ROW_SPLIT):
        hp_ref[c, rows, :] = packed[:, c * SC_ROW_WORDS:(c + 1) * SC_ROW_WORDS]
    h_hi = h.astype(BF16)
    h_lo = (h - h_hi.astype(F32)).astype(BF16)
    acc = (jnp.dot(h_hi, wr_ref[...], preferred_element_type=F32)
           + jnp.dot(h_lo, wr_ref[...], preferred_element_type=F32))
    return acc[:, :LANES] + acc[:, LANES:] + br_ref[...]


def _route_decide(logits, ri_ref, rg_ref, rows):
    lane = lax.broadcasted_iota(jnp.int32, logits.shape, 1)
    lane_f = lane.astype(F32)
    big = jnp.float32(2 * LANES)

    is_g = lane < N_EXPERT_GROUPS
    gl = jnp.where(is_g, logits, NEG_INF)
    gmax = jnp.max(gl, axis=-1, keepdims=True)
    grp = jnp.min(jnp.where(gl == gmax, lane_f, big), axis=-1, keepdims=True)
    p_grp = 1.0 / jnp.sum(jnp.where(is_g, jnp.exp(gl - gmax), 0.0), axis=-1, keepdims=True)

    lo = N_EXPERT_GROUPS + grp * EXPERTS_PER_GROUP
    in_grp = (lane_f >= lo) & (lane_f < lo + EXPERTS_PER_GROUP)
    el = jnp.where(in_grp, logits, NEG_INF)
    v1 = jnp.max(el, axis=-1, keepdims=True)
    i1 = jnp.min(jnp.where(el == v1, lane_f, big), axis=-1, keepdims=True)
    el2 = jnp.where(lane_f == i1, NEG_INF, el)
    v2 = jnp.max(el2, axis=-1, keepdims=True)
    i2 = jnp.min(jnp.where(el2 == v2, lane_f, big), axis=-1, keepdims=True)
    e21 = jnp.exp(v2 - v1)
    g1 = p_grp / (1.0 + e21)
    g2 = g1 * e21
    ri = jnp.where(lane == 0, i1 - N_EXPERT_GROUPS, jnp.where(lane == 1, i2 - N_EXPERT_GROUPS, 0.0))
    ri_ref[rows, :] = ri.astype(jnp.int32)
    rg_ref[rows, :] = jnp.where(lane == 0, g1, jnp.where(lane == 1, g2, 0.0))


def _row_halves(tm):
    return [slice(0, tm // 2), slice(tm // 2, tm)]


def _route_out_specs(n, tm):
    row = lambda i: (i, 0)
    specs = [
        pl.BlockSpec((tm, D_MODEL), row),
        pl.BlockSpec((tm, LANES), row),
        pl.BlockSpec((tm, LANES), row),
        pl.BlockSpec((SC_ROW_SPLIT, tm, SC_ROW_WORDS), lambda i: (0, i, 0)),
    ]
    shapes = [
        jax.ShapeDtypeStruct((n, D_MODEL), F32),
        jax.ShapeDtypeStruct((n, LANES), jnp.int32),
        jax.ShapeDtypeStruct((n, LANES), F32),
        jax.ShapeDtypeStruct((SC_ROW_SPLIT, n, SC_ROW_WORDS), jnp.uint32),
    ]
    return specs, shapes


def _load_by_residue(in_ref, dil, scr):
    if dil == 1:
        return in_ref[...].astype(F32)
    chunks, rows, _ = scr.shape
    width = chunks * LANES
    for r in range(dil):
        for c in range(chunks):
            lanes = slice(r * width + c * LANES, r * width + (c + 1) * LANES)
            scr[c, pl.ds(r, rows // dil, stride=dil), :] = in_ref[:, lanes].astype(F32)
    return jnp.concatenate([scr[c] for c in range(chunks)], axis=1) if chunks > 1 else scr[0]


def _attn_out_kernel(o0, o1, o2, l0, l1, l2, x_ref, wo_ref, ex_ref, g_ref, wr_ref, br_ref,
                     y_ref, ri_ref, rg_ref, hp_ref, o_scr1, o_scr2, l_scr1, l_scr2):
    ls = [_load_by_residue(l, dil, scr)
          for l, dil, scr in zip((l0, l1, l2), DILATIONS, (None, l_scr1, l_scr2))]
    m = jnp.maximum(jnp.maximum(ls[0], ls[1]), ls[2])
    es = [jnp.exp(l - m) for l in ls]
    inv = 1.0 / (es[0] + es[1] + es[2])
    a_wide = []
    for g in range(N_GROUPS):
        alpha = es[g] * inv
        a_hi = alpha.astype(BF16)
        a_lo = (alpha - a_hi.astype(F32)).astype(BF16)
        a_wide.append(jnp.dot(a_hi, ex_ref[...], preferred_element_type=F32)
                      + jnp.dot(a_lo, ex_ref[...], preferred_element_type=F32))
    scaled = jnp.concatenate(
        [(_load_by_residue(o_ref, dil, scr) * a).astype(BF16)
         for o_ref, dil, scr, a in zip((o0, o1, o2), DILATIONS, (None, o_scr1, o_scr2), a_wide)], axis=1)
    halves = _row_halves(x_ref.shape[0])
    ys = [x_ref[rows, :] + jnp.dot(scaled[rows, :], wo_ref[...], preferred_element_type=F32)
          for rows in halves]
    for rows, y in zip(halves, ys):
        y_ref[rows, :] = y
    logits = [_route_logits(y, g_ref[...], wr_ref, br_ref, hp_ref, rows) for rows, y in zip(halves, ys)]
    for rows, lg in zip(halves, logits):
        _route_decide(lg, ri_ref, rg_ref, rows)


def _attn_out_call(os_, lses, x2d, w_o, expand, gain, wr, br):
    n = x2d.shape[0]
    tm = ROW_TILE
    row = lambda i: (i, 0)
    const = lambda i: (0, 0)
    out_specs, out_shape = _route_out_specs(n, tm)
    return pl.pallas_call(
        _attn_out_kernel,
        grid=(n // tm,),
        in_specs=(
            [pl.BlockSpec((tm // dil, dil * GROUP_WIDTH), row) for dil in DILATIONS]
            + [pl.BlockSpec((tm // dil, dil * LANES), row) for dil in DILATIONS]
            + [
                pl.BlockSpec((tm, D_MODEL), row),
                pl.BlockSpec((ATTN_WIDTH, D_MODEL), const),
                pl.BlockSpec((LANES, GROUP_WIDTH), const),
                pl.BlockSpec((1, D_MODEL), const),
                pl.BlockSpec((D_MODEL, 2 * LANES), const),
                pl.BlockSpec((1, LANES), const),
            ]
        ),
        out_specs=out_specs,
        out_shape=out_shape,
        scratch_shapes=([pltpu.VMEM((GROUP_WIDTH // LANES, tm, LANES), F32)] * 2
                        + [pltpu.VMEM((1, tm, LANES), F32)] * 2),
        compiler_params=_cparams(("parallel",)),
        name="attn_out_router",
    )(*os_, *lses, x2d, w_o, expand, gain, wr, br)


N_MOE_REFS = 2 + SC_ROW_SPLIT * TOP_K


def _conv_kernel(*refs, seq, tm):
    cur, prev, nxt = (refs[b * N_MOE_REFS:(b + 1) * N_MOE_REFS] for b in range(3))
    (gm_ref, w1_ref, b1_ref, wdw_ref, bdw_ref, lng_ref, lnb_ref, w2_ref, b2_ref, gf_ref, wr_ref, br_ref,
     y_ref, ri_ref, rg_ref, hp_ref, x_scr, u_scr, c_scr) = refs[3 * N_MOE_REFS:]
    i = pl.program_id(0)
    pos0 = (i * tm) % seq
    hl = CONV_HALO
    x_scr[0:hl, :] = _combined_rows(prev[0], prev[1], prev[2:])
    x_scr[hl:hl + tm, :] = _combined_rows(cur[0], cur[1], cur[2:])
    x_scr[hl + tm:hl + tm + hl, :] = _combined_rows(nxt[0], nxt[1], nxt[2:])
    halves = _row_halves(tm)
    first = hl - CONV_WIDTH // 2
    r_all = lax.broadcasted_iota(jnp.int32, (tm + 2 * hl, 1), 0)
    outside = ((r_all < hl) & (pos0 == 0)) | ((r_all >= hl + tm) & (pos0 + tm == seq))
    u_parts = [slice(0, tm // 2 + 2 * hl), slice(tm // 2 + 2 * hl, tm + 2 * hl)]
    gated = []
    for part in u_parts:
        h = _rms(x_scr[part, :], gm_ref[...]).astype(BF16)
        a = jnp.dot(h, w1_ref[...], preferred_element_type=F32) + b1_ref[...]
        gated.append(a)
    for part, a in zip(u_parts, gated):
        u = a[:, :D_MODEL] * jax.nn.sigmoid(a[:, D_MODEL:])
        u_scr[part, :] = jnp.where(outside[part, :], 0.0, u)

    rc = 64
    lc = LANES
    logits = []
    for rows in halves:
        for j in range(rows.start // rc, rows.stop // rc):
            for c in range(D_MODEL // lc):
                cols = slice(c * lc, (c + 1) * lc)
                acc = jnp.zeros((rc, lc), F32) + bdw_ref[:, cols]
                for b in range(SUBLANES):
                    part = None
                    for t in range(CONV_WIDTH):
                        if (t + first) % SUBLANES != b:
                            continue
                        off = j * rc + (t + first) - b
                        term = u_scr[off:off + rc + SUBLANES, cols] * wdw_ref[t:t + 1, cols]
                        part = term if part is None else part + term
                    acc = acc + part[b:b + rc]
                c_scr[j * rc:(j + 1) * rc, cols] = acc
        cv = c_scr[rows, :]
        mu = jnp.mean(cv, axis=-1, keepdims=True)
        xc = cv - mu
        var = jnp.mean(xc * xc, axis=-1, keepdims=True)
        ln = xc * lax.rsqrt(var + EPS) * lng_ref[...] + lnb_ref[...]
        act = (ln * jax.nn.sigmoid(ln)).astype(BF16)
        y = (x_scr[hl + rows.start:hl + rows.stop, :]
             + jnp.dot(act, w2_ref[...], preferred_element_type=F32) + b2_ref[...])
        y_ref[rows, :] = y
        logits.append(_route_logits(y, gf_ref[...], wr_ref, br_ref, hp_ref, rows))
    for rows, lg in zip(halves, logits):
        _route_decide(lg, ri_ref, rg_ref, rows)


def _conv_call(y2d, rg, rows, seq, gm, w1, b1, wdw, bdw, lng, lnb, w2, b2, gf, wr, br):
    n = y2d.shape[0]
    tm = ROW_TILE
    hl = CONV_HALO
    per = tm // hl
    tiles = n // tm
    const = lambda i: (0, 0)
    cur_blk = lambda i: i
    prev_blk = lambda i: jnp.maximum(i * per - 1, 0)
    next_blk = lambda i: jnp.minimum((i + 1) * per, n // hl - 1)

    def moe_specs(rows_blk, blk_of, blocks_per_piece):
        at = lambda off: (lambda i: (blk_of(i) + off, 0))
        return ([pl.BlockSpec((rows_blk, D_MODEL), at(0)), pl.BlockSpec((rows_blk, LANES), at(0))]
                + [pl.BlockSpec((rows_blk, SC_ROW_WORDS), at(j * blocks_per_piece))
                   for j in range(SC_ROW_SPLIT * TOP_K)])

    moe_args = [y2d, rg] + [rows] * (SC_ROW_SPLIT * TOP_K)
    out_specs, out_shape = _route_out_specs(n, tm)
    return pl.pallas_call(
        functools.partial(_conv_kernel, seq=seq, tm=tm),
        grid=(tiles,),
        in_specs=(
            moe_specs(tm, cur_blk, tiles) + moe_specs(hl, prev_blk, tiles * per)
            + moe_specs(hl, next_blk, tiles * per)
            + [
                pl.BlockSpec((1, D_MODEL), const),
                pl.BlockSpec((D_MODEL, 2 * D_MODEL), const),
                pl.BlockSpec((1, 2 * D_MODEL), const),
                pl.BlockSpec((CONV_WIDTH, D_MODEL), const),
                pl.BlockSpec((1, D_MODEL), const),
                pl.BlockSpec((1, D_MODEL), const),
                pl.BlockSpec((1, D_MODEL), const),
                pl.BlockSpec((D_MODEL, D_MODEL), const),
                pl.BlockSpec((1, D_MODEL), const),
                pl.BlockSpec((1, D_MODEL), const),
                pl.BlockSpec((D_MODEL, 2 * LANES), const),
                pl.BlockSpec((1, LANES), const),
            ]
        ),
        out_specs=out_specs,
        out_shape=out_shape,
        scratch_shapes=[
            pltpu.VMEM((tm + 2 * hl, D_MODEL), F32),
            pltpu.VMEM((tm + 2 * hl, D_MODEL), F32),
            pltpu.VMEM((tm, D_MODEL), F32),
        ],
        compiler_params=_cparams(("parallel",)),
        name="conv_module_router",
    )(*moe_args, *moe_args, *moe_args, gm, w1, b1, wdw, bdw, lng, lnb, w2, b2, gf, wr, br)


def _sc_mesh():
    return plsc.VectorSubcoreMesh(core_axis_name="core", subcore_axis_name="subcore")


def _sc_scatter_rows(src, idx, n_src_windows, out_rows):
    m = idx.shape[1]
    width = src.shape[1]
    nsw = n_src_windows

    @functools.partial(pl.kernel, out_type=jax.ShapeDtypeStruct((out_rows, width), src.dtype),
                       mesh=_sc_mesh(), name="sc_scatter_rows")
    def scatter(src_hbm, idx_hbm, out_hbm):
        def body(src_vmem, idx_vmem):
            pltpu.sync_copy(src_vmem, out_hbm.at[idx_vmem.at[0]])

        pltpu.emit_pipeline(
            body,
            grid=(m // SC_WINDOW,),
            in_specs=[
                pl.BlockSpec((SC_WINDOW, width), lambda w: ((w // (TOP_K * nsw)) * nsw + w % nsw, 0)),
                pl.BlockSpec((1, SC_WINDOW), lambda w: (0, w)),
            ],
            out_specs=[],
            core_axis_name=("core", "subcore"),
            dimension_semantics=(pltpu.PARALLEL,),
        )(src_hbm, idx_hbm)

    return scatter(src, idx)


def _sc_gather_rows(table, idx):
    m = idx.shape[1]
    width = table.shape[1]

    @functools.partial(pl.kernel, out_type=jax.ShapeDtypeStruct((m, width), table.dtype),
                       mesh=_sc_mesh(), name="sc_gather_rows")
    def gather(table_hbm, idx_hbm, out_hbm):
        def body(idx_vmem, out_vmem):
            pltpu.sync_copy(table_hbm.at[idx_vmem.at[0]], out_vmem)

        pltpu.emit_pipeline(
            body,
            grid=(m // SC_WINDOW,),
            in_specs=[pl.BlockSpec((1, SC_WINDOW), lambda w: (0, w))],
            out_specs=[pl.BlockSpec((SC_WINDOW, width), lambda w: (w, 0))],
            core_axis_name=("core", "subcore"),
            dimension_semantics=(pltpu.PARALLEL,),
        )(idx_hbm, out_hbm)

    return gather(table, idx)


def _expert_kernel(be_ref, nv_ref, xs_ref, wg_ref, wu_ref, wd_ref, yb_ref, wg_s, wu_s, wd_s):
    i = pl.program_id(0)

    @pl.when((i == 0) | (be_ref[i] != be_ref[jnp.maximum(i - 1, 0)]))
    def _():
        wg_s[...] = wg_ref[0, 0].astype(BF16)
        wu_s[...] = wu_ref[0, 0].astype(BF16)
        wd_s[...] = wd_ref[0, 0].astype(BF16)

    @pl.when(nv_ref[i] > 0)
    def _():
        words = jnp.concatenate([xs_ref[c] for c in range(SC_ROW_SPLIT)], axis=1)
        row = lax.broadcasted_iota(jnp.int32, (words.shape[0], 1), 0)
        words = jnp.where(row < nv_ref[i], words, jnp.uint32(0))
        hi, lo = _unpack_rows(words)
        xb = jnp.concatenate([hi.astype(BF16), lo.astype(BF16)], axis=1)
        chunks = [slice(c, c + MXU_WIDTH) for c in range(0, D_EXPERT, MXU_WIDTH)]
        gu = [(jnp.dot(xb, wg_s[:, c], preferred_element_type=F32),
               jnp.dot(xb, wu_s[:, c], preferred_element_type=F32)) for c in chunks]
        y = None
        for c, (g, u) in zip(chunks, gu):
            hid = (g * jax.nn.sigmoid(g) * u).astype(BF16)
            part = jnp.dot(hid, wd_s[c, :], preferred_element_type=F32)
            y = part if y is None else y + part
        packed = _pack_rows(y)
        for c in range(SC_ROW_SPLIT):
            yb_ref[c] = packed[:, c * SC_ROW_WORDS:(c + 1) * SC_ROW_WORDS]


def _expert_call(block_expert, block_valid, xs, layer, w_gate, w_up, w_down):
    p_total = xs.shape[1]
    blk = MOE_BLOCK
    wmap = lambda i, be, nv: (layer, be[i], 0, 0)
    rows = lambda i, be, nv: (0, i, 0)
    return pl.pallas_call(
        _expert_kernel,
        grid_spec=pltpu.PrefetchScalarGridSpec(
            num_scalar_prefetch=2,
            grid=(p_total // blk,),
            in_specs=[
                pl.BlockSpec((SC_ROW_SPLIT, blk, SC_ROW_WORDS), rows),
                pl.BlockSpec((1, 1, D_MODEL, D_EXPERT), wmap),
                pl.BlockSpec((1, 1, D_MODEL, D_EXPERT), wmap),
                pl.BlockSpec((1, 1, D_EXPERT, D_MODEL), wmap),
            ],
            out_specs=pl.BlockSpec((SC_ROW_SPLIT, blk, SC_ROW_WORDS), rows),
            scratch_shapes=[
                pltpu.VMEM((D_MODEL, D_EXPERT), BF16),
                pltpu.VMEM((D_MODEL, D_EXPERT), BF16),
                pltpu.VMEM((D_EXPERT, D_MODEL), BF16),
            ],
        ),
        out_shape=jax.ShapeDtypeStruct((SC_ROW_SPLIT, p_total, SC_ROW_WORDS), jnp.uint32),
        compiler_params=_cparams(("arbitrary",)),
        name="moe_experts",
    )(block_expert, block_valid, xs, w_gate, w_up, w_down)


def _combined_rows(y_ref, rg_ref, piece_refs):
    rg = rg_ref[...]
    his, los = [], []
    for c in range(SC_ROW_SPLIT):
        hi0, lo0 = _unpack_rows(piece_refs[c * TOP_K][...])
        hi1, lo1 = _unpack_rows(piece_refs[c * TOP_K + 1][...])
        his.append(rg[:, 0:1] * hi0 + rg[:, 1:2] * hi1)
        los.append(rg[:, 0:1] * lo0 + rg[:, 1:2] * lo1)
    return y_ref[...] + jnp.concatenate(his + los, axis=1)


def _combine_kernel(y_ref, rg_ref, *refs):
    refs[-1][...] = _combined_rows(y_ref, rg_ref, refs[:-1])


def _combine_call(y2d, rg, rows):
    n = y2d.shape[0]
    tm = ROW_TILE
    tiles = n // tm
    return pl.pallas_call(
        _combine_kernel,
        grid=(tiles,),
        in_specs=[
            pl.BlockSpec((tm, D_MODEL), lambda i: (i, 0)),
            pl.BlockSpec((tm, LANES), lambda i: (i, 0)),
        ] + [
            pl.BlockSpec((tm, SC_ROW_WORDS), functools.partial(lambda i, j: (i + j * tiles, 0), j=j))
            for j in range(SC_ROW_SPLIT * TOP_K)
        ],
        out_specs=pl.BlockSpec((tm, D_MODEL), lambda i: (i, 0)),
        out_shape=jax.ShapeDtypeStruct((n, D_MODEL), F32),
        compiler_params=_cparams(("parallel",)),
        name="moe_combine",
    )(y2d, rg, *([rows] * (SC_ROW_SPLIT * TOP_K)))


def _slot_tables(ri, n):
    e = ri[:, :TOP_K]
    onehot = (e[:, :, None] == jnp.arange(N_EXPERTS, dtype=jnp.int32)).astype(jnp.int32)
    tok_cnt = onehot.sum(axis=1)
    t = 256
    cnt_t = tok_cnt.reshape(n // t, t, N_EXPERTS)
    tri = (jnp.arange(t)[:, None] > jnp.arange(t)[None, :]).astype(F32)
    within = jnp.einsum("ts,nsc->ntc", tri, cnt_t.astype(F32)).astype(jnp.int32)
    tile_sum = cnt_t.sum(axis=1)
    tile_base = jnp.cumsum(tile_sum, axis=0) - tile_sum
    rank = (within + tile_base[:, None, :]).reshape(n, N_EXPERTS)
    counts = tile_sum.sum(axis=0)
    padded = ((counts + MOE_BLOCK - 1) // MOE_BLOCK) * MOE_BLOCK
    pad_end = jnp.cumsum(padded)
    pad_start = pad_end - padded
    slot = ((rank + pad_start)[:, None, :] * onehot).sum(axis=-1)
    n_blocks = (n * TOP_K) // MOE_BLOCK + N_EXPERTS
    block_start = jnp.arange(n_blocks, dtype=jnp.int32) * MOE_BLOCK
    block_expert = jnp.minimum((pad_end[None, :] <= block_start[:, None]).sum(axis=-1),
                               N_EXPERTS - 1).astype(jnp.int32)
    used_end = (pad_start + counts)[block_expert]
    block_valid = jnp.clip(used_end - block_start, 0, MOE_BLOCK).astype(jnp.int32)
    p_total = n_blocks * MOE_BLOCK
    slot_km = slot.T.reshape(1, TOP_K * n).astype(jnp.int32)
    idx = jnp.concatenate([slot_km + c * p_total for c in range(SC_ROW_SPLIT)], axis=1)
    return idx, block_expert, block_valid, p_total


def _moe_rows(n, ri, hp, layer, experts):
    idx, block_expert, block_valid, p_total = _slot_tables(ri, n)
    xs = _sc_scatter_rows(hp.reshape(SC_ROW_SPLIT * n, SC_ROW_WORDS), idx, n // SC_WINDOW,
                          SC_ROW_SPLIT * p_total)
    yb = _expert_call(block_expert, block_valid, xs.reshape(SC_ROW_SPLIT, p_total, SC_ROW_WORDS),
                      layer, *experts)
    return _sc_gather_rows(yb.reshape(SC_ROW_SPLIT * p_total, SC_ROW_WORDS), idx)


def _rope_tables(seq):
    pos = jnp.arange(seq, dtype=F32)
    inv_freq = ROPE_THETA ** (-jnp.arange(0, ROPE_DIM, 2, dtype=F32) / ROPE_DIM)
    ang = pos[:, None] * inv_freq[None, :]
    cos = jnp.cos(ang)
    sin = jnp.sin(ang)
    half = ROPE_DIM // 2
    rest = HEAD_DIM - ROPE_DIM
    cos_h = jnp.concatenate([cos, cos, jnp.ones((seq, rest), F32)], axis=1)
    sa_h = jnp.concatenate([-sin, jnp.zeros((seq, half + rest), F32)], axis=1)
    sb_h = jnp.concatenate([jnp.zeros((seq, half), F32), sin, jnp.zeros((seq, rest), F32)], axis=1)
    tile = lambda t: jnp.tile(t, (1, HEADS_PER_GROUP))
    return tile(cos_h), tile(sa_h), tile(sb_h)


def _router_weights(w_rg, b_rg, w_re, b_re):
    w = jnp.concatenate([w_rg, w_re], axis=1)
    w = jnp.pad(w, ((0, 0), (0, LANES - w.shape[1])))
    w_hi = w.astype(BF16)
    w_lo = (w - w_hi.astype(F32)).astype(BF16)
    b = jnp.pad(jnp.concatenate([b_rg, b_re]), (0, LANES - N_EXPERT_GROUPS - N_EXPERTS))
    return jnp.concatenate([w_hi, w_lo], axis=1), b.reshape(1, LANES)


def _attention_layer(x, p):
    batch, seq, d = x.shape
    x2d = x.reshape(batch * seq, d)
    cos, sa, sb = _rope_tables(seq)
    qkv = _qkv_call(x2d, seq, p["ln_mix0"], p["w_qkv"], p["mseg"], p["qg"], p["kg"], cos, sa, sb)
    os_, lses = [], []
    for g, dil in enumerate(DILATIONS):
        o, lse = _attn_call(qkv[g], qkv[3 + g], qkv[6 + g], batch, seq, dil)
        os_.append(o)
        lses.append(lse)
    return _attn_out_call(os_, lses, x2d, p["w_o"], p["expand"], p["ln_ffn0"], p["wr0"], p["br0"])


def _conv_layer(state, rows, seq, p):
    y, _, rg, _ = state
    return _conv_call(y, rg, rows, seq, p["ln_mix1"], p["w_pw1"], p["b_pw1"], p["w_dw"], p["b_dw"],
                      p["conv_ln_g"], p["conv_ln_b"], p["w_pw2"], p["b_pw2"],
                      p["ln_ffn1"], p["wr1"], p["br1"])


def _experts_of(state, layer, p):
    y, ri, _, hp = state
    return _moe_rows(y.shape[0], ri, hp, layer, p["experts"])


def _encoder(xs, p):
    order = sorted(range(len(xs)), key=lambda i: xs[i].shape[0] * xs[i].shape[1])
    state = {i: _attention_layer(xs[i], p) for i in order}
    rows = {i: _experts_of(state[i], 0, p) for i in order}
    state = {i: _conv_layer(state[i], rows[i], xs[i].shape[1], p) for i in order}
    rows = {i: _experts_of(state[i], 1, p) for i in order}
    outs = {i: _combine_call(state[i][0], state[i][2], rows[i]).reshape(xs[i].shape) for i in order}
    return tuple(outs[i] for i in range(len(xs)))


def kernel(x_prompt, x_sample, ln_mix, ln_ffn, w_qkv, q_gain, k_gain, w_o, w_pw1, b_pw1, w_dw, b_dw,
           conv_ln_g, conv_ln_b, w_pw2, b_pw2, w_router_group, b_router_group, w_router_expert,
           b_router_expert, w_gate, w_up, w_down):
    row = lambda v: v.reshape(1, -1).astype(F32)
    n_heads = N_GROUPS * HEADS_PER_GROUP
    head_id = np.arange(ATTN_WIDTH) // HEAD_DIM
    mseg = jnp.asarray((head_id[:, None] == head_id[None, :]) / HEAD_DIM, BF16)
    slot_id = np.arange(GROUP_WIDTH) // HEAD_DIM
    expand = jnp.asarray(np.arange(LANES)[:, None] == slot_id[None, :], BF16)
    wr0, br0 = _router_weights(w_router_group[0], b_router_group[0], w_router_expert[0], b_router_expert[0])
    wr1, br1 = _router_weights(w_router_group[1], b_router_group[1], w_router_expert[1], b_router_expert[1])
    p = dict(
        ln_mix0=row(ln_mix[0]), ln_mix1=row(ln_mix[1]), ln_ffn0=row(ln_ffn[0]), ln_ffn1=row(ln_ffn[1]),
        w_qkv=w_qkv[0].astype(BF16), mseg=mseg, expand=expand,
        qg=row(jnp.tile(q_gain[0], n_heads) * (HEAD_DIM ** -0.5 * LOG2E)), kg=row(jnp.tile(k_gain[0], n_heads)),
        w_o=w_o[0].astype(BF16),
        w_pw1=w_pw1[0].astype(BF16), b_pw1=row(b_pw1[0]), w_dw=w_dw[0].astype(F32), b_dw=row(b_dw[0]),
        conv_ln_g=row(conv_ln_g[0]), conv_ln_b=row(conv_ln_b[0]),
        w_pw2=w_pw2[0].astype(BF16), b_pw2=row(b_pw2[0]),
        wr0=wr0, br0=br0, wr1=wr1, br1=br1,
        experts=(w_gate, w_up, w_down),
    )
    return _encoder((x_prompt, x_sample), p)
```

```python
import functools

import jax
import jax.numpy as jnp
import numpy as np
from jax import lax
from jax.experimental import pallas as pl
from jax.experimental.pallas import tpu as pltpu
from jax.experimental.pallas import tpu_sc as plsc

D_MODEL = 1024
HEAD_DIM = 64
HEADS_PER_GROUP = 4
DILATIONS = (1, 4, 16)
HALF_WINDOW = 64
N_GROUPS = len(DILATIONS)
GROUP_WIDTH = HEADS_PER_GROUP * HEAD_DIM
ATTN_WIDTH = N_GROUPS * GROUP_WIDTH
ROPE_DIM = HEAD_DIM // 4
ROPE_THETA = 500000.0
CONV_WIDTH = 31
CONV_HALO = 16
N_EXPERT_GROUPS = 4
EXPERTS_PER_GROUP = 8
N_EXPERTS = N_EXPERT_GROUPS * EXPERTS_PER_GROUP
TOP_K = 2
D_EXPERT = 512
EPS = 1e-6
NEG_INF = -1e30
LOG2E = 1.4426950408889634
LN2 = 0.6931471805599453

LANES = 128
SUBLANES = 8
MXU_WIDTH = 256
ROW_TILE = 512
ATTN_Q_TILE = 128
ATTN_STEP_ROWS = 512
MOE_BLOCK = 512
VMEM_LIMIT = 56 * 1024 * 1024

PACKED_WIDTH = D_MODEL // 2
SC_WINDOW = 128
SC_ROW_SPLIT = 2
SC_ROW_WORDS = PACKED_WIDTH // SC_ROW_SPLIT

F32 = jnp.float32
BF16 = jnp.bfloat16


def _cparams(sem):
    return pltpu.CompilerParams(dimension_semantics=sem, vmem_limit_bytes=VMEM_LIMIT)


def _rms(x, gain):
    return x * lax.rsqrt(jnp.mean(x * x, axis=-1, keepdims=True) + EPS) * gain


def _pack_rows(x):
    bits = pltpu.bitcast(x.astype(BF16).astype(F32), jnp.uint32)
    half = x.shape[1] // 2
    return bits[:, :half] | (bits[:, half:] >> 16)


def _unpack_rows(w):
    hi = pltpu.bitcast(w & jnp.uint32(0xFFFF0000), F32)
    lo = pltpu.bitcast(w << 16, F32)
    return hi, lo


def _store_by_residue(out_ref, val, dil, scr):
    if dil == 1:
        out_ref[...] = val.astype(out_ref.dtype)
        return
    rows, width = val.shape
    for c in range(width // LANES):
        scr[c] = val[:, c * LANES:(c + 1) * LANES]
    for r in range(dil):
        for c in range(width // LANES):
            lanes = slice(r * width + c * LANES, r * width + (c + 1) * LANES)
            out_ref[:, lanes] = scr[c, pl.ds(r, rows // dil, stride=dil), :].astype(out_ref.dtype)


def _qkv_kernel(x_ref, g_ref, w_ref, mseg_ref, qg_ref, kg_ref, cos_ref, sa_ref, sb_ref,
                q0, q1, q2, k0, k1, k2, v0, v1, v2, *scrs):
    h = _rms(x_ref[...], g_ref[...]).astype(BF16)
    cos = cos_ref[...]
    sa = sa_ref[...]
    sb = sb_ref[...]
    scrs = iter(scrs)

    def store(out_ref, val, g):
        _store_by_residue(out_ref, val, DILATIONS[g], next(scrs) if DILATIONS[g] > 1 else None)

    q = jnp.dot(h, w_ref[:, 0:ATTN_WIDTH], preferred_element_type=F32)
    k = jnp.dot(h, w_ref[:, ATTN_WIDTH:2 * ATTN_WIDTH], preferred_element_type=F32)
    v = jnp.dot(h, w_ref[:, 2 * ATTN_WIDTH:3 * ATTN_WIDTH], preferred_element_type=F32)
    ms_q = jnp.dot((q * q).astype(BF16), mseg_ref[...], preferred_element_type=F32)
    ms_k = jnp.dot((k * k).astype(BF16), mseg_ref[...], preferred_element_type=F32)
    for g, o in enumerate((v0, v1, v2)):
        store(o, v[:, g * GROUP_WIDTH:(g + 1) * GROUP_WIDTH], g)
    for t, ms, gain, outs in ((q, ms_q, qg_ref[...], (q0, q1, q2)), (k, ms_k, kg_ref[...], (k0, k1, k2))):
        tn = t * lax.rsqrt(ms + EPS) * gain
        for g in range(N_GROUPS):
            c = tn[:, g * GROUP_WIDTH:(g + 1) * GROUP_WIDTH]
            r = (c * cos + pltpu.roll(c, GROUP_WIDTH - ROPE_DIM // 2, 1) * sa
                 + pltpu.roll(c, ROPE_DIM // 2, 1) * sb)
            store(outs[g], r, g)


def _qkv_call(x2d, seq, gain, w_qkv, mseg, qg, kg, cos, sa, sb):
    n = x2d.shape[0]
    tm = ROW_TILE
    tiles_per_seq = seq // tm
    row = lambda i: (i, 0)
    const = lambda i: (0, 0)
    tab = lambda i: (i % tiles_per_seq, 0)
    out_specs = [pl.BlockSpec((tm // dil, dil * GROUP_WIDTH), row) for dil in DILATIONS] * 3
    out_shape = [jax.ShapeDtypeStruct((n // dil, dil * GROUP_WIDTH), BF16) for dil in DILATIONS] * 3
    return pl.pallas_call(
        _qkv_kernel,
        grid=(n // tm,),
        in_specs=[
            pl.BlockSpec((tm, D_MODEL), row),
            pl.BlockSpec((1, D_MODEL), const),
            pl.BlockSpec((D_MODEL, 3 * ATTN_WIDTH), const),
            pl.BlockSpec((ATTN_WIDTH, ATTN_WIDTH), const),
            pl.BlockSpec((1, ATTN_WIDTH), const),
            pl.BlockSpec((1, ATTN_WIDTH), const),
            pl.BlockSpec((tm, GROUP_WIDTH), tab),
            pl.BlockSpec((tm, GROUP_WIDTH), tab),
            pl.BlockSpec((tm, GROUP_WIDTH), tab),
        ],
        out_specs=out_specs,
        out_shape=out_shape,
        scratch_shapes=[pltpu.VMEM((GROUP_WIDTH // LANES, tm, LANES), F32)]
        * (3 * sum(dil > 1 for dil in DILATIONS)),
        compiler_params=_cparams(("parallel",)),
        name="qkv_proj",
    )(x2d, gain, w_qkv, mseg, qg, kg, cos, sa, sb)


def _attn_kernel(q_ref, k_ref, v_ref, o_ref, lse_ref, *, ls, rows, res, win):
    i = pl.program_id(2)
    tq = ATTN_Q_TILE
    rel = (lax.broadcasted_iota(jnp.int32, (tq, win), 0) - lax.broadcasted_iota(jnp.int32, (tq, win), 1))
    lane = lax.broadcasted_iota(jnp.int32, (tq, LANES), 1)
    lane_kv = lax.broadcasted_iota(jnp.int32, (win, LANES), 1)

    blocks = []
    for r in range(res):
        for sb in range(rows // tq):
            q_start = i * rows + sb * tq
            k_start = pl.multiple_of(jnp.clip(q_start - HALF_WINDOW, 0, ls - win), HALF_WINDOW)
            mask = jnp.abs(rel + (q_start - k_start)) <= HALF_WINDOW
            blocks.append((r, slice(sb * tq, (sb + 1) * tq), k_start, mask))
    chains = [(blk, h) for blk in blocks for h in range(HEADS_PER_GROUP)]

    def cols(r, h):
        return slice(r * GROUP_WIDTH + h * HEAD_DIM, r * GROUP_WIDTH + (h + 1) * HEAD_DIM)

    scores = [lax.dot_general(q_ref[0, q_rows, cols(r, h)], k_ref[0, pl.ds(k_start, win), cols(r, h)],
                              (((1,), (1,)), ((), ())), preferred_element_type=F32)
              for (r, q_rows, k_start, _), h in chains]
    maxes, probs = [], []
    for ((_, _, _, mask), _), s in zip(chains, scores):
        s = jnp.where(mask, s, NEG_INF)
        m = jnp.max(s, axis=-1, keepdims=True)
        maxes.append(m)
        probs.append(jnp.exp2(s - m).astype(BF16))
    outs = []
    for ((r, _, k_start, _), h), p in zip(chains, probs):
        pair = slice(r * GROUP_WIDTH + (h // 2) * LANES, r * GROUP_WIDTH + (h // 2 + 1) * LANES)
        own_half = (lane_kv < HEAD_DIM) if h % 2 == 0 else (lane_kv >= HEAD_DIM)
        v_ext = jnp.where(own_half, v_ref[0, pl.ds(k_start, win), pair], jnp.ones((), BF16))
        outs.append(jnp.dot(p, v_ext, preferred_element_type=F32))
    lse_tiles = {}
    for ((r, q_rows, _, _), h), o_ext, m in zip(chains, outs, maxes):
        den_ext = pltpu.roll(o_ext, HEAD_DIM, 1)
        own = slice((h % 2) * HEAD_DIM, (h % 2 + 1) * HEAD_DIM)
        o_ref[0, q_rows, cols(r, h)] = (o_ext / den_ext)[:, own].astype(BF16)
        den_at_h = den_ext if h % 2 == 0 else o_ext
        key = (r, q_rows.start)
        lse_tiles[key] = jnp.where(lane == h, (m + jnp.log2(den_at_h)) * LN2,
                                   lse_tiles.get(key, jnp.zeros((tq, LANES), F32)))
    for (r, q0), tile in lse_tiles.items():
        lse_ref[0, q0:q0 + tq, r * LANES:(r + 1) * LANES] = tile


def _attn_call(q, k, v, batch, seq, dil):
    ls = seq // dil
    rows = min(ATTN_STEP_ROWS, ls)
    res = max(1, min(dil, ATTN_STEP_ROWS // rows))
    win = min(ATTN_Q_TILE + 2 * HALF_WINDOW, ls)
    view = lambda t: t.reshape(batch, ls, dil * GROUP_WIDTH)
    qmap = lambda b, r, i: (b, i, r)
    kvmap = lambda b, r, i: (b, 0, r)
    o, lse = pl.pallas_call(
        functools.partial(_attn_kernel, ls=ls, rows=rows, res=res, win=win),
        grid=(batch, dil // res, ls // rows),
        in_specs=[
            pl.BlockSpec((1, rows, res * GROUP_WIDTH), qmap),
            pl.BlockSpec((1, ls, res * GROUP_WIDTH), kvmap),
            pl.BlockSpec((1, ls, res * GROUP_WIDTH), kvmap),
        ],
        out_specs=[
            pl.BlockSpec((1, rows, res * GROUP_WIDTH), qmap),
            pl.BlockSpec((1, rows, res * LANES), qmap),
        ],
        out_shape=[
            jax.ShapeDtypeStruct((batch, ls, dil * GROUP_WIDTH), BF16),
            jax.ShapeDtypeStruct((batch, ls, dil * LANES), F32),
        ],
        compiler_params=_cparams(("parallel", "parallel", "arbitrary")),
        name=f"banded_attn_d{dil}",
    )(view(q), view(k), view(v))
    return o.reshape(batch * ls, dil * GROUP_WIDTH), lse.reshape(batch * ls, dil * LANES)


def _route_logits(x, gain, wr_ref, br_ref, hp_ref, rows):
    h = _rms(x, gain)
    packed = _pack_rows(h)
    for c in range(SC_ROW_SPLIT):
        hp_ref[c, rows, :] = packed[:, c * SC_ROW_WORDS:(c + 1) * SC_ROW_WORDS]
    h_hi = h.astype(BF16)
    h_lo = (h - h_hi.astype(F32)).astype(BF16)
    acc = (jnp.dot(h_hi, wr_ref[...], preferred_element_type=F32)
           + jnp.dot(h_lo, wr_ref[...], preferred_element_type=F32))
    return acc[:, :LANES] + acc[:, LANES:] + br_ref[...]


def _route_decide(logits, ri_ref, rg_ref, rows):
    lane = lax.broadcasted_iota(jnp.int32, logits.shape, 1)
    lane_f = lane.astype(F32)
    big = jnp.float32(2 * LANES)

    is_g = lane < N_EXPERT_GROUPS
    gl = jnp.where(is_g, logits, NEG_INF)
    gmax = jnp.max(gl, axis=-1, keepdims=True)
    grp = jnp.min(jnp.where(gl == gmax, lane_f, big), axis=-1, keepdims=True)
    p_grp = 1.0 / jnp.sum(jnp.where(is_g, jnp.exp(gl - gmax), 0.0), axis=-1, keepdims=True)

    lo = N_EXPERT_GROUPS + grp * EXPERTS_PER_GROUP
    in_grp = (lane_f >= lo) & (lane_f < lo + EXPERTS_PER_GROUP)
    el = jnp.where(in_grp, logits, NEG_INF)
    v1 = jnp.max(el, axis=-1, keepdims=True)
    i1 = jnp.min(jnp.where(el == v1, lane_f, big), axis=-1, keepdims=True)
    el2 = jnp.where(lane_f == i1, NEG_INF, el)
    v2 = jnp.max(el2, axis=-1, keepdims=True)
    i2 = jnp.min(jnp.where(el2 == v2, lane_f, big), axis=-1, keepdims=True)
    e21 = jnp.exp(v2 - v1)
    g1 = p_grp / (1.0 + e21)
    g2 = g1 * e21
    ri = jnp.where(lane == 0, i1 - N_EXPERT_GROUPS, jnp.where(lane == 1, i2 - N_EXPERT_GROUPS, 0.0))
    ri_ref[rows, :] = ri.astype(jnp.int32)
    rg_ref[rows, :] = jnp.where(lane == 0, g1, jnp.where(lane == 1, g2, 0.0))


def _row_halves(tm):
    return [slice(0, tm // 2), slice(tm // 2, tm)]


def _route_out_specs(n, tm):
    row = lambda i: (i, 0)
    specs = [
        pl.BlockSpec((tm, D_MODEL), row),
        pl.BlockSpec((tm, LANES), row),
        pl.BlockSpec((tm, LANES), row),
        pl.BlockSpec((SC_ROW_SPLIT, tm, SC_ROW_WORDS), lambda i: (0, i, 0)),
    ]
    shapes = [
        jax.ShapeDtypeStruct((n, D_MODEL), F32),
        jax.ShapeDtypeStruct((n, LANES), jnp.int32),
        jax.ShapeDtypeStruct((n, LANES), F32),
        jax.ShapeDtypeStruct((SC_ROW_SPLIT, n, SC_ROW_WORDS), jnp.uint32),
    ]
    return specs, shapes


def _load_by_residue(in_ref, dil, scr):
    if dil == 1:
        return in_ref[...].astype(F32)
    chunks, rows, _ = scr.shape
    width = chunks * LANES
    for r in range(dil):
        for c in range(chunks):
            lanes = slice(r * width + c * LANES, r * width + (c + 1) * LANES)
            scr[c, pl.ds(r, rows // dil, stride=dil), :] = in_ref[:, lanes].astype(F32)
    return jnp.concatenate([scr[c] for c in range(chunks)], axis=1) if chunks > 1 else scr[0]


def _attn_out_kernel(o0, o1, o2, l0, l1, l2, x_ref, wo_ref, ex_ref, g_ref, wr_ref, br_ref,
                     y_ref, ri_ref, rg_ref, hp_ref, o_scr1, o_scr2, l_scr1, l_scr2):
    ls = [_load_by_residue(l, dil, scr)
          for l, dil, scr in zip((l0, l1, l2), DILATIONS, (None, l_scr1, l_scr2))]
    m = jnp.maximum(jnp.maximum(ls[0], ls[1]), ls[2])
    es = [jnp.exp(l - m) for l in ls]
    inv = 1.0 / (es[0] + es[1] + es[2])
    a_wide = []
    for g in range(N_GROUPS):
        alpha = es[g] * inv
        a_hi = alpha.astype(BF16)
        a_lo = (alpha - a_hi.astype(F32)).astype(BF16)
        a_wide.append(jnp.dot(a_hi, ex_ref[...], preferred_element_type=F32)
                      + jnp.dot(a_lo, ex_ref[...], preferred_element_type=F32))
    scaled = jnp.concatenate(
        [(_load_by_residue(o_ref, dil, scr) * a).astype(BF16)
         for o_ref, dil, scr, a in zip((o0, o1, o2), DILATIONS, (None, o_scr1, o_scr2), a_wide)], axis=1)
    halves = _row_halves(x_ref.shape[0])
    ys = [x_ref[rows, :] + jnp.dot(scaled[rows, :], wo_ref[...], preferred_element_type=F32)
          for rows in halves]
    for rows, y in zip(halves, ys):
        y_ref[rows, :] = y
    logits = [_route_logits(y, g_ref[...], wr_ref, br_ref, hp_ref, rows) for rows, y in zip(halves, ys)]
    for rows, lg in zip(halves, logits):
        _route_decide(lg, ri_ref, rg_ref, rows)


def _attn_out_call(os_, lses, x2d, w_o, expand, gain, wr, br):
    n = x2d.shape[0]
    tm = ROW_TILE
    row = lambda i: (i, 0)
    const = lambda i: (0, 0)
    out_specs, out_shape = _route_out_specs(n, tm)
    return pl.pallas_call(
        _attn_out_kernel,
        grid=(n // tm,),
        in_specs=(
            [pl.BlockSpec((tm // dil, dil * GROUP_WIDTH), row) for dil in DILATIONS]
            + [pl.BlockSpec((tm // dil, dil * LANES), row) for dil in DILATIONS]
            + [
                pl.BlockSpec((tm, D_MODEL), row),
                pl.BlockSpec((ATTN_WIDTH, D_MODEL), const),
                pl.BlockSpec((LANES, GROUP_WIDTH), const),
                pl.BlockSpec((1, D_MODEL), const),
                pl.BlockSpec((D_MODEL, 2 * LANES), const),
                pl.BlockSpec((1, LANES), const),
            ]
        ),
        out_specs=out_specs,
        out_shape=out_shape,
        scratch_shapes=([pltpu.VMEM((GROUP_WIDTH // LANES, tm, LANES), F32)] * 2
                        + [pltpu.VMEM((1, tm, LANES), F32)] * 2),
        compiler_params=_cparams(("parallel",)),
        name="attn_out_router",
    )(*os_, *lses, x2d, w_o, expand, gain, wr, br)


N_MOE_REFS = 2 + SC_ROW_SPLIT * TOP_K


def _conv_kernel(*refs, seq, tm):
    cur, prev, nxt = (refs[b * N_MOE_REFS:(b + 1) * N_MOE_REFS] for b in range(3))
    (gm_ref, w1_ref, b1_ref, wdw_ref, bdw_ref, lng_ref, lnb_ref, w2_ref, b2_ref, gf_ref, wr_ref, br_ref,
     y_ref, ri_ref, rg_ref, hp_ref, x_scr, u_scr, c_scr, us_scr, cs_scr) = refs[3 * N_MOE_REFS:]
    i = pl.program_id(0)
    pos0 = (i * tm) % seq
    hl = CONV_HALO
    x_scr[0:hl, :] = _combined_rows(prev[0], prev[1], prev[2:])
    x_scr[hl:hl + tm, :] = _combined_rows(cur[0], cur[1], cur[2:])
    x_scr[hl + tm:hl + tm + hl, :] = _combined_rows(nxt[0], nxt[1], nxt[2:])
    halves = _row_halves(tm)
    first = hl - CONV_WIDTH // 2
    r_all = lax.broadcasted_iota(jnp.int32, (tm + 2 * hl, 1), 0)
    outside = ((r_all < hl) & (pos0 == 0)) | ((r_all >= hl + tm) & (pos0 + tm == seq))
    u_parts = [slice(0, tm // 2 + 2 * hl), slice(tm // 2 + 2 * hl, tm + 2 * hl)]
    gated = []
    for part in u_parts:
        h = _rms(x_scr[part, :], gm_ref[...]).astype(BF16)
        a = jnp.dot(h, w1_ref[...], preferred_element_type=F32) + b1_ref[...]
        gated.append(a)
    for part, a in zip(u_parts, gated):
        u = a[:, :D_MODEL] * jax.nn.sigmoid(a[:, D_MODEL:])
        u = jnp.where(outside[part, :], 0.0, u)
        for c in range(D_MODEL // LANES):
            u_scr[c, part, :] = u[:, c * LANES:(c + 1) * LANES]

    seg = (tm // 2) // SUBLANES
    seg_tiles = seg + 2 * hl
    group = SUBLANES
    logits = []
    for rows in halves:
        for c in range(D_MODEL // LANES):
            cols = slice(c * LANES, (c + 1) * LANES)
            for sgm in range(SUBLANES):
                r0 = rows.start + seg * sgm
                us_scr[c, pl.ds(sgm, seg_tiles, stride=SUBLANES), :] = u_scr[c, r0:r0 + seg_tiles, :]
            for k0 in range(0, seg, group):
                accs = [jnp.zeros((SUBLANES, LANES), F32) + bdw_ref[:, cols] for _ in range(group)]
                for t in range(CONV_WIDTH):
                    w_t = wdw_ref[t:t + 1, cols]
                    for j in range(group):
                        tile = (k0 + j + t + first) * SUBLANES
                        accs[j] = accs[j] + us_scr[c, tile:tile + SUBLANES, :] * w_t
                for j in range(group):
                    cs_scr[c, (k0 + j) * SUBLANES:(k0 + j + 1) * SUBLANES, :] = accs[j]
            for sgm in range(SUBLANES):
                r0 = rows.start + seg * sgm
                c_scr[c, r0:r0 + seg, :] = cs_scr[c, pl.ds(sgm, seg, stride=SUBLANES), :]
        cv = jnp.concatenate([c_scr[c, rows, :] for c in range(D_MODEL // LANES)], axis=1)
        mu = jnp.mean(cv, axis=-1, keepdims=True)
        xc = cv - mu
        var = jnp.mean(xc * xc, axis=-1, keepdims=True)
        ln = xc * lax.rsqrt(var + EPS) * lng_ref[...] + lnb_ref[...]
        act = (ln * jax.nn.sigmoid(ln)).astype(BF16)
        y = (x_scr[hl + rows.start:hl + rows.stop, :]
             + jnp.dot(act, w2_ref[...], preferred_element_type=F32) + b2_ref[...])
        y_ref[rows, :] = y
        logits.append(_route_logits(y, gf_ref[...], wr_ref, br_ref, hp_ref, rows))
    for rows, lg in zip(halves, logits):
        _route_decide(lg, ri_ref, rg_ref, rows)


def _conv_call(y2d, rg, rows, seq, gm, w1, b1, wdw, bdw, lng, lnb, w2, b2, gf, wr, br):
    n = y2d.shape[0]
    tm = ROW_TILE
    hl = CONV_HALO
    per = tm // hl
    tiles = n // tm
    const = lambda i: (0, 0)
    cur_blk = lambda i: i
    prev_blk = lambda i: jnp.maximum(i * per - 1, 0)
    next_blk = lambda i: jnp.minimum((i + 1) * per, n // hl - 1)

    def moe_specs(rows_blk, blk_of, blocks_per_piece):
        at = lambda off: (lambda i: (blk_of(i) + off, 0))
        return ([pl.BlockSpec((rows_blk, D_MODEL), at(0)), pl.BlockSpec((rows_blk, LANES), at(0))]
                + [pl.BlockSpec((rows_blk, SC_ROW_WORDS), at(j * blocks_per_piece))
                   for j in range(SC_ROW_SPLIT * TOP_K)])

    moe_args = [y2d, rg] + [rows] * (SC_ROW_SPLIT * TOP_K)
    out_specs, out_shape = _route_out_specs(n, tm)
    return pl.pallas_call(
        functools.partial(_conv_kernel, seq=seq, tm=tm),
        grid=(tiles,),
        in_specs=(
            moe_specs(tm, cur_blk, tiles) + moe_specs(hl, prev_blk, tiles * per)
            + moe_specs(hl, next_blk, tiles * per)
            + [
                pl.BlockSpec((1, D_MODEL), const),
                pl.BlockSpec((D_MODEL, 2 * D_MODEL), const),
                pl.BlockSpec((1, 2 * D_MODEL), const),
                pl.BlockSpec((CONV_WIDTH, D_MODEL), const),
                pl.BlockSpec((1, D_MODEL), const),
                pl.BlockSpec((1, D_MODEL), const),
                pl.BlockSpec((1, D_MODEL), const),
                pl.BlockSpec((D_MODEL, D_MODEL), const),
                pl.BlockSpec((1, D_MODEL), const),
                pl.BlockSpec((1, D_MODEL), const),
                pl.BlockSpec((D_MODEL, 2 * LANES), const),
                pl.BlockSpec((1, LANES), const),
            ]
        ),
        out_specs=out_specs,
        out_shape=out_shape,
        scratch_shapes=[
            pltpu.VMEM((tm + 2 * hl, D_MODEL), F32),
            pltpu.VMEM((D_MODEL // LANES, tm + 2 * hl, LANES), F32),
            pltpu.VMEM((D_MODEL // LANES, tm, LANES), F32),
            pltpu.VMEM((D_MODEL // LANES, (tm // 2 // SUBLANES + 2 * hl) * SUBLANES, LANES), F32),
            pltpu.VMEM((D_MODEL // LANES, tm // 2, LANES), F32),
        ],
        compiler_params=_cparams(("parallel",)),
        name="conv_module_router",
    )(*moe_args, *moe_args, *moe_args, gm, w1, b1, wdw, bdw, lng, lnb, w2, b2, gf, wr, br)


def _sc_mesh():
    return plsc.VectorSubcoreMesh(core_axis_name="core", subcore_axis_name="subcore")


def _sc_scatter_rows(src, idx, n_src_windows, out_rows):
    m = idx.shape[1]
    width = src.shape[1]
    nsw = n_src_windows

    @functools.partial(pl.kernel, out_type=jax.ShapeDtypeStruct((out_rows, width), src.dtype),
                       mesh=_sc_mesh(), name="sc_scatter_rows")
    def scatter(src_hbm, idx_hbm, out_hbm):
        def body(src_vmem, idx_vmem):
            pltpu.sync_copy(src_vmem, out_hbm.at[idx_vmem.at[0]])

        pltpu.emit_pipeline(
            body,
            grid=(m // SC_WINDOW,),
            in_specs=[
                pl.BlockSpec((SC_WINDOW, width), lambda w: ((w // (TOP_K * nsw)) * nsw + w % nsw, 0)),
                pl.BlockSpec((1, SC_WINDOW), lambda w: (0, w)),
            ],
            out_specs=[],
            core_axis_name=("core", "subcore"),
            dimension_semantics=(pltpu.PARALLEL,),
        )(src_hbm, idx_hbm)

    return scatter(src, idx)


def _sc_gather_rows(table, idx):
    m = idx.shape[1]
    width = table.shape[1]

    @functools.partial(pl.kernel, out_type=jax.ShapeDtypeStruct((m, width), table.dtype),
                       mesh=_sc_mesh(), name="sc_gather_rows")
    def gather(table_hbm, idx_hbm, out_hbm):
        def body(idx_vmem, out_vmem):
            pltpu.sync_copy(table_hbm.at[idx_vmem.at[0]], out_vmem)

        pltpu.emit_pipeline(
            body,
            grid=(m // SC_WINDOW,),
            in_specs=[pl.BlockSpec((1, SC_WINDOW), lambda w: (0, w))],
            out_specs=[pl.BlockSpec((SC_WINDOW, width), lambda w: (w, 0))],
            core_axis_name=("core", "subcore"),
            dimension_semantics=(pltpu.PARALLEL,),
        )(idx_hbm, out_hbm)

    return gather(table, idx)


def _expert_kernel(be_ref, nv_ref, xs_ref, wg_ref, wu_ref, wd_ref, yb_ref, wg_s, wu_s, wd_s):
    i = pl.program_id(0)

    @pl.when((i == 0) | (be_ref[i] != be_ref[jnp.maximum(i - 1, 0)]))
    def _():
        wg_s[...] = wg_ref[0, 0].astype(BF16)
        wu_s[...] = wu_ref[0, 0].astype(BF16)
        wd_s[...] = wd_ref[0, 0].astype(BF16)

    @pl.when(nv_ref[i] > 0)
    def _():
        words = jnp.concatenate([xs_ref[c] for c in range(SC_ROW_SPLIT)], axis=1)
        row = lax.broadcasted_iota(jnp.int32, (words.shape[0], 1), 0)
        words = jnp.where(row < nv_ref[i], words, jnp.uint32(0))
        hi, lo = _unpack_rows(words)
        xb = jnp.concatenate([hi.astype(BF16), lo.astype(BF16)], axis=1)
        chunks = [slice(c, c + MXU_WIDTH) for c in range(0, D_EXPERT, MXU_WIDTH)]
        gu = [(jnp.dot(xb, wg_s[:, c], preferred_element_type=F32),
               jnp.dot(xb, wu_s[:, c], preferred_element_type=F32)) for c in chunks]
        y = None
        for c, (g, u) in zip(chunks, gu):
            hid = (g * jax.nn.sigmoid(g) * u).astype(BF16)
            part = jnp.dot(hid, wd_s[c, :], preferred_element_type=F32)
            y = part if y is None else y + part
        packed = _pack_rows(y)
        for c in range(SC_ROW_SPLIT):
            yb_ref[c] = packed[:, c * SC_ROW_WORDS:(c + 1) * SC_ROW_WORDS]


def _expert_call(block_expert, block_valid, xs, layer, w_gate, w_up, w_down):
    p_total = xs.shape[1]
    blk = MOE_BLOCK
    wmap = lambda i, be, nv: (layer, be[i], 0, 0)
    rows = lambda i, be, nv: (0, i, 0)
    return pl.pallas_call(
        _expert_kernel,
        grid_spec=pltpu.PrefetchScalarGridSpec(
            num_scalar_prefetch=2,
            grid=(p_total // blk,),
            in_specs=[
                pl.BlockSpec((SC_ROW_SPLIT, blk, SC_ROW_WORDS), rows),
                pl.BlockSpec((1, 1, D_MODEL, D_EXPERT), wmap),
                pl.BlockSpec((1, 1, D_MODEL, D_EXPERT), wmap),
                pl.BlockSpec((1, 1, D_EXPERT, D_MODEL), wmap),
            ],
            out_specs=pl.BlockSpec((SC_ROW_SPLIT, blk, SC_ROW_WORDS), rows),
            scratch_shapes=[
                pltpu.VMEM((D_MODEL, D_EXPERT), BF16),
                pltpu.VMEM((D_MODEL, D_EXPERT), BF16),
                pltpu.VMEM((D_EXPERT, D_MODEL), BF16),
            ],
        ),
        out_shape=jax.ShapeDtypeStruct((SC_ROW_SPLIT, p_total, SC_ROW_WORDS), jnp.uint32),
        compiler_params=_cparams(("arbitrary",)),
        name="moe_experts",
    )(block_expert, block_valid, xs, w_gate, w_up, w_down)


def _combined_rows(y_ref, rg_ref, piece_refs):
    rg = rg_ref[...]
    his, los = [], []
    for c in range(SC_ROW_SPLIT):
        hi0, lo0 = _unpack_rows(piece_refs[c * TOP_K][...])
        hi1, lo1 = _unpack_rows(piece_refs[c * TOP_K + 1][...])
        his.append(rg[:, 0:1] * hi0 + rg[:, 1:2] * hi1)
        los.append(rg[:, 0:1] * lo0 + rg[:, 1:2] * lo1)
    return y_ref[...] + jnp.concatenate(his + los, axis=1)


def _combine_kernel(y_ref, rg_ref, *refs):
    refs[-1][...] = _combined_rows(y_ref, rg_ref, refs[:-1])


def _combine_call(y2d, rg, rows):
    n = y2d.shape[0]
    tm = ROW_TILE
    tiles = n // tm
    return pl.pallas_call(
        _combine_kernel,
        grid=(tiles,),
        in_specs=[
            pl.BlockSpec((tm, D_MODEL), lambda i: (i, 0)),
            pl.BlockSpec((tm, LANES), lambda i: (i, 0)),
        ] + [
            pl.BlockSpec((tm, SC_ROW_WORDS), functools.partial(lambda i, j: (i + j * tiles, 0), j=j))
            for j in range(SC_ROW_SPLIT * TOP_K)
        ],
        out_specs=pl.BlockSpec((tm, D_MODEL), lambda i: (i, 0)),
        out_shape=jax.ShapeDtypeStruct((n, D_MODEL), F32),
        compiler_params=_cparams(("parallel",)),
        name="moe_combine",
    )(y2d, rg, *([rows] * (SC_ROW_SPLIT * TOP_K)))


def _slot_tables(ri, n):
    e = ri[:, :TOP_K]
    onehot = (e[:, :, None] == jnp.arange(N_EXPERTS, dtype=jnp.int32)).astype(jnp.int32)
    tok_cnt = onehot.sum(axis=1)
    t = 256
    cnt_t = tok_cnt.reshape(n // t, t, N_EXPERTS)
    tri = (jnp.arange(t)[:, None] > jnp.arange(t)[None, :]).astype(F32)
    within = jnp.einsum("ts,nsc->ntc", tri, cnt_t.astype(F32)).astype(jnp.int32)
    tile_sum = cnt_t.sum(axis=1)
    tile_base = jnp.cumsum(tile_sum, axis=0) - tile_sum
    rank = (within + tile_base[:, None, :]).reshape(n, N_EXPERTS)
    counts = tile_sum.sum(axis=0)
    padded = ((counts + MOE_BLOCK - 1) // MOE_BLOCK) * MOE_BLOCK
    pad_end = jnp.cumsum(padded)
    pad_start = pad_end - padded
    slot = ((rank + pad_start)[:, None, :] * onehot).sum(axis=-1)
    n_blocks = (n * TOP_K) // MOE_BLOCK + N_EXPERTS
    block_start = jnp.arange(n_blocks, dtype=jnp.int32) * MOE_BLOCK
    block_expert = jnp.minimum((pad_end[None, :] <= block_start[:, None]).sum(axis=-1),
                               N_EXPERTS - 1).astype(jnp.int32)
    used_end = (pad_start + counts)[block_expert]
    block_valid = jnp.clip(used_end - block_start, 0, MOE_BLOCK).astype(jnp.int32)
    p_total = n_blocks * MOE_BLOCK
    slot_km = slot.T.reshape(1, TOP_K * n).astype(jnp.int32)
    idx = jnp.concatenate([slot_km + c * p_total for c in range(SC_ROW_SPLIT)], axis=1)
    return idx, block_expert, block_valid, p_total


def _moe_rows(n, ri, hp, layer, experts):
    idx, block_expert, block_valid, p_total = _slot_tables(ri, n)
    xs = _sc_scatter_rows(hp.reshape(SC_ROW_SPLIT * n, SC_ROW_WORDS), idx, n // SC_WINDOW,
                          SC_ROW_SPLIT * p_total)
    yb = _expert_call(block_expert, block_valid, xs.reshape(SC_ROW_SPLIT, p_total, SC_ROW_WORDS),
                      layer, *experts)
    return _sc_gather_rows(yb.reshape(SC_ROW_SPLIT * p_total, SC_ROW_WORDS), idx)


def _rope_tables(seq):
    pos = jnp.arange(seq, dtype=F32)
    inv_freq = ROPE_THETA ** (-jnp.arange(0, ROPE_DIM, 2, dtype=F32) / ROPE_DIM)
    ang = pos[:, None] * inv_freq[None, :]
    cos = jnp.cos(ang)
    sin = jnp.sin(ang)
    half = ROPE_DIM // 2
    rest = HEAD_DIM - ROPE_DIM
    cos_h = jnp.concatenate([cos, cos, jnp.ones((seq, rest), F32)], axis=1)
    sa_h = jnp.concatenate([-sin, jnp.zeros((seq, half + rest), F32)], axis=1)
    sb_h = jnp.concatenate([jnp.zeros((seq, half), F32), sin, jnp.zeros((seq, rest), F32)], axis=1)
    tile = lambda t: jnp.tile(t, (1, HEADS_PER_GROUP))
    return tile(cos_h), tile(sa_h), tile(sb_h)


def _router_weights(w_rg, b_rg, w_re, b_re):
    w = jnp.concatenate([w_rg, w_re], axis=1)
    w = jnp.pad(w, ((0, 0), (0, LANES - w.shape[1])))
    w_hi = w.astype(BF16)
    w_lo = (w - w_hi.astype(F32)).astype(BF16)
    b = jnp.pad(jnp.concatenate([b_rg, b_re]), (0, LANES - N_EXPERT_GROUPS - N_EXPERTS))
    return jnp.concatenate([w_hi, w_lo], axis=1), b.reshape(1, LANES)


def _attention_layer(x, p):
    batch, seq, d = x.shape
    x2d = x.reshape(batch * seq, d)
    cos, sa, sb = _rope_tables(seq)
    qkv = _qkv_call(x2d, seq, p["ln_mix0"], p["w_qkv"], p["mseg"], p["qg"], p["kg"], cos, sa, sb)
    os_, lses = [], []
    for g, dil in enumerate(DILATIONS):
        o, lse = _attn_call(qkv[g], qkv[3 + g], qkv[6 + g], batch, seq, dil)
        os_.append(o)
        lses.append(lse)
    return _attn_out_call(os_, lses, x2d, p["w_o"], p["expand"], p["ln_ffn0"], p["wr0"], p["br0"])


def _conv_layer(state, rows, seq, p):
    y, _, rg, _ = state
    return _conv_call(y, rg, rows, seq, p["ln_mix1"], p["w_pw1"], p["b_pw1"], p["w_dw"], p["b_dw"],
                      p["conv_ln_g"], p["conv_ln_b"], p["w_pw2"], p["b_pw2"],
                      p["ln_ffn1"], p["wr1"], p["br1"])


def _experts_of(state, layer, p):
    y, ri, _, hp = state
    return _moe_rows(y.shape[0], ri, hp, layer, p["experts"])


def _encoder(xs, p):
    order = sorted(range(len(xs)), key=lambda i: xs[i].shape[0] * xs[i].shape[1])
    state = {i: _attention_layer(xs[i], p) for i in order}
    rows = {i: _experts_of(state[i], 0, p) for i in order}
    state = {i: _conv_layer(state[i], rows[i], xs[i].shape[1], p) for i in order}
    rows = {i: _experts_of(state[i], 1, p) for i in order}
    outs = {i: _combine_call(state[i][0], state[i][2], rows[i]).reshape(xs[i].shape) for i in order}
    return tuple(outs[i] for i in range(len(xs)))


def kernel(x_prompt, x_sample, ln_mix, ln_ffn, w_qkv, q_gain, k_gain, w_o, w_pw1, b_pw1, w_dw, b_dw,
           conv_ln_g, conv_ln_b, w_pw2, b_pw2, w_router_group, b_router_group, w_router_expert,
           b_router_expert, w_gate, w_up, w_down):
    row = lambda v: v.reshape(1, -1).astype(F32)
    n_heads = N_GROUPS * HEADS_PER_GROUP
    head_id = np.arange(ATTN_WIDTH) // HEAD_DIM
    mseg = jnp.asarray((head_id[:, None] == head_id[None, :]) / HEAD_DIM, BF16)
    slot_id = np.arange(GROUP_WIDTH) // HEAD_DIM
    expand = jnp.asarray(np.arange(LANES)[:, None] == slot_id[None, :], BF16)
    wr0, br0 = _router_weights(w_router_group[0], b_router_group[0], w_router_expert[0], b_router_expert[0])
    wr1, br1 = _router_weights(w_router_group[1], b_router_group[1], w_router_expert[1], b_router_expert[1])
    p = dict(
        ln_mix0=row(ln_mix[0]), ln_mix1=row(ln_mix[1]), ln_ffn0=row(ln_ffn[0]), ln_ffn1=row(ln_ffn[1]),
        w_qkv=w_qkv[0].astype(BF16), mseg=mseg, expand=expand,
        qg=row(jnp.tile(q_gain[0], n_heads) * (HEAD_DIM ** -0.5 * LOG2E)), kg=row(jnp.tile(k_gain[0], n_heads)),
        w_o=w_o[0].astype(BF16),
        w_pw1=w_pw1[0].astype(BF16), b_pw1=row(b_pw1[0]), w_dw=w_dw[0].astype(F32), b_dw=row(b_dw[0]),
        conv_ln_g=row(conv_ln_g[0]), conv_ln_b=row(conv_ln_b[0]),
        w_pw2=w_pw2[0].astype(BF16), b_pw2=row(b_pw2[0]),
        wr0=wr0, br0=br0, wr1=wr1, br1=br1,
        experts=(w_gate, w_up, w_down),
    )
    return _encoder((x_prompt, x_sample), p)
```

```python
import functools

import jax
import jax.numpy as jnp
import numpy as np
from jax import lax
from jax.experimental import pallas as pl
from jax.experimental.pallas import tpu as pltpu
from jax.experimental.pallas import tpu_sc as plsc

D_MODEL = 1024
HEAD_DIM = 64
HEADS_PER_GROUP = 4
DILATIONS = (1, 4, 16)
HALF_WINDOW = 64
N_GROUPS = len(DILATIONS)
GROUP_WIDTH = HEADS_PER_GROUP * HEAD_DIM
ATTN_WIDTH = N_GROUPS * GROUP_WIDTH
ROPE_DIM = HEAD_DIM // 4
ROPE_THETA = 500000.0
CONV_WIDTH = 31
CONV_HALO = 16
N_EXPERT_GROUPS = 4
EXPERTS_PER_GROUP = 8
N_EXPERTS = N_EXPERT_GROUPS * EXPERTS_PER_GROUP
TOP_K = 2
D_EXPERT = 512
EPS = 1e-6
NEG_INF = -1e30
LOG2E = 1.4426950408889634
LN2 = 0.6931471805599453

LANES = 128
SUBLANES = 8
MXU_WIDTH = 256
ROW_TILE = 512
ATTN_Q_TILE = 128
ATTN_STEP_ROWS = 512
MOE_BLOCK = 512
VMEM_LIMIT = 56 * 1024 * 1024

PACKED_WIDTH = D_MODEL // 2
SC_WINDOW = 128
SC_ROW_SPLIT = 2
SC_ROW_WORDS = PACKED_WIDTH // SC_ROW_SPLIT

F32 = jnp.float32
BF16 = jnp.bfloat16


def _cparams(sem):
    return pltpu.CompilerParams(dimension_semantics=sem, vmem_limit_bytes=VMEM_LIMIT)


def _rms(x, gain):
    return x * lax.rsqrt(jnp.mean(x * x, axis=-1, keepdims=True) + EPS) * gain


def _pack_rows(x):
    bits = pltpu.bitcast(x.astype(BF16).astype(F32), jnp.uint32)
    half = x.shape[1] // 2
    return bits[:, :half] | (bits[:, half:] >> 16)


def _unpack_rows(w):
    hi = pltpu.bitcast(w & jnp.uint32(0xFFFF0000), F32)
    lo = pltpu.bitcast(w << 16, F32)
    return hi, lo


def _store_by_residue(out_ref, val, dil, scr, row0=0):
    rows, width = val.shape
    out_rows = slice(row0 // dil, (row0 + rows) // dil)
    if dil == 1:
        out_ref[out_rows, :] = val.astype(out_ref.dtype)
        return
    for c in range(width // LANES):
        scr[c] = val[:, c * LANES:(c + 1) * LANES]
    for r in range(dil):
        for c in range(width // LANES):
            lanes = slice(r * width + c * LANES, r * width + (c + 1) * LANES)
            out_ref[out_rows, lanes] = scr[c, pl.ds(r, rows // dil, stride=dil), :].astype(out_ref.dtype)


def _qkv_kernel(x_ref, g_ref, w_ref, mseg_ref, qg_ref, kg_ref, cos_ref, sa_ref, sb_ref,
                q0, q1, q2, k0, k1, k2, v0, v1, v2, *scrs):
    scrs = iter(scrs)

    def store(out_ref, val, g, row0):
        _store_by_residue(out_ref, val, DILATIONS[g], next(scrs) if DILATIONS[g] > 1 else None, row0)

    for rows in _row_halves(x_ref.shape[0]):
        h = _rms(x_ref[rows, :], g_ref[...]).astype(BF16)
        q = jnp.dot(h, w_ref[:, 0:ATTN_WIDTH], preferred_element_type=F32)
        k = jnp.dot(h, w_ref[:, ATTN_WIDTH:2 * ATTN_WIDTH], preferred_element_type=F32)
        v = jnp.dot(h, w_ref[:, 2 * ATTN_WIDTH:3 * ATTN_WIDTH], preferred_element_type=F32)
        ms_q = jnp.dot((q * q).astype(BF16), mseg_ref[...], preferred_element_type=F32)
        ms_k = jnp.dot((k * k).astype(BF16), mseg_ref[...], preferred_element_type=F32)
        for g, o in enumerate((v0, v1, v2)):
            store(o, v[:, g * GROUP_WIDTH:(g + 1) * GROUP_WIDTH], g, rows.start)
        cos = cos_ref[rows, :]
        sa = sa_ref[rows, :]
        sb = sb_ref[rows, :]
        for t, ms, gain, outs in ((q, ms_q, qg_ref[...], (q0, q1, q2)), (k, ms_k, kg_ref[...], (k0, k1, k2))):
            tn = t * lax.rsqrt(ms + EPS) * gain
            for g in range(N_GROUPS):
                c = tn[:, g * GROUP_WIDTH:(g + 1) * GROUP_WIDTH]
                r = (c * cos + pltpu.roll(c, GROUP_WIDTH - ROPE_DIM // 2, 1) * sa
                     + pltpu.roll(c, ROPE_DIM // 2, 1) * sb)
                store(outs[g], r, g, rows.start)


def _qkv_call(x2d, seq, gain, w_qkv, mseg, qg, kg, cos, sa, sb):
    n = x2d.shape[0]
    tm = ROW_TILE
    tiles_per_seq = seq // tm
    row = lambda i: (i, 0)
    const = lambda i: (0, 0)
    tab = lambda i: (i % tiles_per_seq, 0)
    out_specs = [pl.BlockSpec((tm // dil, dil * GROUP_WIDTH), row) for dil in DILATIONS] * 3
    out_shape = [jax.ShapeDtypeStruct((n // dil, dil * GROUP_WIDTH), BF16) for dil in DILATIONS] * 3
    return pl.pallas_call(
        _qkv_kernel,
        grid=(n // tm,),
        in_specs=[
            pl.BlockSpec((tm, D_MODEL), row),
            pl.BlockSpec((1, D_MODEL), const),
            pl.BlockSpec((D_MODEL, 3 * ATTN_WIDTH), const),
            pl.BlockSpec((ATTN_WIDTH, ATTN_WIDTH), const),
            pl.BlockSpec((1, ATTN_WIDTH), const),
            pl.BlockSpec((1, ATTN_WIDTH), const),
            pl.BlockSpec((tm, GROUP_WIDTH), tab),
            pl.BlockSpec((tm, GROUP_WIDTH), tab),
            pl.BlockSpec((tm, GROUP_WIDTH), tab),
        ],
        out_specs=out_specs,
        out_shape=out_shape,
        scratch_shapes=[pltpu.VMEM((GROUP_WIDTH // LANES, tm // 2, LANES), F32)]
        * (2 * 3 * sum(dil > 1 for dil in DILATIONS)),
        compiler_params=_cparams(("parallel",)),
        name="qkv_proj",
    )(x2d, gain, w_qkv, mseg, qg, kg, cos, sa, sb)


def _attn_kernel(q_ref, k_ref, v_ref, o_ref, lse_ref, *, ls, rows, res, win):
    i = pl.program_id(2)
    tq = ATTN_Q_TILE
    rel = (lax.broadcasted_iota(jnp.int32, (tq, win), 0) - lax.broadcasted_iota(jnp.int32, (tq, win), 1))
    lane = lax.broadcasted_iota(jnp.int32, (tq, LANES), 1)
    lane_kv = lax.broadcasted_iota(jnp.int32, (win, LANES), 1)

    blocks = []
    for r in range(res):
        for sb in range(rows // tq):
            q_start = i * rows + sb * tq
            k_start = pl.multiple_of(jnp.clip(q_start - HALF_WINDOW, 0, ls - win), HALF_WINDOW)
            mask = jnp.abs(rel + (q_start - k_start)) <= HALF_WINDOW
            blocks.append((r, slice(sb * tq, (sb + 1) * tq), k_start, mask))
    chains = [(blk, h) for blk in blocks for h in range(HEADS_PER_GROUP)]

    def cols(r, h):
        return slice(r * GROUP_WIDTH + h * HEAD_DIM, r * GROUP_WIDTH + (h + 1) * HEAD_DIM)

    scores = [lax.dot_general(q_ref[0, q_rows, cols(r, h)], k_ref[0, pl.ds(k_start, win), cols(r, h)],
                              (((1,), (1,)), ((), ())), preferred_element_type=F32)
              for (r, q_rows, k_start, _), h in chains]
    maxes, probs = [], []
    for ((_, _, _, mask), _), s in zip(chains, scores):
        s = jnp.where(mask, s, NEG_INF)
        m = jnp.max(s, axis=-1, keepdims=True)
        maxes.append(m)
        probs.append(jnp.exp2(s - m).astype(BF16))
    outs = []
    for ((r, _, k_start, _), h), p in zip(chains, probs):
        pair = slice(r * GROUP_WIDTH + (h // 2) * LANES, r * GROUP_WIDTH + (h // 2 + 1) * LANES)
        own_half = (lane_kv < HEAD_DIM) if h % 2 == 0 else (lane_kv >= HEAD_DIM)
        v_ext = jnp.where(own_half, v_ref[0, pl.ds(k_start, win), pair], jnp.ones((), BF16))
        outs.append(jnp.dot(p, v_ext, preferred_element_type=F32))
    lse_tiles = {}
    for ((r, q_rows, _, _), h), o_ext, m in zip(chains, outs, maxes):
        den_ext = pltpu.roll(o_ext, HEAD_DIM, 1)
        own = slice((h % 2) * HEAD_DIM, (h % 2 + 1) * HEAD_DIM)
        o_ref[0, q_rows, cols(r, h)] = (o_ext / den_ext)[:, own].astype(BF16)
        den_at_h = den_ext if h % 2 == 0 else o_ext
        key = (r, q_rows.start)
        lse_tiles[key] = jnp.where(lane == h, (m + jnp.log2(den_at_h)) * LN2,
                                   lse_tiles.get(key, jnp.zeros((tq, LANES), F32)))
    for (r, q0), tile in lse_tiles.items():
        lse_ref[0, q0:q0 + tq, r * LANES:(r + 1) * LANES] = tile


def _attn_call(q, k, v, batch, seq, dil):
    ls = seq // dil
    rows = min(ATTN_STEP_ROWS, ls)
    res = max(1, min(dil, ATTN_STEP_ROWS // rows))
    win = min(ATTN_Q_TILE + 2 * HALF_WINDOW, ls)
    view = lambda t: t.reshape(batch, ls, dil * GROUP_WIDTH)
    qmap = lambda b, r, i: (b, i, r)
    kvmap = lambda b, r, i: (b, 0, r)
    o, lse = pl.pallas_call(
        functools.partial(_attn_kernel, ls=ls, rows=rows, res=res, win=win),
        grid=(batch, dil // res, ls // rows),
        in_specs=[
            pl.BlockSpec((1, rows, res * GROUP_WIDTH), qmap),
            pl.BlockSpec((1, ls, res * GROUP_WIDTH), kvmap),
            pl.BlockSpec((1, ls, res * GROUP_WIDTH), kvmap),
        ],
        out_specs=[
            pl.BlockSpec((1, rows, res * GROUP_WIDTH), qmap),
            pl.BlockSpec((1, rows, res * LANES), qmap),
        ],
        out_shape=[
            jax.ShapeDtypeStruct((batch, ls, dil * GROUP_WIDTH), BF16),
            jax.ShapeDtypeStruct((batch, ls, dil * LANES), F32),
        ],
        compiler_params=_cparams(("parallel", "parallel", "arbitrary")),
        name=f"banded_attn_d{dil}",
    )(view(q), view(k), view(v))
    return o.reshape(batch * ls, dil * GROUP_WIDTH), lse.reshape(batch * ls, dil * LANES)


def _route_logits(x, gain, wr_ref, br_ref, hp_ref, rows):
    h = _rms(x, gain)
    packed = _pack_rows(h)
    for c in range(SC_ROW_SPLIT):
        hp_ref[c, rows, :] = packed[:, c * SC_ROW_WORDS:(c + 1) * SC_ROW_WORDS]
    h_hi = h.astype(BF16)
    h_lo = (h - h_hi.astype(F32)).astype(BF16)
    acc = (jnp.dot(h_hi, wr_ref[...], preferred_element_type=F32)
           + jnp.dot(h_lo, wr_ref[...], preferred_element_type=F32))
    return acc[:, :LANES] + acc[:, LANES:] + br_ref[...]


def _route_decide(logits, ri_ref, rg_ref, rows):
    lane = lax.broadcasted_iota(jnp.int32, logits.shape, 1)
    lane_f = lane.astype(F32)
    big = jnp.float32(2 * LANES)

    is_g = lane < N_EXPERT_GROUPS
    gl = jnp.where(is_g, logits, NEG_INF)
    gmax = jnp.max(gl, axis=-1, keepdims=True)
    grp = jnp.min(jnp.where(gl == gmax, lane_f, big), axis=-1, keepdims=True)
    p_grp = 1.0 / jnp.sum(jnp.where(is_g, jnp.exp(gl - gmax), 0.0), axis=-1, keepdims=True)

    lo = N_EXPERT_GROUPS + grp * EXPERTS_PER_GROUP
    in_grp = (lane_f >= lo) & (lane_f < lo + EXPERTS_PER_GROUP)
    el = jnp.where(in_grp, logits, NEG_INF)
    v1 = jnp.max(el, axis=-1, keepdims=True)
    i1 = jnp.min(jnp.where(el == v1, lane_f, big), axis=-1, keepdims=True)
    el2 = jnp.where(lane_f == i1, NEG_INF, el)
    v2 = jnp.max(el2, axis=-1, keepdims=True)
    i2 = jnp.min(jnp.where(el2 == v2, lane_f, big), axis=-1, keepdims=True)
    e21 = jnp.exp(v2 - v1)
    g1 = p_grp / (1.0 + e21)
    g2 = g1 * e21
    ri = jnp.where(lane == 0, i1 - N_EXPERT_GROUPS, jnp.where(lane == 1, i2 - N_EXPERT_GROUPS, 0.0))
    ri_ref[rows, :] = ri.astype(jnp.int32)
    rg_ref[rows, :] = jnp.where(lane == 0, g1, jnp.where(lane == 1, g2, 0.0))


def _row_halves(tm):
    return [slice(0, tm // 2), slice(tm // 2, tm)]


def _route_out_specs(n, tm):
    row = lambda i: (i, 0)
    specs = [
        pl.BlockSpec((tm, D_MODEL), row),
        pl.BlockSpec((tm, LANES), row),
        pl.BlockSpec((tm, LANES), row),
        pl.BlockSpec((SC_ROW_SPLIT, tm, SC_ROW_WORDS), lambda i: (0, i, 0)),
    ]
    shapes = [
        jax.ShapeDtypeStruct((n, D_MODEL), F32),
        jax.ShapeDtypeStruct((n, LANES), jnp.int32),
        jax.ShapeDtypeStruct((n, LANES), F32),
        jax.ShapeDtypeStruct((SC_ROW_SPLIT, n, SC_ROW_WORDS), jnp.uint32),
    ]
    return specs, shapes


def _load_by_residue(in_ref, dil, scr):
    if dil == 1:
        return in_ref[...].astype(F32)
    chunks, rows, _ = scr.shape
    width = chunks * LANES
    for r in range(dil):
        for c in range(chunks):
            lanes = slice(r * width + c * LANES, r * width + (c + 1) * LANES)
            scr[c, pl.ds(r, rows // dil, stride=dil), :] = in_ref[:, lanes].astype(F32)
    return jnp.concatenate([scr[c] for c in range(chunks)], axis=1) if chunks > 1 else scr[0]


def _attn_out_kernel(o0, o1, o2, l0, l1, l2, x_ref, wo_ref, ex_ref, g_ref, wr_ref, br_ref,
                     y_ref, ri_ref, rg_ref, hp_ref, o_scr1, o_scr2, l_scr1, l_scr2):
    ls = [_load_by_residue(l, dil, scr)
          for l, dil, scr in zip((l0, l1, l2), DILATIONS, (None, l_scr1, l_scr2))]
    m = jnp.maximum(jnp.maximum(ls[0], ls[1]), ls[2])
    es = [jnp.exp(l - m) for l in ls]
    inv = 1.0 / (es[0] + es[1] + es[2])
    a_wide = []
    for g in range(N_GROUPS):
        alpha = es[g] * inv
        a_hi = alpha.astype(BF16)
        a_lo = (alpha - a_hi.astype(F32)).astype(BF16)
        a_wide.append(jnp.dot(a_hi, ex_ref[...], preferred_element_type=F32)
                      + jnp.dot(a_lo, ex_ref[...], preferred_element_type=F32))
    scaled = jnp.concatenate(
        [(_load_by_residue(o_ref, dil, scr) * a).astype(BF16)
         for o_ref, dil, scr, a in zip((o0, o1, o2), DILATIONS, (None, o_scr1, o_scr2), a_wide)], axis=1)
    halves = _row_halves(x_ref.shape[0])
    ys = [x_ref[rows, :] + jnp.dot(scaled[rows, :], wo_ref[...], preferred_element_type=F32)
          for rows in halves]
    for rows, y in zip(halves, ys):
        y_ref[rows, :] = y
    logits = [_route_logits(y, g_ref[...], wr_ref, br_ref, hp_ref, rows) for rows, y in zip(halves, ys)]
    for rows, lg in zip(halves, logits):
        _route_decide(lg, ri_ref, rg_ref, rows)


def _attn_out_call(os_, lses, x2d, w_o, expand, gain, wr, br):
    n = x2d.shape[0]
    tm = ROW_TILE
    row = lambda i: (i, 0)
    const = lambda i: (0, 0)
    out_specs, out_shape = _route_out_specs(n, tm)
    return pl.pallas_call(
        _attn_out_kernel,
        grid=(n // tm,),
        in_specs=(
            [pl.BlockSpec((tm // dil, dil * GROUP_WIDTH), row) for dil in DILATIONS]
            + [pl.BlockSpec((tm // dil, dil * LANES), row) for dil in DILATIONS]
            + [
                pl.BlockSpec((tm, D_MODEL), row),
                pl.BlockSpec((ATTN_WIDTH, D_MODEL), const),
                pl.BlockSpec((LANES, GROUP_WIDTH), const),
                pl.BlockSpec((1, D_MODEL), const),
                pl.BlockSpec((D_MODEL, 2 * LANES), const),
                pl.BlockSpec((1, LANES), const),
            ]
        ),
        out_specs=out_specs,
        out_shape=out_shape,
        scratch_shapes=([pltpu.VMEM((GROUP_WIDTH // LANES, tm, LANES), F32)] * 2
                        + [pltpu.VMEM((1, tm, LANES), F32)] * 2),
        compiler_params=_cparams(("parallel",)),
        name="attn_out_router",
    )(*os_, *lses, x2d, w_o, expand, gain, wr, br)


N_MOE_REFS = 2 + SC_ROW_SPLIT * TOP_K


def _conv_kernel(*refs, seq, tm):
    cur, prev, nxt = (refs[b * N_MOE_REFS:(b + 1) * N_MOE_REFS] for b in range(3))
    (gm_ref, w1_ref, b1_ref, wdw_ref, bdw_ref, lng_ref, lnb_ref, w2_ref, b2_ref, gf_ref, wr_ref, br_ref,
     y_ref, ri_ref, rg_ref, hp_ref, x_scr, u_scr, c_scr, us_scr, cs_scr) = refs[3 * N_MOE_REFS:]
    i = pl.program_id(0)
    pos0 = (i * tm) % seq
    hl = CONV_HALO
    x_scr[0:hl, :] = _combined_rows(prev[0], prev[1], prev[2:])
    x_scr[hl:hl + tm, :] = _combined_rows(cur[0], cur[1], cur[2:])
    x_scr[hl + tm:hl + tm + hl, :] = _combined_rows(nxt[0], nxt[1], nxt[2:])
    halves = _row_halves(tm)
    first = hl - CONV_WIDTH // 2
    r_all = lax.broadcasted_iota(jnp.int32, (tm + 2 * hl, 1), 0)
    outside = ((r_all < hl) & (pos0 == 0)) | ((r_all >= hl + tm) & (pos0 + tm == seq))
    u_parts = [slice(0, tm // 2 + 2 * hl), slice(tm // 2 + 2 * hl, tm + 2 * hl)]
    gated = []
    for part in u_parts:
        h = _rms(x_scr[part, :], gm_ref[...]).astype(BF16)
        a = jnp.dot(h, w1_ref[...], preferred_element_type=F32) + b1_ref[...]
        gated.append(a)
    for part, a in zip(u_parts, gated):
        u = a[:, :D_MODEL] * jax.nn.sigmoid(a[:, D_MODEL:])
        u = jnp.where(outside[part, :], 0.0, u)
        for c in range(D_MODEL // LANES):
            u_scr[c, part, :] = u[:, c * LANES:(c + 1) * LANES]

    seg = (tm // 2) // SUBLANES
    seg_tiles = seg + 2 * hl
    group = SUBLANES
    logits = []
    for rows in halves:
        for c in range(D_MODEL // LANES):
            cols = slice(c * LANES, (c + 1) * LANES)
            for sgm in range(SUBLANES):
                r0 = rows.start + seg * sgm
                us_scr[c, pl.ds(sgm, seg_tiles, stride=SUBLANES), :] = u_scr[c, r0:r0 + seg_tiles, :]
            for k0 in range(0, seg, group):
                accs = [jnp.zeros((SUBLANES, LANES), F32) + bdw_ref[:, cols] for _ in range(group)]
                for t in range(CONV_WIDTH):
                    w_t = wdw_ref[t:t + 1, cols]
                    for j in range(group):
                        tile = (k0 + j + t + first) * SUBLANES
                        accs[j] = accs[j] + us_scr[c, tile:tile + SUBLANES, :] * w_t
                for j in range(group):
                    cs_scr[c, (k0 + j) * SUBLANES:(k0 + j + 1) * SUBLANES, :] = accs[j]
            for sgm in range(SUBLANES):
                r0 = rows.start + seg * sgm
                c_scr[c, r0:r0 + seg, :] = cs_scr[c, pl.ds(sgm, seg, stride=SUBLANES), :]
        cv = jnp.concatenate([c_scr[c, rows, :] for c in range(D_MODEL // LANES)], axis=1)
        mu = jnp.mean(cv, axis=-1, keepdims=True)
        xc = cv - mu
        var = jnp.mean(xc * xc, axis=-1, keepdims=True)
        ln = xc * lax.rsqrt(var + EPS) * lng_ref[...] + lnb_ref[...]
        act = (ln * jax.nn.sigmoid(ln)).astype(BF16)
        y = (x_scr[hl + rows.start:hl + rows.stop, :]
             + jnp.dot(act, w2_ref[...], preferred_element_type=F32) + b2_ref[...])
        y_ref[rows, :] = y
        logits.append(_route_logits(y, gf_ref[...], wr_ref, br_ref, hp_ref, rows))
    for rows, lg in zip(halves, logits):
        _route_decide(lg, ri_ref, rg_ref, rows)


def _conv_call(y2d, rg, rows, seq, gm, w1, b1, wdw, bdw, lng, lnb, w2, b2, gf, wr, br):
    n = y2d.shape[0]
    tm = ROW_TILE
    hl = CONV_HALO
    per = tm // hl
    tiles = n // tm
    const = lambda i: (0, 0)
    cur_blk = lambda i: i
    prev_blk = lambda i: jnp.maximum(i * per - 1, 0)
    next_blk = lambda i: jnp.minimum((i + 1) * per, n // hl - 1)

    def moe_specs(rows_blk, blk_of, blocks_per_piece):
        at = lambda off: (lambda i: (blk_of(i) + off, 0))
        return ([pl.BlockSpec((rows_blk, D_MODEL), at(0)), pl.BlockSpec((rows_blk, LANES), at(0))]
                + [pl.BlockSpec((rows_blk, SC_ROW_WORDS), at(j * blocks_per_piece))
                   for j in range(SC_ROW_SPLIT * TOP_K)])

    moe_args = [y2d, rg] + [rows] * (SC_ROW_SPLIT * TOP_K)
    out_specs, out_shape = _route_out_specs(n, tm)
    return pl.pallas_call(
        functools.partial(_conv_kernel, seq=seq, tm=tm),
        grid=(tiles,),
        in_specs=(
            moe_specs(tm, cur_blk, tiles) + moe_specs(hl, prev_blk, tiles * per)
            + moe_specs(hl, next_blk, tiles * per)
            + [
                pl.BlockSpec((1, D_MODEL), const),
                pl.BlockSpec((D_MODEL, 2 * D_MODEL), const),
                pl.BlockSpec((1, 2 * D_MODEL), const),
                pl.BlockSpec((CONV_WIDTH, D_MODEL), const),
                pl.BlockSpec((1, D_MODEL), const),
                pl.BlockSpec((1, D_MODEL), const),
                pl.BlockSpec((1, D_MODEL), const),
                pl.BlockSpec((D_MODEL, D_MODEL), const),
                pl.BlockSpec((1, D_MODEL), const),
                pl.BlockSpec((1, D_MODEL), const),
                pl.BlockSpec((D_MODEL, 2 * LANES), const),
                pl.BlockSpec((1, LANES), const),
            ]
        ),
        out_specs=out_specs,
        out_shape=out_shape,
        scratch_shapes=[
            pltpu.VMEM((tm + 2 * hl, D_MODEL), F32),
            pltpu.VMEM((D_MODEL // LANES, tm + 2 * hl, LANES), F32),
            pltpu.VMEM((D_MODEL // LANES, tm, LANES), F32),
            pltpu.VMEM((D_MODEL // LANES, (tm // 2 // SUBLANES + 2 * hl) * SUBLANES, LANES), F32),
            pltpu.VMEM((D_MODEL // LANES, tm // 2, LANES), F32),
        ],
        compiler_params=_cparams(("parallel",)),
        name="conv_module_router",
    )(*moe_args, *moe_args, *moe_args, gm, w1, b1, wdw, bdw, lng, lnb, w2, b2, gf, wr, br)


def _sc_mesh():
    return plsc.VectorSubcoreMesh(core_axis_name="core", subcore_axis_name="subcore")


def _sc_scatter_rows(src, idx, n_src_windows, out_rows):
    m = idx.shape[1]
    width = src.shape[1]
    nsw = n_src_windows

    @functools.partial(pl.kernel, out_type=jax.ShapeDtypeStruct((out_rows, width), src.dtype),
                       mesh=_sc_mesh(), name="sc_scatter_rows")
    def scatter(src_hbm, idx_hbm, out_hbm):
        def body(src_vmem, idx_vmem):
            pltpu.sync_copy(src_vmem, out_hbm.at[idx_vmem.at[0]])

        pltpu.emit_pipeline(
            body,
            grid=(m // SC_WINDOW,),
            in_specs=[
                pl.BlockSpec((SC_WINDOW, width), lambda w: ((w // (TOP_K * nsw)) * nsw + w % nsw, 0)),
                pl.BlockSpec((1, SC_WINDOW), lambda w: (0, w)),
            ],
            out_specs=[],
            core_axis_name=("core", "subcore"),
            dimension_semantics=(pltpu.PARALLEL,),
        )(src_hbm, idx_hbm)

    return scatter(src, idx)


def _sc_gather_rows(table, idx):
    m = idx.shape[1]
    width = table.shape[1]

    @functools.partial(pl.kernel, out_type=jax.ShapeDtypeStruct((m, width), table.dtype),
                       mesh=_sc_mesh(), name="sc_gather_rows")
    def gather(table_hbm, idx_hbm, out_hbm):
        def body(idx_vmem, out_vmem):
            pltpu.sync_copy(table_hbm.at[idx_vmem.at[0]], out_vmem)

        pltpu.emit_pipeline(
            body,
            grid=(m // SC_WINDOW,),
            in_specs=[pl.BlockSpec((1, SC_WINDOW), lambda w: (0, w))],
            out_specs=[pl.BlockSpec((SC_WINDOW, width), lambda w: (w, 0))],
            core_axis_name=("core", "subcore"),
            dimension_semantics=(pltpu.PARALLEL,),
        )(idx_hbm, out_hbm)

    return gather(table, idx)


def _expert_kernel(be_ref, nv_ref, xs_ref, wg_ref, wu_ref, wd_ref, yb_ref, wg_s, wu_s, wd_s):
    i = pl.program_id(0)

    @pl.when((i == 0) | (be_ref[i] != be_ref[jnp.maximum(i - 1, 0)]))
    def _():
        wg_s[...] = wg_ref[0, 0].astype(BF16)
        wu_s[...] = wu_ref[0, 0].astype(BF16)
        wd_s[...] = wd_ref[0, 0].astype(BF16)

    @pl.when(nv_ref[i] > 0)
    def _():
        words = jnp.concatenate([xs_ref[c] for c in range(SC_ROW_SPLIT)], axis=1)
        row = lax.broadcasted_iota(jnp.int32, (words.shape[0], 1), 0)
        words = jnp.where(row < nv_ref[i], words, jnp.uint32(0))
        hi, lo = _unpack_rows(words)
        xb = jnp.concatenate([hi.astype(BF16), lo.astype(BF16)], axis=1)
        chunks = [slice(c, c + MXU_WIDTH) for c in range(0, D_EXPERT, MXU_WIDTH)]
        gu = [(jnp.dot(xb, wg_s[:, c], preferred_element_type=F32),
               jnp.dot(xb, wu_s[:, c], preferred_element_type=F32)) for c in chunks]
        y = None
        for c, (g, u) in zip(chunks, gu):
            hid = (g * jax.nn.sigmoid(g) * u).astype(BF16)
            part = jnp.dot(hid, wd_s[c, :], preferred_element_type=F32)
            y = part if y is None else y + part
        packed = _pack_rows(y)
        for c in range(SC_ROW_SPLIT):
            yb_ref[c] = packed[:, c * SC_ROW_WORDS:(c + 1) * SC_ROW_WORDS]


def _expert_call(block_expert, block_valid, xs, layer, w_gate, w_up, w_down):
    p_total = xs.shape[1]
    blk = MOE_BLOCK
    wmap = lambda i, be, nv: (layer, be[i], 0, 0)
    rows = lambda i, be, nv: (0, i, 0)
    return pl.pallas_call(
        _expert_kernel,
        grid_spec=pltpu.PrefetchScalarGridSpec(
            num_scalar_prefetch=2,
            grid=(p_total // blk,),
            in_specs=[
                pl.BlockSpec((SC_ROW_SPLIT, blk, SC_ROW_WORDS), rows),
                pl.BlockSpec((1, 1, D_MODEL, D_EXPERT), wmap),
                pl.BlockSpec((1, 1, D_MODEL, D_EXPERT), wmap),
                pl.BlockSpec((1, 1, D_EXPERT, D_MODEL), wmap),
            ],
            out_specs=pl.BlockSpec((SC_ROW_SPLIT, blk, SC_ROW_WORDS), rows),
            scratch_shapes=[
                pltpu.VMEM((D_MODEL, D_EXPERT), BF16),
                pltpu.VMEM((D_MODEL, D_EXPERT), BF16),
                pltpu.VMEM((D_EXPERT, D_MODEL), BF16),
            ],
        ),
        out_shape=jax.ShapeDtypeStruct((SC_ROW_SPLIT, p_total, SC_ROW_WORDS), jnp.uint32),
        compiler_params=_cparams(("arbitrary",)),
        name="moe_experts",
    )(block_expert, block_valid, xs, w_gate, w_up, w_down)


def _combined_rows(y_ref, rg_ref, piece_refs):
    rg = rg_ref[...]
    his, los = [], []
    for c in range(SC_ROW_SPLIT):
        hi0, lo0 = _unpack_rows(piece_refs[c * TOP_K][...])
        hi1, lo1 = _unpack_rows(piece_refs[c * TOP_K + 1][...])
        his.append(rg[:, 0:1] * hi0 + rg[:, 1:2] * hi1)
        los.append(rg[:, 0:1] * lo0 + rg[:, 1:2] * lo1)
    return y_ref[...] + jnp.concatenate(his + los, axis=1)


def _combine_kernel(y_ref, rg_ref, *refs):
    refs[-1][...] = _combined_rows(y_ref, rg_ref, refs[:-1])


def _combine_call(y2d, rg, rows):
    n = y2d.shape[0]
    tm = ROW_TILE
    tiles = n // tm
    return pl.pallas_call(
        _combine_kernel,
        grid=(tiles,),
        in_specs=[
            pl.BlockSpec((tm, D_MODEL), lambda i: (i, 0)),
            pl.BlockSpec((tm, LANES), lambda i: (i, 0)),
        ] + [
            pl.BlockSpec((tm, SC_ROW_WORDS), functools.partial(lambda i, j: (i + j * tiles, 0), j=j))
            for j in range(SC_ROW_SPLIT * TOP_K)
        ],
        out_specs=pl.BlockSpec((tm, D_MODEL), lambda i: (i, 0)),
        out_shape=jax.ShapeDtypeStruct((n, D_MODEL), F32),
        compiler_params=_cparams(("parallel",)),
        name="moe_combine",
    )(y2d, rg, *([rows] * (SC_ROW_SPLIT * TOP_K)))


def _slot_tables(ri, n):
    e = ri[:, :TOP_K]
    onehot = (e[:, :, None] == jnp.arange(N_EXPERTS, dtype=jnp.int32)).astype(jnp.int32)
    tok_cnt = onehot.sum(axis=1)
    t = 256
    cnt_t = tok_cnt.reshape(n // t, t, N_EXPERTS)
    tri = (jnp.arange(t)[:, None] > jnp.arange(t)[None, :]).astype(F32)
    within = jnp.einsum("ts,nsc->ntc", tri, cnt_t.astype(F32)).astype(jnp.int32)
    tile_sum = cnt_t.sum(axis=1)
    tile_base = jnp.cumsum(tile_sum, axis=0) - tile_sum
    rank = (within + tile_base[:, None, :]).reshape(n, N_EXPERTS)
    counts = tile_sum.sum(axis=0)
    padded = ((counts + MOE_BLOCK - 1) // MOE_BLOCK) * MOE_BLOCK
    pad_end = jnp.cumsum(padded)
    pad_start = pad_end - padded
    slot = ((rank + pad_start)[:, None, :] * onehot).sum(axis=-1)
    n_blocks = (n * TOP_K) // MOE_BLOCK + N_EXPERTS
    block_start = jnp.arange(n_blocks, dtype=jnp.int32) * MOE_BLOCK
    block_expert = jnp.minimum((pad_end[None, :] <= block_start[:, None]).sum(axis=-1),
                               N_EXPERTS - 1).astype(jnp.int32)
    used_end = (pad_start + counts)[block_expert]
    block_valid = jnp.clip(used_end - block_start, 0, MOE_BLOCK).astype(jnp.int32)
    p_total = n_blocks * MOE_BLOCK
    slot_km = slot.T.reshape(1, TOP_K * n).astype(jnp.int32)
    idx = jnp.concatenate([slot_km + c * p_total for c in range(SC_ROW_SPLIT)], axis=1)
    return idx, block_expert, block_valid, p_total


def _after(value, other):
    if other is None:
        return value
    return lax.optimization_barrier((value, other))[0]


def _moe_rows(n, ri, hp, layer, experts, after=None):
    idx, block_expert, block_valid, p_total = _slot_tables(ri, n)
    xs = _sc_scatter_rows(hp.reshape(SC_ROW_SPLIT * n, SC_ROW_WORDS), idx, n // SC_WINDOW,
                          SC_ROW_SPLIT * p_total)
    yb = _expert_call(block_expert, _after(block_valid, after),
                      xs.reshape(SC_ROW_SPLIT, p_total, SC_ROW_WORDS), layer, *experts)
    return _sc_gather_rows(yb.reshape(SC_ROW_SPLIT * p_total, SC_ROW_WORDS), idx), yb


def _rope_tables(seq):
    pos = jnp.arange(seq, dtype=F32)
    inv_freq = ROPE_THETA ** (-jnp.arange(0, ROPE_DIM, 2, dtype=F32) / ROPE_DIM)
    ang = pos[:, None] * inv_freq[None, :]
    cos = jnp.cos(ang)
    sin = jnp.sin(ang)
    half = ROPE_DIM // 2
    rest = HEAD_DIM - ROPE_DIM
    cos_h = jnp.concatenate([cos, cos, jnp.ones((seq, rest), F32)], axis=1)
    sa_h = jnp.concatenate([-sin, jnp.zeros((seq, half + rest), F32)], axis=1)
    sb_h = jnp.concatenate([jnp.zeros((seq, half), F32), sin, jnp.zeros((seq, rest), F32)], axis=1)
    tile = lambda t: jnp.tile(t, (1, HEADS_PER_GROUP))
    return tile(cos_h), tile(sa_h), tile(sb_h)


def _router_weights(w_rg, b_rg, w_re, b_re):
    w = jnp.concatenate([w_rg, w_re], axis=1)
    w = jnp.pad(w, ((0, 0), (0, LANES - w.shape[1])))
    w_hi = w.astype(BF16)
    w_lo = (w - w_hi.astype(F32)).astype(BF16)
    b = jnp.pad(jnp.concatenate([b_rg, b_re]), (0, LANES - N_EXPERT_GROUPS - N_EXPERTS))
    return jnp.concatenate([w_hi, w_lo], axis=1), b.reshape(1, LANES)


def _attention_layer(x, p):
    batch, seq, d = x.shape
    x2d = x.reshape(batch * seq, d)
    cos, sa, sb = _rope_tables(seq)
    qkv = _qkv_call(x2d, seq, p["ln_mix0"], p["w_qkv"], p["mseg"], p["qg"], p["kg"], cos, sa, sb)
    os_, lses = [], []
    for g, dil in enumerate(DILATIONS):
        o, lse = _attn_call(qkv[g], qkv[3 + g], qkv[6 + g], batch, seq, dil)
        os_.append(o)
        lses.append(lse)
    return _attn_out_call(os_, lses, x2d, p["w_o"], p["expand"], p["ln_ffn0"], p["wr0"], p["br0"])


def _conv_layer(state, rows, seq, p, after=None):
    y, _, rg, _ = state
    return _conv_call(y, _after(rg, after), rows, seq, p["ln_mix1"], p["w_pw1"], p["b_pw1"], p["w_dw"],
                      p["b_dw"], p["conv_ln_g"], p["conv_ln_b"], p["w_pw2"], p["b_pw2"],
                      p["ln_ffn1"], p["wr1"], p["br1"])


def _experts_of(state, layer, p, after=None):
    y, ri, _, hp = state
    return _moe_rows(y.shape[0], ri, hp, layer, p["experts"], after)


def _encoder(xs, p):
    small, big = sorted(range(len(xs)), key=lambda i: xs[i].shape[0] * xs[i].shape[1])
    seq = {i: xs[i].shape[1] for i in (small, big)}
    state = {i: _attention_layer(xs[i], p) for i in (big, small)}
    rows, yb = {}, {}
    rows[big], yb[big] = _experts_of(state[big], 0, p, after=state[small][1])
    rows[small], yb[small] = _experts_of(state[small], 0, p)
    state[big] = _conv_layer(state[big], rows[big], seq[big], p, after=yb[small])
    state[small] = _conv_layer(state[small], rows[small], seq[small], p)
    rows[big], _ = _experts_of(state[big], 1, p, after=state[small][1])
    rows[small], _ = _experts_of(state[small], 1, p)
    outs = {i: _combine_call(state[i][0], state[i][2], rows[i]).reshape(xs[i].shape) for i in (big, small)}
    return tuple(outs[i] for i in range(len(xs)))


def kernel(x_prompt, x_sample, ln_mix, ln_ffn, w_qkv, q_gain, k_gain, w_o, w_pw1, b_pw1, w_dw, b_dw,
           conv_ln_g, conv_ln_b, w_pw2, b_pw2, w_router_group, b_router_group, w_router_expert,
           b_router_expert, w_gate, w_up, w_down):
    row = lambda v: v.reshape(1, -1).astype(F32)
    n_heads = N_GROUPS * HEADS_PER_GROUP
    head_id = np.arange(ATTN_WIDTH) // HEAD_DIM
    mseg = jnp.asarray((head_id[:, None] == head_id[None, :]) / HEAD_DIM, BF16)
    slot_id = np.arange(GROUP_WIDTH) // HEAD_DIM
    expand = jnp.asarray(np.arange(LANES)[:, None] == slot_id[None, :], BF16)
    wr0, br0 = _router_weights(w_router_group[0], b_router_group[0], w_router_expert[0], b_router_expert[0])
    wr1, br1 = _router_weights(w_router_group[1], b_router_group[1], w_router_expert[1], b_router_expert[1])
    p = dict(
        ln_mix0=row(ln_mix[0]), ln_mix1=row(ln_mix[1]), ln_ffn0=row(ln_ffn[0]), ln_ffn1=row(ln_ffn[1]),
        w_qkv=w_qkv[0].astype(BF16), mseg=mseg, expand=expand,
        qg=row(jnp.tile(q_gain[0], n_heads) * (HEAD_DIM ** -0.5 * LOG2E)), kg=row(jnp.tile(k_gain[0], n_heads)),
        w_o=w_o[0].astype(BF16),
        w_pw1=w_pw1[0].astype(BF16), b_pw1=row(b_pw1[0]), w_dw=w_dw[0].astype(F32), b_dw=row(b_dw[0]),
        conv_ln_g=row(conv_ln_g[0]), conv_ln_b=row(conv_ln_b[0]),
        w_pw2=w_pw2[0].astype(BF16), b_pw2=row(b_pw2[0]),
        wr0=wr0, br0=br0, wr1=wr1, br1=br1,
        experts=(w_gate, w_up, w_down),
    )
    return _encoder((x_prompt, x_sample), p)
```

```python
import functools

import jax
import jax.numpy as jnp
import numpy as np
from jax import lax
from jax.experimental import pallas as pl
from jax.experimental.pallas import tpu as pltpu
from jax.experimental.pallas import tpu_sc as plsc

D_MODEL = 1024
HEAD_DIM = 64
HEADS_PER_GROUP = 4
DILATIONS = (1, 4, 16)
HALF_WINDOW = 64
N_GROUPS = len(DILATIONS)
GROUP_WIDTH = HEADS_PER_GROUP * HEAD_DIM
ATTN_WIDTH = N_GROUPS * GROUP_WIDTH
ROPE_DIM = HEAD_DIM // 4
ROPE_THETA = 500000.0
CONV_WIDTH = 31
CONV_HALO = 16
N_EXPERT_GROUPS = 4
EXPERTS_PER_GROUP = 8
N_EXPERTS = N_EXPERT_GROUPS * EXPERTS_PER_GROUP
TOP_K = 2
D_EXPERT = 512
EPS = 1e-6
NEG_INF = -1e30
LOG2E = 1.4426950408889634
LN2 = 0.6931471805599453

LANES = 128
SUBLANES = 8
MXU_WIDTH = 256
ROW_TILE = 512
ATTN_Q_TILE = 128
ATTN_STEP_ROWS = 512
MOE_BLOCK = 512
VMEM_LIMIT = 56 * 1024 * 1024

PACKED_WIDTH = D_MODEL // 2
SC_WINDOW = 128
SC_ROW_SPLIT = 2
SC_ROW_WORDS = PACKED_WIDTH // SC_ROW_SPLIT

F32 = jnp.float32
BF16 = jnp.bfloat16


def _cparams(sem):
    return pltpu.CompilerParams(dimension_semantics=sem, vmem_limit_bytes=VMEM_LIMIT)


def _rms(x, gain):
    return x * lax.rsqrt(jnp.mean(x * x, axis=-1, keepdims=True) + EPS) * gain


def _pack_rows(x):
    bits = pltpu.bitcast(x.astype(BF16).astype(F32), jnp.uint32)
    half = x.shape[1] // 2
    return bits[:, :half] | (bits[:, half:] >> 16)


def _unpack_rows(w):
    hi = pltpu.bitcast(w & jnp.uint32(0xFFFF0000), F32)
    lo = pltpu.bitcast(w << 16, F32)
    return hi, lo


def _store_by_residue(out_ref, val, dil, scr, row0=0):
    rows, width = val.shape
    out_rows = slice(row0 // dil, (row0 + rows) // dil)
    if dil == 1:
        out_ref[out_rows, :] = val.astype(out_ref.dtype)
        return
    for c in range(width // LANES):
        scr[c] = val[:, c * LANES:(c + 1) * LANES]
    for r in range(dil):
        for c in range(width // LANES):
            lanes = slice(r * width + c * LANES, r * width + (c + 1) * LANES)
            out_ref[out_rows, lanes] = scr[c, pl.ds(r, rows // dil, stride=dil), :].astype(out_ref.dtype)


def _qkv_kernel(x_ref, g_ref, w_ref, mseg_ref, qg_ref, kg_ref, cos_ref, sa_ref, sb_ref,
                q0, q1, q2, k0, k1, k2, v0, v1, v2, *scrs):
    scrs = iter(scrs)

    def store(out_ref, val, g, row0):
        _store_by_residue(out_ref, val, DILATIONS[g], next(scrs) if DILATIONS[g] > 1 else None, row0)

    for rows in _row_halves(x_ref.shape[0]):
        h = _rms(x_ref[rows, :], g_ref[...]).astype(BF16)
        q = jnp.dot(h, w_ref[:, 0:ATTN_WIDTH], preferred_element_type=F32)
        k = jnp.dot(h, w_ref[:, ATTN_WIDTH:2 * ATTN_WIDTH], preferred_element_type=F32)
        v = jnp.dot(h, w_ref[:, 2 * ATTN_WIDTH:3 * ATTN_WIDTH], preferred_element_type=F32)
        ms_q = jnp.dot((q * q).astype(BF16), mseg_ref[...], preferred_element_type=F32)
        ms_k = jnp.dot((k * k).astype(BF16), mseg_ref[...], preferred_element_type=F32)
        for g, o in enumerate((v0, v1, v2)):
            store(o, v[:, g * GROUP_WIDTH:(g + 1) * GROUP_WIDTH], g, rows.start)
        cos = cos_ref[rows, :]
        sa = sa_ref[rows, :]
        sb = sb_ref[rows, :]
        for t, ms, gain, outs in ((q, ms_q, qg_ref[...], (q0, q1, q2)), (k, ms_k, kg_ref[...], (k0, k1, k2))):
            tn = t * lax.rsqrt(ms + EPS) * gain
            for g in range(N_GROUPS):
                c = tn[:, g * GROUP_WIDTH:(g + 1) * GROUP_WIDTH]
                r = (c * cos + pltpu.roll(c, GROUP_WIDTH - ROPE_DIM // 2, 1) * sa
                     + pltpu.roll(c, ROPE_DIM // 2, 1) * sb)
                store(outs[g], r, g, rows.start)


def _qkv_call(x2d, seq, gain, w_qkv, mseg, qg, kg, cos, sa, sb):
    n = x2d.shape[0]
    tm = ROW_TILE
    tiles_per_seq = seq // tm
    row = lambda i: (i, 0)
    const = lambda i: (0, 0)
    tab = lambda i: (i % tiles_per_seq, 0)
    out_specs = [pl.BlockSpec((tm // dil, dil * GROUP_WIDTH), row) for dil in DILATIONS] * 3
    out_shape = [jax.ShapeDtypeStruct((n // dil, dil * GROUP_WIDTH), BF16) for dil in DILATIONS] * 3
    return pl.pallas_call(
        _qkv_kernel,
        grid=(n // tm,),
        in_specs=[
            pl.BlockSpec((tm, D_MODEL), row),
            pl.BlockSpec((1, D_MODEL), const),
            pl.BlockSpec((D_MODEL, 3 * ATTN_WIDTH), const),
            pl.BlockSpec((ATTN_WIDTH, ATTN_WIDTH), const),
            pl.BlockSpec((1, ATTN_WIDTH), const),
            pl.BlockSpec((1, ATTN_WIDTH), const),
            pl.BlockSpec((tm, GROUP_WIDTH), tab),
            pl.BlockSpec((tm, GROUP_WIDTH), tab),
            pl.BlockSpec((tm, GROUP_WIDTH), tab),
        ],
        out_specs=out_specs,
        out_shape=out_shape,
        scratch_shapes=[pltpu.VMEM((GROUP_WIDTH // LANES, tm // 2, LANES), F32)]
        * (2 * 3 * sum(dil > 1 for dil in DILATIONS)),
        compiler_params=_cparams(("parallel",)),
        name="qkv_proj",
    )(x2d, gain, w_qkv, mseg, qg, kg, cos, sa, sb)


def _attn_kernel(q_ref, k_ref, v_ref, o_ref, lse_ref, *, ls, rows, res, win):
    i = pl.program_id(2)
    tq = ATTN_Q_TILE
    rel = (lax.broadcasted_iota(jnp.int32, (tq, win), 0) - lax.broadcasted_iota(jnp.int32, (tq, win), 1))
    lane = lax.broadcasted_iota(jnp.int32, (tq, LANES), 1)
    lane_kv = lax.broadcasted_iota(jnp.int32, (win, LANES), 1)

    blocks = []
    for r in range(res):
        for sb in range(rows // tq):
            q_start = i * rows + sb * tq
            k_start = pl.multiple_of(jnp.clip(q_start - HALF_WINDOW, 0, ls - win), HALF_WINDOW)
            mask = jnp.abs(rel + (q_start - k_start)) <= HALF_WINDOW
            blocks.append((r, slice(sb * tq, (sb + 1) * tq), k_start, mask))
    chains = [(blk, h) for blk in blocks for h in range(HEADS_PER_GROUP)]

    def cols(r, h):
        return slice(r * GROUP_WIDTH + h * HEAD_DIM, r * GROUP_WIDTH + (h + 1) * HEAD_DIM)

    scores = [lax.dot_general(q_ref[0, q_rows, cols(r, h)], k_ref[0, pl.ds(k_start, win), cols(r, h)],
                              (((1,), (1,)), ((), ())), preferred_element_type=F32)
              for (r, q_rows, k_start, _), h in chains]
    maxes, probs = [], []
    for ((_, _, _, mask), _), s in zip(chains, scores):
        s = jnp.where(mask, s, NEG_INF)
        m = jnp.max(s, axis=-1, keepdims=True)
        maxes.append(m)
        probs.append(jnp.exp2(s - m).astype(BF16))
    outs = []
    for ((r, _, k_start, _), h), p in zip(chains, probs):
        pair = slice(r * GROUP_WIDTH + (h // 2) * LANES, r * GROUP_WIDTH + (h // 2 + 1) * LANES)
        own_half = (lane_kv < HEAD_DIM) if h % 2 == 0 else (lane_kv >= HEAD_DIM)
        v_ext = jnp.where(own_half, v_ref[0, pl.ds(k_start, win), pair], jnp.ones((), BF16))
        outs.append(jnp.dot(p, v_ext, preferred_element_type=F32))
    lse_tiles = {}
    for ((r, q_rows, _, _), h), o_ext, m in zip(chains, outs, maxes):
        den_ext = pltpu.roll(o_ext, HEAD_DIM, 1)
        own = slice((h % 2) * HEAD_DIM, (h % 2 + 1) * HEAD_DIM)
        o_ref[0, q_rows, cols(r, h)] = (o_ext / den_ext)[:, own].astype(BF16)
        den_at_h = den_ext if h % 2 == 0 else o_ext
        key = (r, q_rows.start)
        lse_tiles[key] = jnp.where(lane == h, (m + jnp.log2(den_at_h)) * LN2,
                                   lse_tiles.get(key, jnp.zeros((tq, LANES), F32)))
    for (r, q0), tile in lse_tiles.items():
        lse_ref[0, q0:q0 + tq, r * LANES:(r + 1) * LANES] = tile


def _attn_call(q, k, v, batch, seq, dil):
    ls = seq // dil
    rows = min(ATTN_STEP_ROWS, ls)
    res = max(1, min(dil, ATTN_STEP_ROWS // rows))
    win = min(ATTN_Q_TILE + 2 * HALF_WINDOW, ls)
    view = lambda t: t.reshape(batch, ls, dil * GROUP_WIDTH)
    qmap = lambda b, r, i: (b, i, r)
    kvmap = lambda b, r, i: (b, 0, r)
    o, lse = pl.pallas_call(
        functools.partial(_attn_kernel, ls=ls, rows=rows, res=res, win=win),
        grid=(batch, dil // res, ls // rows),
        in_specs=[
            pl.BlockSpec((1, rows, res * GROUP_WIDTH), qmap),
            pl.BlockSpec((1, ls, res * GROUP_WIDTH), kvmap),
            pl.BlockSpec((1, ls, res * GROUP_WIDTH), kvmap),
        ],
        out_specs=[
            pl.BlockSpec((1, rows, res * GROUP_WIDTH), qmap),
            pl.BlockSpec((1, rows, res * LANES), qmap),
        ],
        out_shape=[
            jax.ShapeDtypeStruct((batch, ls, dil * GROUP_WIDTH), BF16),
            jax.ShapeDtypeStruct((batch, ls, dil * LANES), F32),
        ],
        compiler_params=_cparams(("parallel", "parallel", "arbitrary")),
        name=f"banded_attn_d{dil}",
    )(view(q), view(k), view(v))
    return o.reshape(batch * ls, dil * GROUP_WIDTH), lse.reshape(batch * ls, dil * LANES)


def _route_logits(x, gain, wr_ref, br_ref, hp_ref, rows):
    h = _rms(x, gain)
    packed = _pack_rows(h)
    for c in range(SC_ROW_SPLIT):
        hp_ref[c, rows, :] = packed[:, c * SC_ROW_WORDS:(c + 1) * SC_ROW_WORDS]
    h_hi = h.astype(BF16)
    h_lo = (h - h_hi.astype(F32)).astype(BF16)
    acc = (jnp.dot(h_hi, wr_ref[...], preferred_element_type=F32)
           + jnp.dot(h_lo, wr_ref[...], preferred_element_type=F32))
    return acc[:, :LANES] + acc[:, LANES:] + br_ref[...]


def _route_decide(logits, ri_ref, rg_ref, rows):
    lane = lax.broadcasted_iota(jnp.int32, logits.shape, 1)
    lane_f = lane.astype(F32)
    big = jnp.float32(2 * LANES)

    is_g = lane < N_EXPERT_GROUPS
    gl = jnp.where(is_g, logits, NEG_INF)
    gmax = jnp.max(gl, axis=-1, keepdims=True)
    grp = jnp.min(jnp.where(gl == gmax, lane_f, big), axis=-1, keepdims=True)
    p_grp = 1.0 / jnp.sum(jnp.where(is_g, jnp.exp(gl - gmax), 0.0), axis=-1, keepdims=True)

    lo = N_EXPERT_GROUPS + grp * EXPERTS_PER_GROUP
    in_grp = (lane_f >= lo) & (lane_f < lo + EXPERTS_PER_GROUP)
    el = jnp.where(in_grp, logits, NEG_INF)
    v1 = jnp.max(el, axis=-1, keepdims=True)
    i1 = jnp.min(jnp.where(el == v1, lane_f, big), axis=-1, keepdims=True)
    el2 = jnp.where(lane_f == i1, NEG_INF, el)
    v2 = jnp.max(el2, axis=-1, keepdims=True)
    i2 = jnp.min(jnp.where(el2 == v2, lane_f, big), axis=-1, keepdims=True)
    e21 = jnp.exp(v2 - v1)
    g1 = p_grp / (1.0 + e21)
    g2 = g1 * e21
    ri = jnp.where(lane == 0, i1 - N_EXPERT_GROUPS, jnp.where(lane == 1, i2 - N_EXPERT_GROUPS, 0.0))
    ri_ref[:, rows] = jnp.transpose(ri)[:SUBLANES, :].astype(jnp.int32)
    rg_ref[rows, :] = jnp.where(lane == 0, g1, jnp.where(lane == 1, g2, 0.0))


def _row_halves(tm):
    return [slice(0, tm // 2), slice(tm // 2, tm)]


def _route_out_specs(n, tm):
    row = lambda i: (i, 0)
    specs = [
        pl.BlockSpec((tm, D_MODEL), row),
        pl.BlockSpec((SUBLANES, tm), lambda i: (0, i)),
        pl.BlockSpec((tm, LANES), row),
        pl.BlockSpec((SC_ROW_SPLIT, tm, SC_ROW_WORDS), lambda i: (0, i, 0)),
    ]
    shapes = [
        jax.ShapeDtypeStruct((n, D_MODEL), F32),
        jax.ShapeDtypeStruct((SUBLANES, n), jnp.int32),
        jax.ShapeDtypeStruct((n, LANES), F32),
        jax.ShapeDtypeStruct((SC_ROW_SPLIT, n, SC_ROW_WORDS), jnp.uint32),
    ]
    return specs, shapes


def _load_by_residue(in_ref, dil, scr):
    if dil == 1:
        return in_ref[...].astype(F32)
    chunks, rows, _ = scr.shape
    width = chunks * LANES
    for r in range(dil):
        for c in range(chunks):
            lanes = slice(r * width + c * LANES, r * width + (c + 1) * LANES)
            scr[c, pl.ds(r, rows // dil, stride=dil), :] = in_ref[:, lanes].astype(F32)
    return jnp.concatenate([scr[c] for c in range(chunks)], axis=1) if chunks > 1 else scr[0]


def _attn_out_kernel(o0, o1, o2, l0, l1, l2, x_ref, wo_ref, ex_ref, g_ref, wr_ref, br_ref,
                     y_ref, ri_ref, rg_ref, hp_ref, o_scr1, o_scr2, l_scr1, l_scr2):
    ls = [_load_by_residue(l, dil, scr)
          for l, dil, scr in zip((l0, l1, l2), DILATIONS, (None, l_scr1, l_scr2))]
    m = jnp.maximum(jnp.maximum(ls[0], ls[1]), ls[2])
    es = [jnp.exp(l - m) for l in ls]
    inv = 1.0 / (es[0] + es[1] + es[2])
    a_wide = []
    for g in range(N_GROUPS):
        alpha = es[g] * inv
        a_hi = alpha.astype(BF16)
        a_lo = (alpha - a_hi.astype(F32)).astype(BF16)
        a_wide.append(jnp.dot(a_hi, ex_ref[...], preferred_element_type=F32)
                      + jnp.dot(a_lo, ex_ref[...], preferred_element_type=F32))
    scaled = jnp.concatenate(
        [(_load_by_residue(o_ref, dil, scr) * a).astype(BF16)
         for o_ref, dil, scr, a in zip((o0, o1, o2), DILATIONS, (None, o_scr1, o_scr2), a_wide)], axis=1)
    halves = _row_halves(x_ref.shape[0])
    ys = [x_ref[rows, :] + jnp.dot(scaled[rows, :], wo_ref[...], preferred_element_type=F32)
          for rows in halves]
    for rows, y in zip(halves, ys):
        y_ref[rows, :] = y
    logits = [_route_logits(y, g_ref[...], wr_ref, br_ref, hp_ref, rows) for rows, y in zip(halves, ys)]
    for rows, lg in zip(halves, logits):
        _route_decide(lg, ri_ref, rg_ref, rows)


def _attn_out_call(os_, lses, x2d, w_o, expand, gain, wr, br):
    n = x2d.shape[0]
    tm = ROW_TILE
    row = lambda i: (i, 0)
    const = lambda i: (0, 0)
    out_specs, out_shape = _route_out_specs(n, tm)
    return pl.pallas_call(
        _attn_out_kernel,
        grid=(n // tm,),
        in_specs=(
            [pl.BlockSpec((tm // dil, dil * GROUP_WIDTH), row) for dil in DILATIONS]
            + [pl.BlockSpec((tm // dil, dil * LANES), row) for dil in DILATIONS]
            + [
                pl.BlockSpec((tm, D_MODEL), row),
                pl.BlockSpec((ATTN_WIDTH, D_MODEL), const),
                pl.BlockSpec((LANES, GROUP_WIDTH), const),
                pl.BlockSpec((1, D_MODEL), const),
                pl.BlockSpec((D_MODEL, 2 * LANES), const),
                pl.BlockSpec((1, LANES), const),
            ]
        ),
        out_specs=out_specs,
        out_shape=out_shape,
        scratch_shapes=([pltpu.VMEM((GROUP_WIDTH // LANES, tm, LANES), F32)] * 2
                        + [pltpu.VMEM((1, tm, LANES), F32)] * 2),
        compiler_params=_cparams(("parallel",)),
        name="attn_out_router",
    )(*os_, *lses, x2d, w_o, expand, gain, wr, br)


N_MOE_REFS = 2 + SC_ROW_SPLIT * TOP_K


def _conv_kernel(*refs, seq, tm):
    cur, prev, nxt = (refs[b * N_MOE_REFS:(b + 1) * N_MOE_REFS] for b in range(3))
    (gm_ref, w1_ref, b1_ref, wdw_ref, bdw_ref, lng_ref, lnb_ref, w2_ref, b2_ref, gf_ref, wr_ref, br_ref,
     y_ref, ri_ref, rg_ref, hp_ref, x_scr, u_scr, c_scr, us_scr, cs_scr) = refs[3 * N_MOE_REFS:]
    i = pl.program_id(0)
    pos0 = (i * tm) % seq
    hl = CONV_HALO
    x_scr[0:hl, :] = _combined_rows(prev[0], prev[1], prev[2:])
    x_scr[hl:hl + tm, :] = _combined_rows(cur[0], cur[1], cur[2:])
    x_scr[hl + tm:hl + tm + hl, :] = _combined_rows(nxt[0], nxt[1], nxt[2:])
    halves = _row_halves(tm)
    first = hl - CONV_WIDTH // 2
    r_all = lax.broadcasted_iota(jnp.int32, (tm + 2 * hl, 1), 0)
    outside = ((r_all < hl) & (pos0 == 0)) | ((r_all >= hl + tm) & (pos0 + tm == seq))
    u_parts = [slice(0, tm // 2 + 2 * hl), slice(tm // 2 + 2 * hl, tm + 2 * hl)]
    gated = []
    for part in u_parts:
        h = _rms(x_scr[part, :], gm_ref[...]).astype(BF16)
        a = jnp.dot(h, w1_ref[...], preferred_element_type=F32) + b1_ref[...]
        gated.append(a)
    for part, a in zip(u_parts, gated):
        u = a[:, :D_MODEL] * jax.nn.sigmoid(a[:, D_MODEL:])
        u = jnp.where(outside[part, :], 0.0, u)
        for c in range(D_MODEL // LANES):
            u_scr[c, part, :] = u[:, c * LANES:(c + 1) * LANES]

    seg = (tm // 2) // SUBLANES
    seg_tiles = seg + 2 * hl
    group = SUBLANES
    logits = []
    for rows in halves:
        for c in range(D_MODEL // LANES):
            cols = slice(c * LANES, (c + 1) * LANES)
            for sgm in range(SUBLANES):
                r0 = rows.start + seg * sgm
                us_scr[c, pl.ds(sgm, seg_tiles, stride=SUBLANES), :] = u_scr[c, r0:r0 + seg_tiles, :]
            for k0 in range(0, seg, group):
                accs = [jnp.zeros((SUBLANES, LANES), F32) + bdw_ref[:, cols] for _ in range(group)]
                for t in range(CONV_WIDTH):
                    w_t = wdw_ref[t:t + 1, cols]
                    for j in range(group):
                        tile = (k0 + j + t + first) * SUBLANES
                        accs[j] = accs[j] + us_scr[c, tile:tile + SUBLANES, :] * w_t
                for j in range(group):
                    cs_scr[c, (k0 + j) * SUBLANES:(k0 + j + 1) * SUBLANES, :] = accs[j]
            for sgm in range(SUBLANES):
                r0 = rows.start + seg * sgm
                c_scr[c, r0:r0 + seg, :] = cs_scr[c, pl.ds(sgm, seg, stride=SUBLANES), :]
        cv = jnp.concatenate([c_scr[c, rows, :] for c in range(D_MODEL // LANES)], axis=1)
        mu = jnp.mean(cv, axis=-1, keepdims=True)
        xc = cv - mu
        var = jnp.mean(xc * xc, axis=-1, keepdims=True)
        ln = xc * lax.rsqrt(var + EPS) * lng_ref[...] + lnb_ref[...]
        act = (ln * jax.nn.sigmoid(ln)).astype(BF16)
        y = (x_scr[hl + rows.start:hl + rows.stop, :]
             + jnp.dot(act, w2_ref[...], preferred_element_type=F32) + b2_ref[...])
        y_ref[rows, :] = y
        logits.append(_route_logits(y, gf_ref[...], wr_ref, br_ref, hp_ref, rows))
    for rows, lg in zip(halves, logits):
        _route_decide(lg, ri_ref, rg_ref, rows)


def _conv_call(y2d, rg, rows, seq, gm, w1, b1, wdw, bdw, lng, lnb, w2, b2, gf, wr, br):
    n = y2d.shape[0]
    tm = ROW_TILE
    hl = CONV_HALO
    per = tm // hl
    tiles = n // tm
    const = lambda i: (0, 0)
    cur_blk = lambda i: i
    prev_blk = lambda i: jnp.maximum(i * per - 1, 0)
    next_blk = lambda i: jnp.minimum((i + 1) * per, n // hl - 1)

    def moe_specs(rows_blk, blk_of, blocks_per_piece):
        at = lambda off: (lambda i: (blk_of(i) + off, 0))
        return ([pl.BlockSpec((rows_blk, D_MODEL), at(0)), pl.BlockSpec((rows_blk, LANES), at(0))]
                + [pl.BlockSpec((rows_blk, SC_ROW_WORDS), at(j * blocks_per_piece))
                   for j in range(SC_ROW_SPLIT * TOP_K)])

    moe_args = [y2d, rg] + [rows] * (SC_ROW_SPLIT * TOP_K)
    out_specs, out_shape = _route_out_specs(n, tm)
    return pl.pallas_call(
        functools.partial(_conv_kernel, seq=seq, tm=tm),
        grid=(tiles,),
        in_specs=(
            moe_specs(tm, cur_blk, tiles) + moe_specs(hl, prev_blk, tiles * per)
            + moe_specs(hl, next_blk, tiles * per)
            + [
                pl.BlockSpec((1, D_MODEL), const),
                pl.BlockSpec((D_MODEL, 2 * D_MODEL), const),
                pl.BlockSpec((1, 2 * D_MODEL), const),
                pl.BlockSpec((CONV_WIDTH, D_MODEL), const),
                pl.BlockSpec((1, D_MODEL), const),
                pl.BlockSpec((1, D_MODEL), const),
                pl.BlockSpec((1, D_MODEL), const),
                pl.BlockSpec((D_MODEL, D_MODEL), const),
                pl.BlockSpec((1, D_MODEL), const),
                pl.BlockSpec((1, D_MODEL), const),
                pl.BlockSpec((D_MODEL, 2 * LANES), const),
                pl.BlockSpec((1, LANES), const),
            ]
        ),
        out_specs=out_specs,
        out_shape=out_shape,
        scratch_shapes=[
            pltpu.VMEM((tm + 2 * hl, D_MODEL), F32),
            pltpu.VMEM((D_MODEL // LANES, tm + 2 * hl, LANES), F32),
            pltpu.VMEM((D_MODEL // LANES, tm, LANES), F32),
            pltpu.VMEM((D_MODEL // LANES, (tm // 2 // SUBLANES + 2 * hl) * SUBLANES, LANES), F32),
            pltpu.VMEM((D_MODEL // LANES, tm // 2, LANES), F32),
        ],
        compiler_params=_cparams(("parallel",)),
        name="conv_module_router",
    )(*moe_args, *moe_args, *moe_args, gm, w1, b1, wdw, bdw, lng, lnb, w2, b2, gf, wr, br)


def _sc_mesh():
    return plsc.VectorSubcoreMesh(core_axis_name="core", subcore_axis_name="subcore")


def _sc_scatter_rows(src, idx, n_src_windows, out_rows):
    m = idx.shape[1]
    width = src.shape[1]
    nsw = n_src_windows

    @functools.partial(pl.kernel, out_type=jax.ShapeDtypeStruct((out_rows, width), src.dtype),
                       mesh=_sc_mesh(), name="sc_scatter_rows")
    def scatter(src_hbm, idx_hbm, out_hbm):
        def body(src_vmem, idx_vmem):
            pltpu.sync_copy(src_vmem, out_hbm.at[idx_vmem.at[0]])

        pltpu.emit_pipeline(
            body,
            grid=(m // SC_WINDOW,),
            in_specs=[
                pl.BlockSpec((SC_WINDOW, width), lambda w: ((w // (TOP_K * nsw)) * nsw + w % nsw, 0)),
                pl.BlockSpec((1, SC_WINDOW), lambda w: (0, w)),
            ],
            out_specs=[],
            core_axis_name=("core", "subcore"),
            dimension_semantics=(pltpu.PARALLEL,),
        )(src_hbm, idx_hbm)

    return scatter(src, idx)


def _sc_gather_rows(table, idx):
    m = idx.shape[1]
    width = table.shape[1]

    @functools.partial(pl.kernel, out_type=jax.ShapeDtypeStruct((m, width), table.dtype),
                       mesh=_sc_mesh(), name="sc_gather_rows")
    def gather(table_hbm, idx_hbm, out_hbm):
        def body(idx_vmem, out_vmem):
            pltpu.sync_copy(table_hbm.at[idx_vmem.at[0]], out_vmem)

        pltpu.emit_pipeline(
            body,
            grid=(m // SC_WINDOW,),
            in_specs=[pl.BlockSpec((1, SC_WINDOW), lambda w: (0, w))],
            out_specs=[pl.BlockSpec((SC_WINDOW, width), lambda w: (w, 0))],
            core_axis_name=("core", "subcore"),
            dimension_semantics=(pltpu.PARALLEL,),
        )(idx_hbm, out_hbm)

    return gather(table, idx)


def _expert_kernel(be_ref, nv_ref, xs_ref, wg_ref, wu_ref, wd_ref, yb_ref, wg_s, wu_s, wd_s):
    i = pl.program_id(0)

    @pl.when((i == 0) | (be_ref[i] != be_ref[jnp.maximum(i - 1, 0)]))
    def _():
        wg_s[...] = wg_ref[0, 0].astype(BF16)
        wu_s[...] = wu_ref[0, 0].astype(BF16)
        wd_s[...] = wd_ref[0, 0].astype(BF16)

    @pl.when(nv_ref[i] > 0)
    def _():
        words = jnp.concatenate([xs_ref[c] for c in range(SC_ROW_SPLIT)], axis=1)
        row = lax.broadcasted_iota(jnp.int32, (words.shape[0], 1), 0)
        words = jnp.where(row < nv_ref[i], words, jnp.uint32(0))
        hi, lo = _unpack_rows(words)
        xb = jnp.concatenate([hi.astype(BF16), lo.astype(BF16)], axis=1)
        chunks = [slice(c, c + MXU_WIDTH) for c in range(0, D_EXPERT, MXU_WIDTH)]
        gu = [(jnp.dot(xb, wg_s[:, c], preferred_element_type=F32),
               jnp.dot(xb, wu_s[:, c], preferred_element_type=F32)) for c in chunks]
        y = None
        for c, (g, u) in zip(chunks, gu):
            hid = (g * jax.nn.sigmoid(g) * u).astype(BF16)
            part = jnp.dot(hid, wd_s[c, :], preferred_element_type=F32)
            y = part if y is None else y + part
        packed = _pack_rows(y)
        for c in range(SC_ROW_SPLIT):
            yb_ref[c] = packed[:, c * SC_ROW_WORDS:(c + 1) * SC_ROW_WORDS]


def _expert_call(block_expert, block_valid, xs, layer, w_gate, w_up, w_down):
    p_total = xs.shape[1]
    blk = MOE_BLOCK
    wmap = lambda i, be, nv: (layer, be[i], 0, 0)
    rows = lambda i, be, nv: (0, i, 0)
    return pl.pallas_call(
        _expert_kernel,
        grid_spec=pltpu.PrefetchScalarGridSpec(
            num_scalar_prefetch=2,
            grid=(p_total // blk,),
            in_specs=[
                pl.BlockSpec((SC_ROW_SPLIT, blk, SC_ROW_WORDS), rows),
                pl.BlockSpec((1, 1, D_MODEL, D_EXPERT), wmap),
                pl.BlockSpec((1, 1, D_MODEL, D_EXPERT), wmap),
                pl.BlockSpec((1, 1, D_EXPERT, D_MODEL), wmap),
            ],
            out_specs=pl.BlockSpec((SC_ROW_SPLIT, blk, SC_ROW_WORDS), rows),
            scratch_shapes=[
                pltpu.VMEM((D_MODEL, D_EXPERT), BF16),
                pltpu.VMEM((D_MODEL, D_EXPERT), BF16),
                pltpu.VMEM((D_EXPERT, D_MODEL), BF16),
            ],
        ),
        out_shape=jax.ShapeDtypeStruct((SC_ROW_SPLIT, p_total, SC_ROW_WORDS), jnp.uint32),
        compiler_params=_cparams(("arbitrary",)),
        name="moe_experts",
    )(block_expert, block_valid, xs, w_gate, w_up, w_down)


def _combined_rows(y_ref, rg_ref, piece_refs):
    rg = rg_ref[...]
    his, los = [], []
    for c in range(SC_ROW_SPLIT):
        hi0, lo0 = _unpack_rows(piece_refs[c * TOP_K][...])
        hi1, lo1 = _unpack_rows(piece_refs[c * TOP_K + 1][...])
        his.append(rg[:, 0:1] * hi0 + rg[:, 1:2] * hi1)
        los.append(rg[:, 0:1] * lo0 + rg[:, 1:2] * lo1)
    return y_ref[...] + jnp.concatenate(his + los, axis=1)


def _combine_kernel(y_ref, rg_ref, *refs):
    refs[-1][...] = _combined_rows(y_ref, rg_ref, refs[:-1])


def _combine_call(y2d, rg, rows):
    n = y2d.shape[0]
    tm = ROW_TILE
    tiles = n // tm
    return pl.pallas_call(
        _combine_kernel,
        grid=(tiles,),
        in_specs=[
            pl.BlockSpec((tm, D_MODEL), lambda i: (i, 0)),
            pl.BlockSpec((tm, LANES), lambda i: (i, 0)),
        ] + [
            pl.BlockSpec((tm, SC_ROW_WORDS), functools.partial(lambda i, j: (i + j * tiles, 0), j=j))
            for j in range(SC_ROW_SPLIT * TOP_K)
        ],
        out_specs=pl.BlockSpec((tm, D_MODEL), lambda i: (i, 0)),
        out_shape=jax.ShapeDtypeStruct((n, D_MODEL), F32),
        compiler_params=_cparams(("parallel",)),
        name="moe_combine",
    )(y2d, rg, *([rows] * (SC_ROW_SPLIT * TOP_K)))


def _slot_tables(ri, n):
    e = ri[:TOP_K]
    experts = jnp.arange(N_EXPERTS, dtype=jnp.int32)[None, :, None]
    onehot = (e[:, None, :] == experts).astype(jnp.int32)
    tok_cnt = onehot.sum(axis=0)
    t = 256
    cnt_t = tok_cnt.reshape(N_EXPERTS, n // t, t)
    tri = (jnp.arange(t)[:, None] < jnp.arange(t)[None, :]).astype(F32)
    within = jnp.einsum("ens,st->ent", cnt_t.astype(F32), tri).astype(jnp.int32)
    tile_sum = cnt_t.sum(axis=2)
    tile_base = jnp.cumsum(tile_sum, axis=1) - tile_sum
    rank = (within + tile_base[:, :, None]).reshape(N_EXPERTS, n)
    counts = tile_sum.sum(axis=1)
    padded = ((counts + MOE_BLOCK - 1) // MOE_BLOCK) * MOE_BLOCK
    pad_end = jnp.cumsum(padded)
    pad_start = pad_end - padded
    slot = ((rank + pad_start[:, None])[None, :, :] * onehot).sum(axis=1)
    n_blocks = (n * TOP_K) // MOE_BLOCK + N_EXPERTS
    block_start = jnp.arange(n_blocks, dtype=jnp.int32) * MOE_BLOCK
    block_expert = jnp.minimum((pad_end[None, :] <= block_start[:, None]).sum(axis=-1),
                               N_EXPERTS - 1).astype(jnp.int32)
    used_end = (pad_start + counts)[block_expert]
    block_valid = jnp.clip(used_end - block_start, 0, MOE_BLOCK).astype(jnp.int32)
    p_total = n_blocks * MOE_BLOCK
    slot_km = slot.reshape(1, TOP_K * n).astype(jnp.int32)
    idx = jnp.concatenate([slot_km + c * p_total for c in range(SC_ROW_SPLIT)], axis=1)
    return idx, block_expert, block_valid, p_total


def _after(value, other):
    if other is None:
        return value
    return lax.optimization_barrier((value, other))[0]


def _moe_rows(n, ri, hp, layer, experts, after=None):
    idx, block_expert, block_valid, p_total = _slot_tables(ri, n)
    xs = _sc_scatter_rows(hp.reshape(SC_ROW_SPLIT * n, SC_ROW_WORDS), idx, n // SC_WINDOW,
                          SC_ROW_SPLIT * p_total)
    yb = _expert_call(block_expert, _after(block_valid, after),
                      xs.reshape(SC_ROW_SPLIT, p_total, SC_ROW_WORDS), layer, *experts)
    return _sc_gather_rows(yb.reshape(SC_ROW_SPLIT * p_total, SC_ROW_WORDS), idx), yb


def _rope_tables(seq):
    pos = jnp.arange(seq, dtype=F32)
    inv_freq = ROPE_THETA ** (-jnp.arange(0, ROPE_DIM, 2, dtype=F32) / ROPE_DIM)
    ang = pos[:, None] * inv_freq[None, :]
    cos = jnp.cos(ang)
    sin = jnp.sin(ang)
    half = ROPE_DIM // 2
    rest = HEAD_DIM - ROPE_DIM
    cos_h = jnp.concatenate([cos, cos, jnp.ones((seq, rest), F32)], axis=1)
    sa_h = jnp.concatenate([-sin, jnp.zeros((seq, half + rest), F32)], axis=1)
    sb_h = jnp.concatenate([jnp.zeros((seq, half), F32), sin, jnp.zeros((seq, rest), F32)], axis=1)
    tile = lambda t: jnp.tile(t, (1, HEADS_PER_GROUP))
    return tile(cos_h), tile(sa_h), tile(sb_h)


def _router_weights(w_rg, b_rg, w_re, b_re):
    w = jnp.concatenate([w_rg, w_re], axis=1)
    w = jnp.pad(w, ((0, 0), (0, LANES - w.shape[1])))
    w_hi = w.astype(BF16)
    w_lo = (w - w_hi.astype(F32)).astype(BF16)
    b = jnp.pad(jnp.concatenate([b_rg, b_re]), (0, LANES - N_EXPERT_GROUPS - N_EXPERTS))
    return jnp.concatenate([w_hi, w_lo], axis=1), b.reshape(1, LANES)


def _attention_layer(x, p):
    batch, seq, d = x.shape
    x2d = x.reshape(batch * seq, d)
    cos, sa, sb = _rope_tables(seq)
    qkv = _qkv_call(x2d, seq, p["ln_mix0"], p["w_qkv"], p["mseg"], p["qg"], p["kg"], cos, sa, sb)
    os_, lses = [], []
    for g, dil in enumerate(DILATIONS):
        o, lse = _attn_call(qkv[g], qkv[3 + g], qkv[6 + g], batch, seq, dil)
        os_.append(o)
        lses.append(lse)
    return _attn_out_call(os_, lses, x2d, p["w_o"], p["expand"], p["ln_ffn0"], p["wr0"], p["br0"])


def _conv_layer(state, rows, seq, p, after=None):
    y, _, rg, _ = state
    return _conv_call(y, _after(rg, after), rows, seq, p["ln_mix1"], p["w_pw1"], p["b_pw1"], p["w_dw"],
                      p["b_dw"], p["conv_ln_g"], p["conv_ln_b"], p["w_pw2"], p["b_pw2"],
                      p["ln_ffn1"], p["wr1"], p["br1"])


def _experts_of(state, layer, p, after=None):
    y, ri, _, hp = state
    return _moe_rows(y.shape[0], ri, hp, layer, p["experts"], after)


def _encoder(xs, p):
    small, big = sorted(range(len(xs)), key=lambda i: xs[i].shape[0] * xs[i].shape[1])
    seq = {i: xs[i].shape[1] for i in (small, big)}
    state = {i: _attention_layer(xs[i], p) for i in (big, small)}
    rows, yb = {}, {}
    rows[big], yb[big] = _experts_of(state[big], 0, p, after=state[small][1])
    rows[small], yb[small] = _experts_of(state[small], 0, p)
    state[big] = _conv_layer(state[big], rows[big], seq[big], p, after=yb[small])
    state[small] = _conv_layer(state[small], rows[small], seq[small], p)
    rows[big], _ = _experts_of(state[big], 1, p, after=state[small][1])
    rows[small], _ = _experts_of(state[small], 1, p)
    outs = {i: _combine_call(state[i][0], state[i][2], rows[i]).reshape(xs[i].shape) for i in (big, small)}
    return tuple(outs[i] for i in range(len(xs)))


def kernel(x_prompt, x_sample, ln_mix, ln_ffn, w_qkv, q_gain, k_gain, w_o, w_pw1, b_pw1, w_dw, b_dw,
           conv_ln_g, conv_ln_b, w_pw2, b_pw2, w_router_group, b_router_group, w_router_expert,
           b_router_expert, w_gate, w_up, w_down):
    row = lambda v: v.reshape(1, -1).astype(F32)
    n_heads = N_GROUPS * HEADS_PER_GROUP
    head_id = np.arange(ATTN_WIDTH) // HEAD_DIM
    mseg = jnp.asarray((head_id[:, None] == head_id[None, :]) / HEAD_DIM, BF16)
    slot_id = np.arange(GROUP_WIDTH) // HEAD_DIM
    expand = jnp.asarray(np.arange(LANES)[:, None] == slot_id[None, :], BF16)
    wr0, br0 = _router_weights(w_router_group[0], b_router_group[0], w_router_expert[0], b_router_expert[0])
    wr1, br1 = _router_weights(w_router_group[1], b_router_group[1], w_router_expert[1], b_router_expert[1])
    p = dict(
        ln_mix0=row(ln_mix[0]), ln_mix1=row(ln_mix[1]), ln_ffn0=row(ln_ffn[0]), ln_ffn1=row(ln_ffn[1]),
        w_qkv=w_qkv[0].astype(BF16), mseg=mseg, expand=expand,
        qg=row(jnp.tile(q_gain[0], n_heads) * (HEAD_DIM ** -0.5 * LOG2E)), kg=row(jnp.tile(k_gain[0], n_heads)),
        w_o=w_o[0].astype(BF16),
        w_pw1=w_pw1[0].astype(BF16), b_pw1=row(b_pw1[0]), w_dw=w_dw[0].astype(F32), b_dw=row(b_dw[0]),
        conv_ln_g=row(conv_ln_g[0]), conv_ln_b=row(conv_ln_b[0]),
        w_pw2=w_pw2[0].astype(BF16), b_pw2=row(b_pw2[0]),
        wr0=wr0, br0=br0, wr1=wr1, br1=br1,
        experts=(w_gate, w_up, w_down),
    )
    return _encoder((x_prompt, x_sample), p)
```

```python
import functools

import jax
import jax.numpy as jnp
import numpy as np
from jax import lax
from jax.experimental import pallas as pl
from jax.experimental.pallas import tpu as pltpu
from jax.experimental.pallas import tpu_sc as plsc

D_MODEL = 1024
HEAD_DIM = 64
HEADS_PER_GROUP = 4
DILATIONS = (1, 4, 16)
HALF_WINDOW = 64
N_GROUPS = len(DILATIONS)
GROUP_WIDTH = HEADS_PER_GROUP * HEAD_DIM
ATTN_WIDTH = N_GROUPS * GROUP_WIDTH
ROPE_DIM = HEAD_DIM // 4
ROPE_THETA = 500000.0
CONV_WIDTH = 31
CONV_HALO = 16
N_EXPERT_GROUPS = 4
EXPERTS_PER_GROUP = 8
N_EXPERTS = N_EXPERT_GROUPS * EXPERTS_PER_GROUP
TOP_K = 2
D_EXPERT = 512
EPS = 1e-6
NEG_INF = -1e30
LOG2E = 1.4426950408889634
LN2 = 0.6931471805599453

LANES = 128
SUBLANES = 8
MXU_WIDTH = 256
ROW_TILE = 512
ATTN_Q_TILE = 128
ATTN_STEP_ROWS = 512
MOE_BLOCK = 512
VMEM_LIMIT = 56 * 1024 * 1024

PACKED_WIDTH = D_MODEL // 2
SC_WINDOW = 128
SC_ROW_SPLIT = 2
SC_ROW_WORDS = PACKED_WIDTH // SC_ROW_SPLIT

F32 = jnp.float32
BF16 = jnp.bfloat16


def _cparams(sem):
    return pltpu.CompilerParams(dimension_semantics=sem, vmem_limit_bytes=VMEM_LIMIT)


def _rms(x, gain):
    return x * lax.rsqrt(jnp.mean(x * x, axis=-1, keepdims=True) + EPS) * gain


def _pack_rows(x):
    bits = pltpu.bitcast(x.astype(BF16).astype(F32), jnp.uint32)
    half = x.shape[1] // 2
    return bits[:, :half] | (bits[:, half:] >> 16)


def _unpack_rows(w):
    hi = pltpu.bitcast(w & jnp.uint32(0xFFFF0000), F32)
    lo = pltpu.bitcast(w << 16, F32)
    return hi, lo


def _store_by_residue(out_ref, val, dil, scr, row0=0):
    rows, width = val.shape
    out_rows = slice(row0 // dil, (row0 + rows) // dil)
    if dil == 1:
        out_ref[out_rows, :] = val.astype(out_ref.dtype)
        return
    for c in range(width // LANES):
        scr[c] = val[:, c * LANES:(c + 1) * LANES]
    for r in range(dil):
        for c in range(width // LANES):
            lanes = slice(r * width + c * LANES, r * width + (c + 1) * LANES)
            out_ref[out_rows, lanes] = scr[c, pl.ds(r, rows // dil, stride=dil), :].astype(out_ref.dtype)


def _qkv_kernel(x_ref, g_ref, w_ref, mseg_ref, qg_ref, kg_ref, cos_ref, sa_ref, sb_ref,
                q0, q1, q2, k0, k1, k2, v0, v1, v2, *scrs):
    scrs = iter(scrs)

    def store(out_ref, val, g, row0):
        _store_by_residue(out_ref, val, DILATIONS[g], next(scrs) if DILATIONS[g] > 1 else None, row0)

    for rows in _row_halves(x_ref.shape[0]):
        h = _rms(x_ref[rows, :], g_ref[...]).astype(BF16)
        q = jnp.dot(h, w_ref[:, 0:ATTN_WIDTH], preferred_element_type=F32)
        k = jnp.dot(h, w_ref[:, ATTN_WIDTH:2 * ATTN_WIDTH], preferred_element_type=F32)
        v = jnp.dot(h, w_ref[:, 2 * ATTN_WIDTH:3 * ATTN_WIDTH], preferred_element_type=F32)
        ms_q = jnp.dot((q * q).astype(BF16), mseg_ref[...], preferred_element_type=F32)
        ms_k = jnp.dot((k * k).astype(BF16), mseg_ref[...], preferred_element_type=F32)
        for g, o in enumerate((v0, v1, v2)):
            store(o, v[:, g * GROUP_WIDTH:(g + 1) * GROUP_WIDTH], g, rows.start)
        widen = lambda t: jnp.concatenate([t] * (GROUP_WIDTH // LANES), axis=1)
        cos = widen(cos_ref[rows, :])
        sa = widen(sa_ref[rows, :])
        sb = widen(sb_ref[rows, :])
        for t, ms, gain, outs in ((q, ms_q, qg_ref[...], (q0, q1, q2)), (k, ms_k, kg_ref[...], (k0, k1, k2))):
            tn = t * lax.rsqrt(ms + EPS) * gain
            for g in range(N_GROUPS):
                c = tn[:, g * GROUP_WIDTH:(g + 1) * GROUP_WIDTH]
                r = (c * cos + pltpu.roll(c, GROUP_WIDTH - ROPE_DIM // 2, 1) * sa
                     + pltpu.roll(c, ROPE_DIM // 2, 1) * sb)
                store(outs[g], r, g, rows.start)


def _qkv_call(x2d, seq, gain, w_qkv, mseg, qg, kg, cos, sa, sb):
    n = x2d.shape[0]
    tm = ROW_TILE
    tiles_per_seq = seq // tm
    row = lambda i: (i, 0)
    const = lambda i: (0, 0)
    tab = lambda i: (i % tiles_per_seq, 0)
    out_specs = [pl.BlockSpec((tm // dil, dil * GROUP_WIDTH), row) for dil in DILATIONS] * 3
    out_shape = [jax.ShapeDtypeStruct((n // dil, dil * GROUP_WIDTH), BF16) for dil in DILATIONS] * 3
    return pl.pallas_call(
        _qkv_kernel,
        grid=(n // tm,),
        in_specs=[
            pl.BlockSpec((tm, D_MODEL), row),
            pl.BlockSpec((1, D_MODEL), const),
            pl.BlockSpec((D_MODEL, 3 * ATTN_WIDTH), const),
            pl.BlockSpec((ATTN_WIDTH, ATTN_WIDTH), const),
            pl.BlockSpec((1, ATTN_WIDTH), const),
            pl.BlockSpec((1, ATTN_WIDTH), const),
            pl.BlockSpec((tm, LANES), tab),
            pl.BlockSpec((tm, LANES), tab),
            pl.BlockSpec((tm, LANES), tab),
        ],
        out_specs=out_specs,
        out_shape=out_shape,
        scratch_shapes=[pltpu.VMEM((GROUP_WIDTH // LANES, tm // 2, LANES), F32)]
        * (2 * 3 * sum(dil > 1 for dil in DILATIONS)),
        compiler_params=_cparams(("parallel",)),
        name="qkv_proj",
    )(x2d, gain, w_qkv, mseg, qg, kg, cos, sa, sb)


def _attn_kernel(q_ref, k_ref, v_ref, o_ref, lse_ref, *, ls, rows, res, win):
    i = pl.program_id(2)
    tq = ATTN_Q_TILE
    rel = (lax.broadcasted_iota(jnp.int32, (tq, win), 0) - lax.broadcasted_iota(jnp.int32, (tq, win), 1))
    lane = lax.broadcasted_iota(jnp.int32, (tq, LANES), 1)
    lane_kv = lax.broadcasted_iota(jnp.int32, (win, LANES), 1)

    blocks = []
    for r in range(res):
        for sb in range(rows // tq):
            q_start = i * rows + sb * tq
            k_start = pl.multiple_of(jnp.clip(q_start - HALF_WINDOW, 0, ls - win), HALF_WINDOW)
            mask = jnp.abs(rel + (q_start - k_start)) <= HALF_WINDOW
            blocks.append((r, slice(sb * tq, (sb + 1) * tq), k_start, mask))
    chains = [(blk, h) for blk in blocks for h in range(HEADS_PER_GROUP)]

    def cols(r, h):
        return slice(r * GROUP_WIDTH + h * HEAD_DIM, r * GROUP_WIDTH + (h + 1) * HEAD_DIM)

    scores = [lax.dot_general(q_ref[0, q_rows, cols(r, h)], k_ref[0, pl.ds(k_start, win), cols(r, h)],
                              (((1,), (1,)), ((), ())), preferred_element_type=F32)
              for (r, q_rows, k_start, _), h in chains]
    maxes, probs = [], []
    for ((_, _, _, mask), _), s in zip(chains, scores):
        s = jnp.where(mask, s, NEG_INF)
        m = jnp.max(s, axis=-1, keepdims=True)
        maxes.append(m)
        probs.append(jnp.exp2(s - m).astype(BF16))
    outs = []
    for ((r, _, k_start, _), h), p in zip(chains, probs):
        pair = slice(r * GROUP_WIDTH + (h // 2) * LANES, r * GROUP_WIDTH + (h // 2 + 1) * LANES)
        own_half = (lane_kv < HEAD_DIM) if h % 2 == 0 else (lane_kv >= HEAD_DIM)
        v_ext = jnp.where(own_half, v_ref[0, pl.ds(k_start, win), pair], jnp.ones((), BF16))
        outs.append(jnp.dot(p, v_ext, preferred_element_type=F32))
    lse_tiles = {}
    for ((r, q_rows, _, _), h), o_ext, m in zip(chains, outs, maxes):
        den_ext = pltpu.roll(o_ext, HEAD_DIM, 1)
        own = slice((h % 2) * HEAD_DIM, (h % 2 + 1) * HEAD_DIM)
        o_ref[0, q_rows, cols(r, h)] = (o_ext / den_ext)[:, own].astype(BF16)
        den_at_h = den_ext if h % 2 == 0 else o_ext
        key = (r, q_rows.start)
        lse_tiles[key] = jnp.where(lane == h, (m + jnp.log2(den_at_h)) * LN2,
                                   lse_tiles.get(key, jnp.zeros((tq, LANES), F32)))
    for (r, q0), tile in lse_tiles.items():
        lse_ref[0, q0:q0 + tq, r * LANES:(r + 1) * LANES] = tile


def _attn_call(q, k, v, batch, seq, dil):
    ls = seq // dil
    rows = min(ATTN_STEP_ROWS, ls)
    res = max(1, min(dil, ATTN_STEP_ROWS // rows))
    win = min(ATTN_Q_TILE + 2 * HALF_WINDOW, ls)
    view = lambda t: t.reshape(batch, ls, dil * GROUP_WIDTH)
    qmap = lambda b, r, i: (b, i, r)
    kvmap = lambda b, r, i: (b, 0, r)
    o, lse = pl.pallas_call(
        functools.partial(_attn_kernel, ls=ls, rows=rows, res=res, win=win),
        grid=(batch, dil // res, ls // rows),
        in_specs=[
            pl.BlockSpec((1, rows, res * GROUP_WIDTH), qmap),
            pl.BlockSpec((1, ls, res * GROUP_WIDTH), kvmap),
            pl.BlockSpec((1, ls, res * GROUP_WIDTH), kvmap),
        ],
        out_specs=[
            pl.BlockSpec((1, rows, res * GROUP_WIDTH), qmap),
            pl.BlockSpec((1, rows, res * LANES), qmap),
        ],
        out_shape=[
            jax.ShapeDtypeStruct((batch, ls, dil * GROUP_WIDTH), BF16),
            jax.ShapeDtypeStruct((batch, ls, dil * LANES), F32),
        ],
        compiler_params=_cparams(("parallel", "parallel", "arbitrary")),
        name=f"banded_attn_d{dil}",
    )(view(q), view(k), view(v))
    return o.reshape(batch * ls, dil * GROUP_WIDTH), lse.reshape(batch * ls, dil * LANES)


def _route_logits(x, gain, wr_ref, br_ref, hp_ref, rows):
    h = _rms(x, gain)
    packed = _pack_rows(h)
    for c in range(SC_ROW_SPLIT):
        hp_ref[c, rows, :] = packed[:, c * SC_ROW_WORDS:(c + 1) * SC_ROW_WORDS]
    h_hi = h.astype(BF16)
    h_lo = (h - h_hi.astype(F32)).astype(BF16)
    acc = (jnp.dot(h_hi, wr_ref[...], preferred_element_type=F32)
           + jnp.dot(h_lo, wr_ref[...], preferred_element_type=F32))
    return acc[:, :LANES] + acc[:, LANES:] + br_ref[...]


def _route_decide(logits, ri_ref, rg_ref, rows):
    lane = lax.broadcasted_iota(jnp.int32, logits.shape, 1)
    lane_f = lane.astype(F32)
    big = jnp.float32(2 * LANES)

    is_g = lane < N_EXPERT_GROUPS
    gl = jnp.where(is_g, logits, NEG_INF)
    gmax = jnp.max(gl, axis=-1, keepdims=True)
    grp = jnp.min(jnp.where(gl == gmax, lane_f, big), axis=-1, keepdims=True)
    p_grp = 1.0 / jnp.sum(jnp.where(is_g, jnp.exp(gl - gmax), 0.0), axis=-1, keepdims=True)

    lo = N_EXPERT_GROUPS + grp * EXPERTS_PER_GROUP
    in_grp = (lane_f >= lo) & (lane_f < lo + EXPERTS_PER_GROUP)
    el = jnp.where(in_grp, logits, NEG_INF)
    v1 = jnp.max(el, axis=-1, keepdims=True)
    i1 = jnp.min(jnp.where(el == v1, lane_f, big), axis=-1, keepdims=True)
    el2 = jnp.where(lane_f == i1, NEG_INF, el)
    v2 = jnp.max(el2, axis=-1, keepdims=True)
    i2 = jnp.min(jnp.where(el2 == v2, lane_f, big), axis=-1, keepdims=True)
    e21 = jnp.exp(v2 - v1)
    g1 = p_grp / (1.0 + e21)
    g2 = g1 * e21
    ri = jnp.where(lane == 0, i1 - N_EXPERT_GROUPS, jnp.where(lane == 1, i2 - N_EXPERT_GROUPS, 0.0))
    ri_ref[:, rows] = jnp.transpose(ri)[:SUBLANES, :].astype(jnp.int32)
    rg_ref[rows, :] = jnp.where(lane == 0, g1, jnp.where(lane == 1, g2, 0.0))


def _row_halves(tm):
    return [slice(0, tm // 2), slice(tm // 2, tm)]


def _route_out_specs(n, tm):
    row = lambda i: (i, 0)
    specs = [
        pl.BlockSpec((tm, D_MODEL), row),
        pl.BlockSpec((SUBLANES, tm), lambda i: (0, i)),
        pl.BlockSpec((tm, LANES), row),
        pl.BlockSpec((SC_ROW_SPLIT, tm, SC_ROW_WORDS), lambda i: (0, i, 0)),
    ]
    shapes = [
        jax.ShapeDtypeStruct((n, D_MODEL), F32),
        jax.ShapeDtypeStruct((SUBLANES, n), jnp.int32),
        jax.ShapeDtypeStruct((n, LANES), F32),
        jax.ShapeDtypeStruct((SC_ROW_SPLIT, n, SC_ROW_WORDS), jnp.uint32),
    ]
    return specs, shapes


def _load_by_residue(in_ref, dil, scr):
    if dil == 1:
        return in_ref[...].astype(F32)
    chunks, rows, _ = scr.shape
    width = chunks * LANES
    for r in range(dil):
        for c in range(chunks):
            lanes = slice(r * width + c * LANES, r * width + (c + 1) * LANES)
            scr[c, pl.ds(r, rows // dil, stride=dil), :] = in_ref[:, lanes].astype(F32)
    return jnp.concatenate([scr[c] for c in range(chunks)], axis=1) if chunks > 1 else scr[0]


def _attn_out_kernel(o0, o1, o2, l0, l1, l2, x_ref, wo_ref, ex_ref, g_ref, wr_ref, br_ref,
                     y_ref, ri_ref, rg_ref, hp_ref, o_scr1, o_scr2, l_scr1, l_scr2):
    ls = [_load_by_residue(l, dil, scr)
          for l, dil, scr in zip((l0, l1, l2), DILATIONS, (None, l_scr1, l_scr2))]
    m = jnp.maximum(jnp.maximum(ls[0], ls[1]), ls[2])
    es = [jnp.exp(l - m) for l in ls]
    inv = 1.0 / (es[0] + es[1] + es[2])
    a_wide = []
    for g in range(N_GROUPS):
        alpha = es[g] * inv
        a_hi = alpha.astype(BF16)
        a_lo = (alpha - a_hi.astype(F32)).astype(BF16)
        a_wide.append(jnp.dot(a_hi, ex_ref[...], preferred_element_type=F32)
                      + jnp.dot(a_lo, ex_ref[...], preferred_element_type=F32))
    scaled = jnp.concatenate(
        [(_load_by_residue(o_ref, dil, scr) * a).astype(BF16)
         for o_ref, dil, scr, a in zip((o0, o1, o2), DILATIONS, (None, o_scr1, o_scr2), a_wide)], axis=1)
    halves = _row_halves(x_ref.shape[0])
    ys = [x_ref[rows, :] + jnp.dot(scaled[rows, :], wo_ref[...], preferred_element_type=F32)
          for rows in halves]
    for rows, y in zip(halves, ys):
        y_ref[rows, :] = y
    logits = [_route_logits(y, g_ref[...], wr_ref, br_ref, hp_ref, rows) for rows, y in zip(halves, ys)]
    for rows, lg in zip(halves, logits):
        _route_decide(lg, ri_ref, rg_ref, rows)


def _attn_out_call(os_, lses, x2d, w_o, expand, gain, wr, br):
    n = x2d.shape[0]
    tm = ROW_TILE
    row = lambda i: (i, 0)
    const = lambda i: (0, 0)
    out_specs, out_shape = _route_out_specs(n, tm)
    return pl.pallas_call(
        _attn_out_kernel,
        grid=(n // tm,),
        in_specs=(
            [pl.BlockSpec((tm // dil, dil * GROUP_WIDTH), row) for dil in DILATIONS]
            + [pl.BlockSpec((tm // dil, dil * LANES), row) for dil in DILATIONS]
            + [
                pl.BlockSpec((tm, D_MODEL), row),
                pl.BlockSpec((ATTN_WIDTH, D_MODEL), const),
                pl.BlockSpec((LANES, GROUP_WIDTH), const),
                pl.BlockSpec((1, D_MODEL), const),
                pl.BlockSpec((D_MODEL, 2 * LANES), const),
                pl.BlockSpec((1, LANES), const),
            ]
        ),
        out_specs=out_specs,
        out_shape=out_shape,
        scratch_shapes=([pltpu.VMEM((GROUP_WIDTH // LANES, tm, LANES), F32)] * 2
                        + [pltpu.VMEM((1, tm, LANES), F32)] * 2),
        compiler_params=_cparams(("parallel",)),
        name="attn_out_router",
    )(*os_, *lses, x2d, w_o, expand, gain, wr, br)


N_MOE_REFS = 2 + SC_ROW_SPLIT * TOP_K


def _conv_kernel(*refs, seq, tm):
    cur, prev, nxt = (refs[b * N_MOE_REFS:(b + 1) * N_MOE_REFS] for b in range(3))
    (gm_ref, w1_ref, b1_ref, wdw_ref, bdw_ref, lng_ref, lnb_ref, w2_ref, b2_ref, gf_ref, wr_ref, br_ref,
     y_ref, ri_ref, rg_ref, hp_ref, x_scr, u_scr, c_scr, us_scr, cs_scr) = refs[3 * N_MOE_REFS:]
    i = pl.program_id(0)
    pos0 = (i * tm) % seq
    hl = CONV_HALO
    x_scr[0:hl, :] = _combined_rows(prev[0], prev[1], prev[2:])
    x_scr[hl:hl + tm, :] = _combined_rows(cur[0], cur[1], cur[2:])
    x_scr[hl + tm:hl + tm + hl, :] = _combined_rows(nxt[0], nxt[1], nxt[2:])
    halves = _row_halves(tm)
    first = hl - CONV_WIDTH // 2
    r_all = lax.broadcasted_iota(jnp.int32, (tm + 2 * hl, 1), 0)
    outside = ((r_all < hl) & (pos0 == 0)) | ((r_all >= hl + tm) & (pos0 + tm == seq))
    u_parts = [slice(0, tm // 2 + 2 * hl), slice(tm // 2 + 2 * hl, tm + 2 * hl)]
    gated = []
    for part in u_parts:
        h = _rms(x_scr[part, :], gm_ref[...]).astype(BF16)
        a = jnp.dot(h, w1_ref[...], preferred_element_type=F32) + b1_ref[...]
        gated.append(a)
    for part, a in zip(u_parts, gated):
        u = a[:, :D_MODEL] * jax.nn.sigmoid(a[:, D_MODEL:])
        u = jnp.where(outside[part, :], 0.0, u)
        for c in range(D_MODEL // LANES):
            u_scr[c, part, :] = u[:, c * LANES:(c + 1) * LANES]

    seg = (tm // 2) // SUBLANES
    seg_tiles = seg + 2 * hl
    group = SUBLANES
    logits = []
    for rows in halves:
        for c in range(D_MODEL // LANES):
            cols = slice(c * LANES, (c + 1) * LANES)
            for sgm in range(SUBLANES):
                r0 = rows.start + seg * sgm
                us_scr[c, pl.ds(sgm, seg_tiles, stride=SUBLANES), :] = u_scr[c, r0:r0 + seg_tiles, :]
            for k0 in range(0, seg, group):
                accs = [jnp.zeros((SUBLANES, LANES), F32) + bdw_ref[:, cols] for _ in range(group)]
                for t in range(CONV_WIDTH):
                    w_t = wdw_ref[t:t + 1, cols]
                    for j in range(group):
                        tile = (k0 + j + t + first) * SUBLANES
                        accs[j] = accs[j] + us_scr[c, tile:tile + SUBLANES, :] * w_t
                for j in range(group):
                    cs_scr[c, (k0 + j) * SUBLANES:(k0 + j + 1) * SUBLANES, :] = accs[j]
            for sgm in range(SUBLANES):
                r0 = rows.start + seg * sgm
                c_scr[c, r0:r0 + seg, :] = cs_scr[c, pl.ds(sgm, seg, stride=SUBLANES), :]
        cv = jnp.concatenate([c_scr[c, rows, :] for c in range(D_MODEL // LANES)], axis=1)
        mu = jnp.mean(cv, axis=-1, keepdims=True)
        xc = cv - mu
        var = jnp.mean(xc * xc, axis=-1, keepdims=True)
        ln = xc * lax.rsqrt(var + EPS) * lng_ref[...] + lnb_ref[...]
        act = (ln * jax.nn.sigmoid(ln)).astype(BF16)
        y = (x_scr[hl + rows.start:hl + rows.stop, :]
             + jnp.dot(act, w2_ref[...], preferred_element_type=F32) + b2_ref[...])
        y_ref[rows, :] = y
        logits.append(_route_logits(y, gf_ref[...], wr_ref, br_ref, hp_ref, rows))
    for rows, lg in zip(halves, logits):
        _route_decide(lg, ri_ref, rg_ref, rows)


def _conv_call(y2d, rg, rows, seq, gm, w1, b1, wdw, bdw, lng, lnb, w2, b2, gf, wr, br):
    n = y2d.shape[0]
    tm = ROW_TILE
    hl = CONV_HALO
    per = tm // hl
    tiles = n // tm
    const = lambda i: (0, 0)
    cur_blk = lambda i: i
    prev_blk = lambda i: jnp.maximum(i * per - 1, 0)
    next_blk = lambda i: jnp.minimum((i + 1) * per, n // hl - 1)

    def moe_specs(rows_blk, blk_of, blocks_per_piece):
        at = lambda off: (lambda i: (blk_of(i) + off, 0))
        return ([pl.BlockSpec((rows_blk, D_MODEL), at(0)), pl.BlockSpec((rows_blk, LANES), at(0))]
                + [pl.BlockSpec((rows_blk, SC_ROW_WORDS), at(j * blocks_per_piece))
                   for j in range(SC_ROW_SPLIT * TOP_K)])

    moe_args = [y2d, rg] + [rows] * (SC_ROW_SPLIT * TOP_K)
    out_specs, out_shape = _route_out_specs(n, tm)
    return pl.pallas_call(
        functools.partial(_conv_kernel, seq=seq, tm=tm),
        grid=(tiles,),
        in_specs=(
            moe_specs(tm, cur_blk, tiles) + moe_specs(hl, prev_blk, tiles * per)
            + moe_specs(hl, next_blk, tiles * per)
            + [
                pl.BlockSpec((1, D_MODEL), const),
                pl.BlockSpec((D_MODEL, 2 * D_MODEL), const),
                pl.BlockSpec((1, 2 * D_MODEL), const),
                pl.BlockSpec((CONV_WIDTH, D_MODEL), const),
                pl.BlockSpec((1, D_MODEL), const),
                pl.BlockSpec((1, D_MODEL), const),
                pl.BlockSpec((1, D_MODEL), const),
                pl.BlockSpec((D_MODEL, D_MODEL), const),
                pl.BlockSpec((1, D_MODEL), const),
                pl.BlockSpec((1, D_MODEL), const),
                pl.BlockSpec((D_MODEL, 2 * LANES), const),
                pl.BlockSpec((1, LANES), const),
            ]
        ),
        out_specs=out_specs,
        out_shape=out_shape,
        scratch_shapes=[
            pltpu.VMEM((tm + 2 * hl, D_MODEL), F32),
            pltpu.VMEM((D_MODEL // LANES, tm + 2 * hl, LANES), F32),
            pltpu.VMEM((D_MODEL // LANES, tm, LANES), F32),
            pltpu.VMEM((D_MODEL // LANES, (tm // 2 // SUBLANES + 2 * hl) * SUBLANES, LANES), F32),
            pltpu.VMEM((D_MODEL // LANES, tm // 2, LANES), F32),
        ],
        compiler_params=_cparams(("parallel",)),
        name="conv_module_router",
    )(*moe_args, *moe_args, *moe_args, gm, w1, b1, wdw, bdw, lng, lnb, w2, b2, gf, wr, br)


def _sc_mesh():
    return plsc.VectorSubcoreMesh(core_axis_name="core", subcore_axis_name="subcore")


def _sc_scatter_rows(src, idx, n_src_windows, out_rows):
    m = idx.shape[1]
    width = src.shape[1]
    nsw = n_src_windows

    @functools.partial(pl.kernel, out_type=jax.ShapeDtypeStruct((out_rows, width), src.dtype),
                       mesh=_sc_mesh(), name="sc_scatter_rows")
    def scatter(src_hbm, idx_hbm, out_hbm):
        def body(src_vmem, idx_vmem):
            pltpu.sync_copy(src_vmem, out_hbm.at[idx_vmem.at[0]])

        pltpu.emit_pipeline(
            body,
            grid=(m // SC_WINDOW,),
            in_specs=[
                pl.BlockSpec((SC_WINDOW, width), lambda w: ((w // (TOP_K * nsw)) * nsw + w % nsw, 0)),
                pl.BlockSpec((1, SC_WINDOW), lambda w: (0, w)),
            ],
            out_specs=[],
            core_axis_name=("core", "subcore"),
            dimension_semantics=(pltpu.PARALLEL,),
        )(src_hbm, idx_hbm)

    return scatter(src, idx)


def _sc_gather_rows(table, idx):
    m = idx.shape[1]
    width = table.shape[1]

    @functools.partial(pl.kernel, out_type=jax.ShapeDtypeStruct((m, width), table.dtype),
                       mesh=_sc_mesh(), name="sc_gather_rows")
    def gather(table_hbm, idx_hbm, out_hbm):
        def body(idx_vmem, out_vmem):
            pltpu.sync_copy(table_hbm.at[idx_vmem.at[0]], out_vmem)

        pltpu.emit_pipeline(
            body,
            grid=(m // SC_WINDOW,),
            in_specs=[pl.BlockSpec((1, SC_WINDOW), lambda w: (0, w))],
            out_specs=[pl.BlockSpec((SC_WINDOW, width), lambda w: (w, 0))],
            core_axis_name=("core", "subcore"),
            dimension_semantics=(pltpu.PARALLEL,),
        )(idx_hbm, out_hbm)

    return gather(table, idx)


def _expert_kernel(be_ref, nv_ref, xs_ref, wg_ref, wu_ref, wd_ref, yb_ref, wg_s, wu_s, wd_s):
    i = pl.program_id(0)

    @pl.when((i == 0) | (be_ref[i] != be_ref[jnp.maximum(i - 1, 0)]))
    def _():
        wg_s[...] = wg_ref[0, 0].astype(BF16)
        wu_s[...] = wu_ref[0, 0].astype(BF16)
        wd_s[...] = wd_ref[0, 0].astype(BF16)

    @pl.when(nv_ref[i] > 0)
    def _():
        words = jnp.concatenate([xs_ref[c] for c in range(SC_ROW_SPLIT)], axis=1)
        row = lax.broadcasted_iota(jnp.int32, (words.shape[0], 1), 0)
        words = jnp.where(row < nv_ref[i], words, jnp.uint32(0))
        hi, lo = _unpack_rows(words)
        xb = jnp.concatenate([hi.astype(BF16), lo.astype(BF16)], axis=1)
        chunks = [slice(c, c + MXU_WIDTH) for c in range(0, D_EXPERT, MXU_WIDTH)]
        gu = [(jnp.dot(xb, wg_s[:, c], preferred_element_type=F32),
               jnp.dot(xb, wu_s[:, c], preferred_element_type=F32)) for c in chunks]
        y = None
        for c, (g, u) in zip(chunks, gu):
            hid = (g * jax.nn.sigmoid(g) * u).astype(BF16)
            part = jnp.dot(hid, wd_s[c, :], preferred_element_type=F32)
            y = part if y is None else y + part
        packed = _pack_rows(y)
        for c in range(SC_ROW_SPLIT):
            yb_ref[c] = packed[:, c * SC_ROW_WORDS:(c + 1) * SC_ROW_WORDS]


def _expert_call(block_expert, block_valid, xs, layer, w_gate, w_up, w_down):
    p_total = xs.shape[1]
    blk = MOE_BLOCK
    wmap = lambda i, be, nv: (layer, be[i], 0, 0)
    rows = lambda i, be, nv: (0, i, 0)
    return pl.pallas_call(
        _expert_kernel,
        grid_spec=pltpu.PrefetchScalarGridSpec(
            num_scalar_prefetch=2,
            grid=(p_total // blk,),
            in_specs=[
                pl.BlockSpec((SC_ROW_SPLIT, blk, SC_ROW_WORDS), rows),
                pl.BlockSpec((1, 1, D_MODEL, D_EXPERT), wmap),
                pl.BlockSpec((1, 1, D_MODEL, D_EXPERT), wmap),
                pl.BlockSpec((1, 1, D_EXPERT, D_MODEL), wmap),
            ],
            out_specs=pl.BlockSpec((SC_ROW_SPLIT, blk, SC_ROW_WORDS), rows),
            scratch_shapes=[
                pltpu.VMEM((D_MODEL, D_EXPERT), BF16),
                pltpu.VMEM((D_MODEL, D_EXPERT), BF16),
                pltpu.VMEM((D_EXPERT, D_MODEL), BF16),
            ],
        ),
        out_shape=jax.ShapeDtypeStruct((SC_ROW_SPLIT, p_total, SC_ROW_WORDS), jnp.uint32),
        compiler_params=_cparams(("arbitrary",)),
        name="moe_experts",
    )(block_expert, block_valid, xs, w_gate, w_up, w_down)


def _combined_rows(y_ref, rg_ref, piece_refs):
    rg = rg_ref[...]
    his, los = [], []
    for c in range(SC_ROW_SPLIT):
        hi0, lo0 = _unpack_rows(piece_refs[c * TOP_K][...])
        hi1, lo1 = _unpack_rows(piece_refs[c * TOP_K + 1][...])
        his.append(rg[:, 0:1] * hi0 + rg[:, 1:2] * hi1)
        los.append(rg[:, 0:1] * lo0 + rg[:, 1:2] * lo1)
    return y_ref[...] + jnp.concatenate(his + los, axis=1)


def _combine_kernel(y_ref, rg_ref, *refs):
    refs[-1][...] = _combined_rows(y_ref, rg_ref, refs[:-1])


def _combine_call(y2d, rg, rows):
    n = y2d.shape[0]
    tm = ROW_TILE
    tiles = n // tm
    return pl.pallas_call(
        _combine_kernel,
        grid=(tiles,),
        in_specs=[
            pl.BlockSpec((tm, D_MODEL), lambda i: (i, 0)),
            pl.BlockSpec((tm, LANES), lambda i: (i, 0)),
        ] + [
            pl.BlockSpec((tm, SC_ROW_WORDS), functools.partial(lambda i, j: (i + j * tiles, 0), j=j))
            for j in range(SC_ROW_SPLIT * TOP_K)
        ],
        out_specs=pl.BlockSpec((tm, D_MODEL), lambda i: (i, 0)),
        out_shape=jax.ShapeDtypeStruct((n, D_MODEL), F32),
        compiler_params=_cparams(("parallel",)),
        name="moe_combine",
    )(y2d, rg, *([rows] * (SC_ROW_SPLIT * TOP_K)))


def _slot_tables(ri, n):
    e = ri[:TOP_K]
    experts = jnp.arange(N_EXPERTS, dtype=jnp.int32)[None, :, None]
    onehot = (e[:, None, :] == experts).astype(jnp.int32)
    tok_cnt = onehot.sum(axis=0)
    t = 256
    cnt_t = tok_cnt.reshape(N_EXPERTS, n // t, t)
    tri = (jnp.arange(t)[:, None] < jnp.arange(t)[None, :]).astype(F32)
    within = jnp.einsum("ens,st->ent", cnt_t.astype(F32), tri).astype(jnp.int32)
    tile_sum = cnt_t.sum(axis=2)
    tile_base = jnp.cumsum(tile_sum, axis=1) - tile_sum
    rank = (within + tile_base[:, :, None]).reshape(N_EXPERTS, n)
    counts = tile_sum.sum(axis=1)
    padded = ((counts + MOE_BLOCK - 1) // MOE_BLOCK) * MOE_BLOCK
    pad_end = jnp.cumsum(padded)
    pad_start = pad_end - padded
    slot = ((rank + pad_start[:, None])[None, :, :] * onehot).sum(axis=1)
    n_blocks = (n * TOP_K) // MOE_BLOCK + N_EXPERTS
    block_start = jnp.arange(n_blocks, dtype=jnp.int32) * MOE_BLOCK
    block_expert = jnp.minimum((pad_end[None, :] <= block_start[:, None]).sum(axis=-1),
                               N_EXPERTS - 1).astype(jnp.int32)
    owner = block_expert[:, None] == jnp.arange(N_EXPERTS, dtype=jnp.int32)[None, :]
    used_end = jnp.where(owner, (pad_start + counts)[None, :], 0).sum(axis=-1)
    block_valid = jnp.clip(used_end - block_start, 0, MOE_BLOCK).astype(jnp.int32)
    p_total = n_blocks * MOE_BLOCK
    slot_km = slot.reshape(1, TOP_K * n).astype(jnp.int32)
    idx = jnp.concatenate([slot_km + c * p_total for c in range(SC_ROW_SPLIT)], axis=1)
    return idx, block_expert, block_valid, p_total


def _after(value, other):
    if other is None:
        return value
    return lax.optimization_barrier((value, other))[0]


def _moe_rows(n, ri, hp, layer, experts, after=None):
    idx, block_expert, block_valid, p_total = _slot_tables(ri, n)
    xs = _sc_scatter_rows(hp.reshape(SC_ROW_SPLIT * n, SC_ROW_WORDS), idx, n // SC_WINDOW,
                          SC_ROW_SPLIT * p_total)
    yb = _expert_call(block_expert, _after(block_valid, after),
                      xs.reshape(SC_ROW_SPLIT, p_total, SC_ROW_WORDS), layer, *experts)
    return _sc_gather_rows(yb.reshape(SC_ROW_SPLIT * p_total, SC_ROW_WORDS), idx), yb


def _rope_tables(seq):
    pos = np.arange(seq, dtype=np.float32)
    inv_freq = np.float32(ROPE_THETA) ** (-np.arange(0, ROPE_DIM, 2, dtype=np.float32) / np.float32(ROPE_DIM))
    ang = pos[:, None] * inv_freq[None, :].astype(np.float32)
    cos = np.cos(ang).astype(np.float32)
    sin = np.sin(ang).astype(np.float32)
    half = ROPE_DIM // 2
    rest = HEAD_DIM - ROPE_DIM
    zeros = lambda w: np.zeros((seq, w), np.float32)
    cos_h = np.concatenate([cos, cos, np.ones((seq, rest), np.float32)], axis=1)
    sa_h = np.concatenate([-sin, zeros(half + rest)], axis=1)
    sb_h = np.concatenate([zeros(half), sin, zeros(rest)], axis=1)
    pair = lambda t: jnp.asarray(np.tile(t, (1, LANES // HEAD_DIM)))
    return pair(cos_h), pair(sa_h), pair(sb_h)


def _router_weights(w_rg, b_rg, w_re, b_re):
    w = jnp.concatenate([w_rg, w_re], axis=1)
    w = jnp.pad(w, ((0, 0), (0, LANES - w.shape[1])))
    w_hi = w.astype(BF16)
    w_lo = (w - w_hi.astype(F32)).astype(BF16)
    b = jnp.pad(jnp.concatenate([b_rg, b_re]), (0, LANES - N_EXPERT_GROUPS - N_EXPERTS))
    return jnp.concatenate([w_hi, w_lo], axis=1), b.reshape(1, LANES)


def _attention_layer(x, p):
    batch, seq, d = x.shape
    x2d = x.reshape(batch * seq, d)
    cos, sa, sb = _rope_tables(seq)
    qkv = _qkv_call(x2d, seq, p["ln_mix0"], p["w_qkv"], p["mseg"], p["qg"], p["kg"], cos, sa, sb)
    os_, lses = [], []
    for g, dil in enumerate(DILATIONS):
        o, lse = _attn_call(qkv[g], qkv[3 + g], qkv[6 + g], batch, seq, dil)
        os_.append(o)
        lses.append(lse)
    return _attn_out_call(os_, lses, x2d, p["w_o"], p["expand"], p["ln_ffn0"], p["wr0"], p["br0"])


def _conv_layer(state, rows, seq, p, after=None):
    y, _, rg, _ = state
    return _conv_call(y, _after(rg, after), rows, seq, p["ln_mix1"], p["w_pw1"], p["b_pw1"], p["w_dw"],
                      p["b_dw"], p["conv_ln_g"], p["conv_ln_b"], p["w_pw2"], p["b_pw2"],
                      p["ln_ffn1"], p["wr1"], p["br1"])


def _experts_of(state, layer, p, after=None):
    y, ri, _, hp = state
    return _moe_rows(y.shape[0], ri, hp, layer, p["experts"], after)


def _encoder(xs, p):
    small, big = sorted(range(len(xs)), key=lambda i: xs[i].shape[0] * xs[i].shape[1])
    seq = {i: xs[i].shape[1] for i in (small, big)}
    state = {i: _attention_layer(xs[i], p) for i in (big, small)}
    rows, yb = {}, {}
    rows[big], yb[big] = _experts_of(state[big], 0, p, after=state[small][1])
    rows[small], yb[small] = _experts_of(state[small], 0, p)
    state[big] = _conv_layer(state[big], rows[big], seq[big], p, after=yb[small])
    state[small] = _conv_layer(state[small], rows[small], seq[small], p)
    rows[big], _ = _experts_of(state[big], 1, p, after=state[small][1])
    rows[small], _ = _experts_of(state[small], 1, p)
    outs = {i: _combine_call(state[i][0], state[i][2], rows[i]).reshape(xs[i].shape) for i in (big, small)}
    return tuple(outs[i] for i in range(len(xs)))


def kernel(x_prompt, x_sample, ln_mix, ln_ffn, w_qkv, q_gain, k_gain, w_o, w_pw1, b_pw1, w_dw, b_dw,
           conv_ln_g, conv_ln_b, w_pw2, b_pw2, w_router_group, b_router_group, w_router_expert,
           b_router_expert, w_gate, w_up, w_down):
    row = lambda v: v.reshape(1, -1).astype(F32)
    n_heads = N_GROUPS * HEADS_PER_GROUP
    head_id = np.arange(ATTN_WIDTH) // HEAD_DIM
    mseg = jnp.asarray((head_id[:, None] == head_id[None, :]) / HEAD_DIM, BF16)
    slot_id = np.arange(GROUP_WIDTH) // HEAD_DIM
    expand = jnp.asarray(np.arange(LANES)[:, None] == slot_id[None, :], BF16)
    wr0, br0 = _router_weights(w_router_group[0], b_router_group[0], w_router_expert[0], b_router_expert[0])
    wr1, br1 = _router_weights(w_router_group[1], b_router_group[1], w_router_expert[1], b_router_expert[1])
    p = dict(
        ln_mix0=row(ln_mix[0]), ln_mix1=row(ln_mix[1]), ln_ffn0=row(ln_ffn[0]), ln_ffn1=row(ln_ffn[1]),
        w_qkv=w_qkv[0].astype(BF16), mseg=mseg, expand=expand,
        qg=row(jnp.tile(q_gain[0], n_heads) * (HEAD_DIM ** -0.5 * LOG2E)), kg=row(jnp.tile(k_gain[0], n_heads)),
        w_o=w_o[0].astype(BF16),
        w_pw1=w_pw1[0].astype(BF16), b_pw1=row(b_pw1[0]), w_dw=w_dw[0].astype(F32), b_dw=row(b_dw[0]),
        conv_ln_g=row(conv_ln_g[0]), conv_ln_b=row(conv_ln_b[0]),
        w_pw2=w_pw2[0].astype(BF16), b_pw2=row(b_pw2[0]),
        wr0=wr0, br0=br0, wr1=wr1, br1=br1,
        experts=(w_gate, w_up, w_down),
    )
    return _encoder((x_prompt, x_sample), p)
```

```python
import functools

import jax
import jax.numpy as jnp
import numpy as np
from jax import lax
from jax.experimental import pallas as pl
from jax.experimental.pallas import tpu as pltpu
from jax.experimental.pallas import tpu_sc as plsc

D_MODEL = 1024
HEAD_DIM = 64
HEADS_PER_GROUP = 4
DILATIONS = (1, 4, 16)
HALF_WINDOW = 64
N_GROUPS = len(DILATIONS)
GROUP_WIDTH = HEADS_PER_GROUP * HEAD_DIM
ATTN_WIDTH = N_GROUPS * GROUP_WIDTH
ROPE_DIM = HEAD_DIM // 4
ROPE_THETA = 500000.0
CONV_WIDTH = 31
CONV_HALO = 16
N_EXPERT_GROUPS = 4
EXPERTS_PER_GROUP = 8
N_EXPERTS = N_EXPERT_GROUPS * EXPERTS_PER_GROUP
TOP_K = 2
D_EXPERT = 512
EPS = 1e-6
NEG_INF = -1e30
LOG2E = 1.4426950408889634
LN2 = 0.6931471805599453

LANES = 128
SUBLANES = 8
MXU_WIDTH = 256
ROW_TILE = 512
ATTN_Q_TILE = 128
ATTN_STEP_ROWS = 512
MOE_BLOCK = 512
VMEM_LIMIT = 56 * 1024 * 1024

PACKED_WIDTH = D_MODEL // 2
SC_WINDOW = 128
SC_ROW_SPLIT = 2
SC_ROW_WORDS = PACKED_WIDTH // SC_ROW_SPLIT

F32 = jnp.float32
BF16 = jnp.bfloat16


def _cparams(sem):
    return pltpu.CompilerParams(dimension_semantics=sem, vmem_limit_bytes=VMEM_LIMIT)


def _rms(x, gain):
    return x * lax.rsqrt(jnp.mean(x * x, axis=-1, keepdims=True) + EPS) * gain


def _pack_rows(x):
    bits = pltpu.bitcast(x.astype(BF16).astype(F32), jnp.uint32)
    half = x.shape[1] // 2
    return bits[:, :half] | (bits[:, half:] >> 16)


def _unpack_rows(w):
    hi = pltpu.bitcast(w & jnp.uint32(0xFFFF0000), F32)
    lo = pltpu.bitcast(w << 16, F32)
    return hi, lo


def _store_by_residue(out_ref, val, dil, scr, row0=0):
    rows, width = val.shape
    out_rows = slice(row0 // dil, (row0 + rows) // dil)
    if dil == 1:
        out_ref[out_rows, :] = val.astype(out_ref.dtype)
        return
    for c in range(width // LANES):
        scr[c] = val[:, c * LANES:(c + 1) * LANES]
    for r in range(dil):
        for c in range(width // LANES):
            lanes = slice(r * width + c * LANES, r * width + (c + 1) * LANES)
            out_ref[out_rows, lanes] = scr[c, pl.ds(r, rows // dil, stride=dil), :].astype(out_ref.dtype)


def _qkv_kernel(x_ref, g_ref, w_ref, mseg_ref, qg_ref, kg_ref, cos_ref, sa_ref, sb_ref,
                q0, q1, q2, k0, k1, k2, v0, v1, v2, *scrs):
    scrs = iter(scrs)

    def store(out_ref, val, g, row0):
        _store_by_residue(out_ref, val, DILATIONS[g], next(scrs) if DILATIONS[g] > 1 else None, row0)

    for rows in _row_halves(x_ref.shape[0]):
        h = _rms(x_ref[rows, :], g_ref[...]).astype(BF16)
        q = jnp.dot(h, w_ref[:, 0:ATTN_WIDTH], preferred_element_type=F32)
        k = jnp.dot(h, w_ref[:, ATTN_WIDTH:2 * ATTN_WIDTH], preferred_element_type=F32)
        v = jnp.dot(h, w_ref[:, 2 * ATTN_WIDTH:3 * ATTN_WIDTH], preferred_element_type=F32)
        ms_q = jnp.dot((q * q).astype(BF16), mseg_ref[...], preferred_element_type=F32)
        ms_k = jnp.dot((k * k).astype(BF16), mseg_ref[...], preferred_element_type=F32)
        for g, o in enumerate((v0, v1, v2)):
            store(o, v[:, g * GROUP_WIDTH:(g + 1) * GROUP_WIDTH], g, rows.start)
        widen = lambda t: jnp.concatenate([t] * (GROUP_WIDTH // LANES), axis=1)
        cos = widen(cos_ref[rows, :])
        sa = widen(sa_ref[rows, :])
        sb = widen(sb_ref[rows, :])
        for t, ms, gain, outs in ((q, ms_q, qg_ref[...], (q0, q1, q2)), (k, ms_k, kg_ref[...], (k0, k1, k2))):
            tn = t * lax.rsqrt(ms + EPS) * gain
            for g in range(N_GROUPS):
                c = tn[:, g * GROUP_WIDTH:(g + 1) * GROUP_WIDTH]
                r = (c * cos + pltpu.roll(c, GROUP_WIDTH - ROPE_DIM // 2, 1) * sa
                     + pltpu.roll(c, ROPE_DIM // 2, 1) * sb)
                store(outs[g], r, g, rows.start)


def _qkv_call(x2d, seq, gain, w_qkv, mseg, qg, kg, cos, sa, sb):
    n = x2d.shape[0]
    tm = ROW_TILE
    tiles_per_seq = seq // tm
    row = lambda i: (i, 0)
    const = lambda i: (0, 0)
    tab = lambda i: (i % tiles_per_seq, 0)
    out_specs = [pl.BlockSpec((tm // dil, dil * GROUP_WIDTH), row) for dil in DILATIONS] * 3
    out_shape = [jax.ShapeDtypeStruct((n // dil, dil * GROUP_WIDTH), BF16) for dil in DILATIONS] * 3
    return pl.pallas_call(
        _qkv_kernel,
        grid=(n // tm,),
        in_specs=[
            pl.BlockSpec((tm, D_MODEL), row),
            pl.BlockSpec((1, D_MODEL), const),
            pl.BlockSpec((D_MODEL, 3 * ATTN_WIDTH), const),
            pl.BlockSpec((ATTN_WIDTH, ATTN_WIDTH), const),
            pl.BlockSpec((1, ATTN_WIDTH), const),
            pl.BlockSpec((1, ATTN_WIDTH), const),
            pl.BlockSpec((tm, LANES), tab),
            pl.BlockSpec((tm, LANES), tab),
            pl.BlockSpec((tm, LANES), tab),
        ],
        out_specs=out_specs,
        out_shape=out_shape,
        scratch_shapes=[pltpu.VMEM((GROUP_WIDTH // LANES, tm // 2, LANES), F32)]
        * (2 * 3 * sum(dil > 1 for dil in DILATIONS)),
        compiler_params=_cparams(("parallel",)),
        name="qkv_proj",
    )(x2d, gain, w_qkv, mseg, qg, kg, cos, sa, sb)


def _attn_kernel(q_ref, k_ref, v_ref, o_ref, lse_ref, *, ls, rows, res, win):
    i = pl.program_id(2)
    tq = ATTN_Q_TILE
    rel = (lax.broadcasted_iota(jnp.int32, (tq, win), 0) - lax.broadcasted_iota(jnp.int32, (tq, win), 1))
    lane = lax.broadcasted_iota(jnp.int32, (tq, LANES), 1)
    lane_kv = lax.broadcasted_iota(jnp.int32, (win, LANES), 1)

    blocks = []
    for r in range(res):
        for sb in range(rows // tq):
            q_start = i * rows + sb * tq
            k_start = pl.multiple_of(jnp.clip(q_start - HALF_WINDOW, 0, ls - win), HALF_WINDOW)
            mask = jnp.abs(rel + (q_start - k_start)) <= HALF_WINDOW
            blocks.append((r, slice(sb * tq, (sb + 1) * tq), k_start, mask))
    chains = [(blk, h) for blk in blocks for h in range(HEADS_PER_GROUP)]

    def cols(r, h):
        return slice(r * GROUP_WIDTH + h * HEAD_DIM, r * GROUP_WIDTH + (h + 1) * HEAD_DIM)

    scores = [lax.dot_general(q_ref[0, q_rows, cols(r, h)], k_ref[0, pl.ds(k_start, win), cols(r, h)],
                              (((1,), (1,)), ((), ())), preferred_element_type=F32)
              for (r, q_rows, k_start, _), h in chains]
    maxes, probs = [], []
    for ((_, _, _, mask), _), s in zip(chains, scores):
        s = jnp.where(mask, s, NEG_INF)
        m = jnp.max(s, axis=-1, keepdims=True)
        maxes.append(m)
        probs.append(jnp.exp2(s - m).astype(BF16))
    outs = []
    for ((r, _, k_start, _), h), p in zip(chains, probs):
        pair = slice(r * GROUP_WIDTH + (h // 2) * LANES, r * GROUP_WIDTH + (h // 2 + 1) * LANES)
        own_half = (lane_kv < HEAD_DIM) if h % 2 == 0 else (lane_kv >= HEAD_DIM)
        v_ext = jnp.where(own_half, v_ref[0, pl.ds(k_start, win), pair], jnp.ones((), BF16))
        outs.append(jnp.dot(p, v_ext, preferred_element_type=F32))
    lse_tiles = {}
    for ((r, q_rows, _, _), h), o_ext, m in zip(chains, outs, maxes):
        den_ext = pltpu.roll(o_ext, HEAD_DIM, 1)
        own = slice((h % 2) * HEAD_DIM, (h % 2 + 1) * HEAD_DIM)
        o_ref[0, q_rows, cols(r, h)] = (o_ext / den_ext)[:, own].astype(BF16)
        den_at_h = den_ext if h % 2 == 0 else o_ext
        key = (r, q_rows.start)
        lse_tiles[key] = jnp.where(lane == h, (m + jnp.log2(den_at_h)) * LN2,
                                   lse_tiles.get(key, jnp.zeros((tq, LANES), F32)))
    for (r, q0), tile in lse_tiles.items():
        lse_ref[0, q0:q0 + tq, r * LANES:(r + 1) * LANES] = tile


def _attn_call(q, k, v, batch, seq, dil):
    ls = seq // dil
    rows = min(ATTN_STEP_ROWS, ls)
    res = max(1, min(dil, ATTN_STEP_ROWS // rows))
    win = min(ATTN_Q_TILE + 2 * HALF_WINDOW, ls)
    view = lambda t: t.reshape(batch, ls, dil * GROUP_WIDTH)
    qmap = lambda b, r, i: (b, i, r)
    kvmap = lambda b, r, i: (b, 0, r)
    o, lse = pl.pallas_call(
        functools.partial(_attn_kernel, ls=ls, rows=rows, res=res, win=win),
        grid=(batch, dil // res, ls // rows),
        in_specs=[
            pl.BlockSpec((1, rows, res * GROUP_WIDTH), qmap),
            pl.BlockSpec((1, ls, res * GROUP_WIDTH), kvmap),
            pl.BlockSpec((1, ls, res * GROUP_WIDTH), kvmap),
        ],
        out_specs=[
            pl.BlockSpec((1, rows, res * GROUP_WIDTH), qmap),
            pl.BlockSpec((1, rows, res * LANES), qmap),
        ],
        out_shape=[
            jax.ShapeDtypeStruct((batch, ls, dil * GROUP_WIDTH), BF16),
            jax.ShapeDtypeStruct((batch, ls, dil * LANES), F32),
        ],
        compiler_params=_cparams(("parallel", "parallel", "arbitrary")),
        name=f"banded_attn_d{dil}",
    )(view(q), view(k), view(v))
    return o.reshape(batch * ls, dil * GROUP_WIDTH), lse.reshape(batch * ls, dil * LANES)


def _route_logits(x, gain, wr_ref, br_ref, hp_ref, rows):
    h = _rms(x, gain)
    packed = _pack_rows(h)
    for c in range(SC_ROW_SPLIT):
        hp_ref[c, rows, :] = packed[:, c * SC_ROW_WORDS:(c + 1) * SC_ROW_WORDS]
    h_hi = h.astype(BF16)
    h_lo = (h - h_hi.astype(F32)).astype(BF16)
    acc = (jnp.dot(h_hi, wr_ref[...], preferred_element_type=F32)
           + jnp.dot(h_lo, wr_ref[...], preferred_element_type=F32))
    return acc[:, :LANES] + acc[:, LANES:] + br_ref[...]


def _route_decide(logits, ri_ref, rg_ref, rows):
    lane = lax.broadcasted_iota(jnp.int32, logits.shape, 1)
    lane_f = lane.astype(F32)
    big = jnp.float32(2 * LANES)

    is_g = lane < N_EXPERT_GROUPS
    gl = jnp.where(is_g, logits, NEG_INF)
    gmax = jnp.max(gl, axis=-1, keepdims=True)
    grp = jnp.min(jnp.where(gl == gmax, lane_f, big), axis=-1, keepdims=True)
    p_grp = 1.0 / jnp.sum(jnp.where(is_g, jnp.exp(gl - gmax), 0.0), axis=-1, keepdims=True)

    lo = N_EXPERT_GROUPS + grp * EXPERTS_PER_GROUP
    in_grp = (lane_f >= lo) & (lane_f < lo + EXPERTS_PER_GROUP)
    el = jnp.where(in_grp, logits, NEG_INF)
    v1 = jnp.max(el, axis=-1, keepdims=True)
    i1 = jnp.min(jnp.where(el == v1, lane_f, big), axis=-1, keepdims=True)
    el2 = jnp.where(lane_f == i1, NEG_INF, el)
    v2 = jnp.max(el2, axis=-1, keepdims=True)
    i2 = jnp.min(jnp.where(el2 == v2, lane_f, big), axis=-1, keepdims=True)
    e21 = jnp.exp(v2 - v1)
    g1 = p_grp / (1.0 + e21)
    g2 = g1 * e21
    ri = jnp.where(lane == 0, i1 - N_EXPERT_GROUPS, jnp.where(lane == 1, i2 - N_EXPERT_GROUPS, 0.0))
    ri_ref[:, rows] = jnp.transpose(ri)[:SUBLANES, :].astype(jnp.int32)
    rg_ref[rows, :] = jnp.where(lane == 0, g1, jnp.where(lane == 1, g2, 0.0))


def _row_halves(tm):
    return [slice(0, tm // 2), slice(tm // 2, tm)]


def _route_out_specs(n, tm):
    row = lambda i: (i, 0)
    specs = [
        pl.BlockSpec((tm, D_MODEL), row),
        pl.BlockSpec((SUBLANES, tm), lambda i: (0, i)),
        pl.BlockSpec((tm, LANES), row),
        pl.BlockSpec((SC_ROW_SPLIT, tm, SC_ROW_WORDS), lambda i: (0, i, 0)),
    ]
    shapes = [
        jax.ShapeDtypeStruct((n, D_MODEL), F32),
        jax.ShapeDtypeStruct((SUBLANES, n), jnp.int32),
        jax.ShapeDtypeStruct((n, LANES), F32),
        jax.ShapeDtypeStruct((SC_ROW_SPLIT, n, SC_ROW_WORDS), jnp.uint32),
    ]
    return specs, shapes


def _load_by_residue(in_ref, dil, scr):
    if dil == 1:
        return in_ref[...].astype(F32)
    chunks, rows, _ = scr.shape
    width = chunks * LANES
    for r in range(dil):
        for c in range(chunks):
            lanes = slice(r * width + c * LANES, r * width + (c + 1) * LANES)
            scr[c, pl.ds(r, rows // dil, stride=dil), :] = in_ref[:, lanes].astype(F32)
    return jnp.concatenate([scr[c] for c in range(chunks)], axis=1) if chunks > 1 else scr[0]


def _attn_out_kernel(o0, o1, o2, l0, l1, l2, x_ref, wo_ref, ex_ref, g_ref, wr_ref, br_ref,
                     y_ref, ri_ref, rg_ref, hp_ref, o_scr1, o_scr2, l_scr1, l_scr2):
    ls = [_load_by_residue(l, dil, scr)
          for l, dil, scr in zip((l0, l1, l2), DILATIONS, (None, l_scr1, l_scr2))]
    m = jnp.maximum(jnp.maximum(ls[0], ls[1]), ls[2])
    es = [jnp.exp(l - m) for l in ls]
    inv = 1.0 / (es[0] + es[1] + es[2])
    a_wide = []
    for g in range(N_GROUPS):
        alpha = es[g] * inv
        a_hi = alpha.astype(BF16)
        a_lo = (alpha - a_hi.astype(F32)).astype(BF16)
        a_wide.append(jnp.dot(a_hi, ex_ref[...], preferred_element_type=F32)
                      + jnp.dot(a_lo, ex_ref[...], preferred_element_type=F32))
    scaled = jnp.concatenate(
        [(_load_by_residue(o_ref, dil, scr) * a).astype(BF16)
         for o_ref, dil, scr, a in zip((o0, o1, o2), DILATIONS, (None, o_scr1, o_scr2), a_wide)], axis=1)
    halves = _row_halves(x_ref.shape[0])
    ys = [x_ref[rows, :] + jnp.dot(scaled[rows, :], wo_ref[...], preferred_element_type=F32)
          for rows in halves]
    for rows, y in zip(halves, ys):
        y_ref[rows, :] = y
    logits = [_route_logits(y, g_ref[...], wr_ref, br_ref, hp_ref, rows) for rows, y in zip(halves, ys)]
    for rows, lg in zip(halves, logits):
        _route_decide(lg, ri_ref, rg_ref, rows)


def _attn_out_call(os_, lses, x2d, w_o, expand, gain, wr, br):
    n = x2d.shape[0]
    tm = ROW_TILE
    row = lambda i: (i, 0)
    const = lambda i: (0, 0)
    out_specs, out_shape = _route_out_specs(n, tm)
    return pl.pallas_call(
        _attn_out_kernel,
        grid=(n // tm,),
        in_specs=(
            [pl.BlockSpec((tm // dil, dil * GROUP_WIDTH), row) for dil in DILATIONS]
            + [pl.BlockSpec((tm // dil, dil * LANES), row) for dil in DILATIONS]
            + [
                pl.BlockSpec((tm, D_MODEL), row),
                pl.BlockSpec((ATTN_WIDTH, D_MODEL), const),
                pl.BlockSpec((LANES, GROUP_WIDTH), const),
                pl.BlockSpec((1, D_MODEL), const),
                pl.BlockSpec((D_MODEL, 2 * LANES), const),
                pl.BlockSpec((1, LANES), const),
            ]
        ),
        out_specs=out_specs,
        out_shape=out_shape,
        scratch_shapes=([pltpu.VMEM((GROUP_WIDTH // LANES, tm, LANES), F32)] * 2
                        + [pltpu.VMEM((1, tm, LANES), F32)] * 2),
        compiler_params=_cparams(("parallel",)),
        name="attn_out_router",
    )(*os_, *lses, x2d, w_o, expand, gain, wr, br)


N_MOE_REFS = 2 + SC_ROW_SPLIT * TOP_K


def _conv_kernel(*refs, seq, tm):
    cur, prev, nxt = (refs[b * N_MOE_REFS:(b + 1) * N_MOE_REFS] for b in range(3))
    (gm_ref, w1_ref, b1_ref, wdw_ref, bdw_ref, lng_ref, lnb_ref, w2_ref, b2_ref, gf_ref, wr_ref, br_ref,
     y_ref, ri_ref, rg_ref, hp_ref, x_scr, u_scr, c_scr, us_scr, cs_scr) = refs[3 * N_MOE_REFS:]
    i = pl.program_id(0)
    pos0 = (i * tm) % seq
    hl = CONV_HALO
    x_scr[0:hl, :] = _combined_rows(prev[0], prev[1], prev[2:])
    x_scr[hl:hl + tm, :] = _combined_rows(cur[0], cur[1], cur[2:])
    x_scr[hl + tm:hl + tm + hl, :] = _combined_rows(nxt[0], nxt[1], nxt[2:])
    halves = _row_halves(tm)
    first = hl - CONV_WIDTH // 2
    r_all = lax.broadcasted_iota(jnp.int32, (tm + 2 * hl, 1), 0)
    outside = ((r_all < hl) & (pos0 == 0)) | ((r_all >= hl + tm) & (pos0 + tm == seq))
    u_parts = [slice(0, tm // 2 + 2 * hl), slice(tm // 2 + 2 * hl, tm + 2 * hl)]
    gated = []
    for part in u_parts:
        h = _rms(x_scr[part, :], gm_ref[...]).astype(BF16)
        a = jnp.dot(h, w1_ref[...], preferred_element_type=F32) + b1_ref[...]
        gated.append(a)
    for part, a in zip(u_parts, gated):
        u = a[:, :D_MODEL] * jax.nn.sigmoid(a[:, D_MODEL:])
        u = jnp.where(outside[part, :], 0.0, u)
        for c in range(D_MODEL // LANES):
            u_scr[c, part, :] = u[:, c * LANES:(c + 1) * LANES]

    seg = (tm // 2) // SUBLANES
    seg_tiles = seg + 2 * hl
    group = SUBLANES
    logits = []
    for rows in halves:
        for c in range(D_MODEL // LANES):
            cols = slice(c * LANES, (c + 1) * LANES)
            for sgm in range(SUBLANES):
                r0 = rows.start + seg * sgm
                us_scr[c, pl.ds(sgm, seg_tiles, stride=SUBLANES), :] = u_scr[c, r0:r0 + seg_tiles, :]
            for k0 in range(0, seg, group):
                accs = [jnp.zeros((SUBLANES, LANES), F32) + bdw_ref[:, cols] for _ in range(group)]
                for t in range(CONV_WIDTH):
                    w_t = wdw_ref[t:t + 1, cols]
                    for j in range(group):
                        tile = (k0 + j + t + first) * SUBLANES
                        accs[j] = accs[j] + us_scr[c, tile:tile + SUBLANES, :] * w_t
                for j in range(group):
                    cs_scr[c, (k0 + j) * SUBLANES:(k0 + j + 1) * SUBLANES, :] = accs[j]
            for sgm in range(SUBLANES):
                r0 = rows.start + seg * sgm
                c_scr[c, r0:r0 + seg, :] = cs_scr[c, pl.ds(sgm, seg, stride=SUBLANES), :]
        cv = jnp.concatenate([c_scr[c, rows, :] for c in range(D_MODEL // LANES)], axis=1)
        mu = jnp.mean(cv, axis=-1, keepdims=True)
        xc = cv - mu
        var = jnp.mean(xc * xc, axis=-1, keepdims=True)
        ln = xc * lax.rsqrt(var + EPS) * lng_ref[...] + lnb_ref[...]
        act = (ln * jax.nn.sigmoid(ln)).astype(BF16)
        y = (x_scr[hl + rows.start:hl + rows.stop, :]
             + jnp.dot(act, w2_ref[...], preferred_element_type=F32) + b2_ref[...])
        y_ref[rows, :] = y
        logits.append(_route_logits(y, gf_ref[...], wr_ref, br_ref, hp_ref, rows))
    for rows, lg in zip(halves, logits):
        _route_decide(lg, ri_ref, rg_ref, rows)


def _conv_call(y2d, rg, rows, seq, gm, w1, b1, wdw, bdw, lng, lnb, w2, b2, gf, wr, br):
    n = y2d.shape[0]
    tm = ROW_TILE
    hl = CONV_HALO
    per = tm // hl
    tiles = n // tm
    const = lambda i: (0, 0)
    cur_blk = lambda i: i
    prev_blk = lambda i: jnp.maximum(i * per - 1, 0)
    next_blk = lambda i: jnp.minimum((i + 1) * per, n // hl - 1)

    def moe_specs(rows_blk, blk_of, blocks_per_piece):
        at = lambda off: (lambda i: (blk_of(i) + off, 0))
        return ([pl.BlockSpec((rows_blk, D_MODEL), at(0)), pl.BlockSpec((rows_blk, LANES), at(0))]
                + [pl.BlockSpec((rows_blk, SC_ROW_WORDS), at(j * blocks_per_piece))
                   for j in range(SC_ROW_SPLIT * TOP_K)])

    moe_args = [y2d, rg] + [rows] * (SC_ROW_SPLIT * TOP_K)
    out_specs, out_shape = _route_out_specs(n, tm)
    return pl.pallas_call(
        functools.partial(_conv_kernel, seq=seq, tm=tm),
        grid=(tiles,),
        in_specs=(
            moe_specs(tm, cur_blk, tiles) + moe_specs(hl, prev_blk, tiles * per)
            + moe_specs(hl, next_blk, tiles * per)
            + [
                pl.BlockSpec((1, D_MODEL), const),
                pl.BlockSpec((D_MODEL, 2 * D_MODEL), const),
                pl.BlockSpec((1, 2 * D_MODEL), const),
                pl.BlockSpec((CONV_WIDTH, D_MODEL), const),
                pl.BlockSpec((1, D_MODEL), const),
                pl.BlockSpec((1, D_MODEL), const),
                pl.BlockSpec((1, D_MODEL), const),
                pl.BlockSpec((D_MODEL, D_MODEL), const),
                pl.BlockSpec((1, D_MODEL), const),
                pl.BlockSpec((1, D_MODEL), const),
                pl.BlockSpec((D_MODEL, 2 * LANES), const),
                pl.BlockSpec((1, LANES), const),
            ]
        ),
        out_specs=out_specs,
        out_shape=out_shape,
        scratch_shapes=[
            pltpu.VMEM((tm + 2 * hl, D_MODEL), F32),
            pltpu.VMEM((D_MODEL // LANES, tm + 2 * hl, LANES), F32),
            pltpu.VMEM((D_MODEL // LANES, tm, LANES), F32),
            pltpu.VMEM((D_MODEL // LANES, (tm // 2 // SUBLANES + 2 * hl) * SUBLANES, LANES), F32),
            pltpu.VMEM((D_MODEL // LANES, tm // 2, LANES), F32),
        ],
        compiler_params=_cparams(("parallel",)),
        name="conv_module_router",
    )(*moe_args, *moe_args, *moe_args, gm, w1, b1, wdw, bdw, lng, lnb, w2, b2, gf, wr, br)


def _sc_mesh():
    return plsc.VectorSubcoreMesh(core_axis_name="core", subcore_axis_name="subcore")


def _sc_scatter_rows(src, idx, n_src_windows, out_rows):
    m = idx.shape[1]
    width = src.shape[1]
    nsw = n_src_windows

    @functools.partial(pl.kernel, out_type=jax.ShapeDtypeStruct((out_rows, width), src.dtype),
                       mesh=_sc_mesh(), name="sc_scatter_rows")
    def scatter(src_hbm, idx_hbm, out_hbm):
        def body(src_vmem, idx_vmem):
            pltpu.sync_copy(src_vmem, out_hbm.at[idx_vmem.at[0]])

        pltpu.emit_pipeline(
            body,
            grid=(m // SC_WINDOW,),
            in_specs=[
                pl.BlockSpec((SC_WINDOW, width), lambda w: ((w // (TOP_K * nsw)) * nsw + w % nsw, 0)),
                pl.BlockSpec((1, SC_WINDOW), lambda w: (0, w)),
            ],
            out_specs=[],
            core_axis_name=("core", "subcore"),
            dimension_semantics=(pltpu.PARALLEL,),
        )(src_hbm, idx_hbm)

    return scatter(src, idx)


def _sc_gather_rows(table, idx):
    m = idx.shape[1]
    width = table.shape[1]

    @functools.partial(pl.kernel, out_type=jax.ShapeDtypeStruct((m, width), table.dtype),
                       mesh=_sc_mesh(), name="sc_gather_rows")
    def gather(table_hbm, idx_hbm, out_hbm):
        def body(idx_vmem, out_vmem):
            pltpu.sync_copy(table_hbm.at[idx_vmem.at[0]], out_vmem)

        pltpu.emit_pipeline(
            body,
            grid=(m // SC_WINDOW,),
            in_specs=[pl.BlockSpec((1, SC_WINDOW), lambda w: (0, w))],
            out_specs=[pl.BlockSpec((SC_WINDOW, width), lambda w: (w, 0))],
            core_axis_name=("core", "subcore"),
            dimension_semantics=(pltpu.PARALLEL,),
        )(idx_hbm, out_hbm)

    return gather(table, idx)


def _expert_kernel(be_ref, nv_ref, xs_ref, wg_ref, wu_ref, wd_ref, yb_ref, wg_s, wu_s, wd_s):
    i = pl.program_id(0)

    @pl.when((i == 0) | (be_ref[i] != be_ref[jnp.maximum(i - 1, 0)]))
    def _():
        wg_s[...] = wg_ref[0, 0].astype(BF16)
        wu_s[...] = wu_ref[0, 0].astype(BF16)
        wd_s[...] = wd_ref[0, 0].astype(BF16)

    @pl.when(nv_ref[i] > 0)
    def _():
        words = jnp.concatenate([xs_ref[c] for c in range(SC_ROW_SPLIT)], axis=1)
        row = lax.broadcasted_iota(jnp.int32, (words.shape[0], 1), 0)
        words = jnp.where(row < nv_ref[i], words, jnp.uint32(0))
        hi, lo = _unpack_rows(words)
        xb = jnp.concatenate([hi.astype(BF16), lo.astype(BF16)], axis=1)
        chunks = [slice(c, c + MXU_WIDTH) for c in range(0, D_EXPERT, MXU_WIDTH)]
        gu = [(jnp.dot(xb, wg_s[:, c], preferred_element_type=F32),
               jnp.dot(xb, wu_s[:, c], preferred_element_type=F32)) for c in chunks]
        y = None
        for c, (g, u) in zip(chunks, gu):
            hid = (g * jax.nn.sigmoid(g) * u).astype(BF16)
            part = jnp.dot(hid, wd_s[c, :], preferred_element_type=F32)
            y = part if y is None else y + part
        packed = _pack_rows(y)
        for c in range(SC_ROW_SPLIT):
            yb_ref[c] = packed[:, c * SC_ROW_WORDS:(c + 1) * SC_ROW_WORDS]


def _expert_call(block_expert, block_valid, xs, layer, w_gate, w_up, w_down):
    p_total = xs.shape[1]
    blk = MOE_BLOCK
    wmap = lambda i, be, nv: (layer, be[i], 0, 0)
    rows = lambda i, be, nv: (0, i, 0)
    return pl.pallas_call(
        _expert_kernel,
        grid_spec=pltpu.PrefetchScalarGridSpec(
            num_scalar_prefetch=2,
            grid=(p_total // blk,),
            in_specs=[
                pl.BlockSpec((SC_ROW_SPLIT, blk, SC_ROW_WORDS), rows),
                pl.BlockSpec((1, 1, D_MODEL, D_EXPERT), wmap),
                pl.BlockSpec((1, 1, D_MODEL, D_EXPERT), wmap),
                pl.BlockSpec((1, 1, D_EXPERT, D_MODEL), wmap),
            ],
            out_specs=pl.BlockSpec((SC_ROW_SPLIT, blk, SC_ROW_WORDS), rows),
            scratch_shapes=[
                pltpu.VMEM((D_MODEL, D_EXPERT), BF16),
                pltpu.VMEM((D_MODEL, D_EXPERT), BF16),
                pltpu.VMEM((D_EXPERT, D_MODEL), BF16),
            ],
        ),
        out_shape=jax.ShapeDtypeStruct((SC_ROW_SPLIT, p_total, SC_ROW_WORDS), jnp.uint32),
        compiler_params=_cparams(("arbitrary",)),
        name="moe_experts",
    )(block_expert, block_valid, xs, w_gate, w_up, w_down)


def _combined_rows(y_ref, rg_ref, piece_refs):
    rg = rg_ref[...]
    his, los = [], []
    for c in range(SC_ROW_SPLIT):
        hi0, lo0 = _unpack_rows(piece_refs[c * TOP_K][...])
        hi1, lo1 = _unpack_rows(piece_refs[c * TOP_K + 1][...])
        his.append(rg[:, 0:1] * hi0 + rg[:, 1:2] * hi1)
        los.append(rg[:, 0:1] * lo0 + rg[:, 1:2] * lo1)
    return y_ref[...] + jnp.concatenate(his + los, axis=1)


def _combine_kernel(y_ref, rg_ref, *refs):
    refs[-1][...] = _combined_rows(y_ref, rg_ref, refs[:-1])


def _combine_call(y2d, rg, rows):
    n = y2d.shape[0]
    tm = 2 * ROW_TILE
    tiles = n // tm
    return pl.pallas_call(
        _combine_kernel,
        grid=(tiles,),
        in_specs=[
            pl.BlockSpec((tm, D_MODEL), lambda i: (i, 0)),
            pl.BlockSpec((tm, LANES), lambda i: (i, 0)),
        ] + [
            pl.BlockSpec((tm, SC_ROW_WORDS), functools.partial(lambda i, j: (i + j * tiles, 0), j=j))
            for j in range(SC_ROW_SPLIT * TOP_K)
        ],
        out_specs=pl.BlockSpec((tm, D_MODEL), lambda i: (i, 0)),
        out_shape=jax.ShapeDtypeStruct((n, D_MODEL), F32),
        compiler_params=_cparams(("parallel",)),
        name="moe_combine",
    )(y2d, rg, *([rows] * (SC_ROW_SPLIT * TOP_K)))


def _slot_tables(ri, n):
    e = ri[:TOP_K]
    experts = jnp.arange(N_EXPERTS, dtype=jnp.int32)[None, :, None]
    onehot = (e[:, None, :] == experts).astype(jnp.int32)
    tok_cnt = onehot.sum(axis=0)
    t = 256
    cnt_t = tok_cnt.reshape(N_EXPERTS, n // t, t)
    tri = (jnp.arange(t)[:, None] < jnp.arange(t)[None, :]).astype(F32)
    within = jnp.einsum("ens,st->ent", cnt_t.astype(F32), tri).astype(jnp.int32)
    tile_sum = cnt_t.sum(axis=2)
    tile_base = jnp.cumsum(tile_sum, axis=1) - tile_sum
    rank = (within + tile_base[:, :, None]).reshape(N_EXPERTS, n)
    counts = tile_sum.sum(axis=1)
    padded = ((counts + MOE_BLOCK - 1) // MOE_BLOCK) * MOE_BLOCK
    pad_end = jnp.cumsum(padded)
    pad_start = pad_end - padded
    slot = ((rank + pad_start[:, None])[None, :, :] * onehot).sum(axis=1)
    n_blocks = (n * TOP_K) // MOE_BLOCK + N_EXPERTS
    block_start = jnp.arange(n_blocks, dtype=jnp.int32) * MOE_BLOCK
    block_expert = jnp.minimum((pad_end[None, :] <= block_start[:, None]).sum(axis=-1),
                               N_EXPERTS - 1).astype(jnp.int32)
    owner = block_expert[:, None] == jnp.arange(N_EXPERTS, dtype=jnp.int32)[None, :]
    used_end = jnp.where(owner, (pad_start + counts)[None, :], 0).sum(axis=-1)
    block_valid = jnp.clip(used_end - block_start, 0, MOE_BLOCK).astype(jnp.int32)
    p_total = n_blocks * MOE_BLOCK
    slot_km = slot.reshape(1, TOP_K * n).astype(jnp.int32)
    idx = jnp.concatenate([slot_km + c * p_total for c in range(SC_ROW_SPLIT)], axis=1)
    return idx, block_expert, block_valid, p_total


def _after(value, other):
    if other is None:
        return value
    return lax.optimization_barrier((value, other))[0]


def _moe_rows(n, ri, hp, layer, experts, after=None):
    idx, block_expert, block_valid, p_total = _slot_tables(ri, n)
    xs = _sc_scatter_rows(hp.reshape(SC_ROW_SPLIT * n, SC_ROW_WORDS), idx, n // SC_WINDOW,
                          SC_ROW_SPLIT * p_total)
    yb = _expert_call(block_expert, _after(block_valid, after),
                      xs.reshape(SC_ROW_SPLIT, p_total, SC_ROW_WORDS), layer, *experts)
    return _sc_gather_rows(yb.reshape(SC_ROW_SPLIT * p_total, SC_ROW_WORDS), idx), yb


def _rope_tables(seq):
    pos = np.arange(seq, dtype=np.float32)
    inv_freq = np.float32(ROPE_THETA) ** (-np.arange(0, ROPE_DIM, 2, dtype=np.float32) / np.float32(ROPE_DIM))
    ang = pos[:, None] * inv_freq[None, :].astype(np.float32)
    cos = np.cos(ang).astype(np.float32)
    sin = np.sin(ang).astype(np.float32)
    half = ROPE_DIM // 2
    rest = HEAD_DIM - ROPE_DIM
    zeros = lambda w: np.zeros((seq, w), np.float32)
    cos_h = np.concatenate([cos, cos, np.ones((seq, rest), np.float32)], axis=1)
    sa_h = np.concatenate([-sin, zeros(half + rest)], axis=1)
    sb_h = np.concatenate([zeros(half), sin, zeros(rest)], axis=1)
    pair = lambda t: jnp.asarray(np.tile(t, (1, LANES // HEAD_DIM)))
    return pair(cos_h), pair(sa_h), pair(sb_h)


def _router_weights(w_rg, b_rg, w_re, b_re):
    w = jnp.concatenate([w_rg, w_re], axis=1)
    w = jnp.pad(w, ((0, 0), (0, LANES - w.shape[1])))
    w_hi = w.astype(BF16)
    w_lo = (w - w_hi.astype(F32)).astype(BF16)
    b = jnp.pad(jnp.concatenate([b_rg, b_re]), (0, LANES - N_EXPERT_GROUPS - N_EXPERTS))
    return jnp.concatenate([w_hi, w_lo], axis=1), b.reshape(1, LANES)


def _attention_layer(x, p):
    batch, seq, d = x.shape
    x2d = x.reshape(batch * seq, d)
    cos, sa, sb = _rope_tables(seq)
    qkv = _qkv_call(x2d, seq, p["ln_mix0"], p["w_qkv"], p["mseg"], p["qg"], p["kg"], cos, sa, sb)
    os_, lses = [], []
    for g, dil in enumerate(DILATIONS):
        o, lse = _attn_call(qkv[g], qkv[3 + g], qkv[6 + g], batch, seq, dil)
        os_.append(o)
        lses.append(lse)
    return _attn_out_call(os_, lses, x2d, p["w_o"], p["expand"], p["ln_ffn0"], p["wr0"], p["br0"])


def _conv_layer(state, rows, seq, p, after=None):
    y, _, rg, _ = state
    return _conv_call(y, _after(rg, after), rows, seq, p["ln_mix1"], p["w_pw1"], p["b_pw1"], p["w_dw"],
                      p["b_dw"], p["conv_ln_g"], p["conv_ln_b"], p["w_pw2"], p["b_pw2"],
                      p["ln_ffn1"], p["wr1"], p["br1"])


def _experts_of(state, layer, p, after=None):
    y, ri, _, hp = state
    return _moe_rows(y.shape[0], ri, hp, layer, p["experts"], after)


def _encoder(xs, p):
    small, big = sorted(range(len(xs)), key=lambda i: xs[i].shape[0] * xs[i].shape[1])
    seq = {i: xs[i].shape[1] for i in (small, big)}
    state = {i: _attention_layer(xs[i], p) for i in (big, small)}
    rows, yb = {}, {}
    rows[big], yb[big] = _experts_of(state[big], 0, p, after=state[small][1])
    rows[small], yb[small] = _experts_of(state[small], 0, p)
    state[big] = _conv_layer(state[big], rows[big], seq[big], p, after=yb[small])
    state[small] = _conv_layer(state[small], rows[small], seq[small], p)
    rows[big], _ = _experts_of(state[big], 1, p, after=state[small][1])
    rows[small], _ = _experts_of(state[small], 1, p)
    outs = {i: _combine_call(state[i][0], state[i][2], rows[i]).reshape(xs[i].shape) for i in (big, small)}
    return tuple(outs[i] for i in range(len(xs)))


def kernel(x_prompt, x_sample, ln_mix, ln_ffn, w_qkv, q_gain, k_gain, w_o, w_pw1, b_pw1, w_dw, b_dw,
           conv_ln_g, conv_ln_b, w_pw2, b_pw2, w_router_group, b_router_group, w_router_expert,
           b_router_expert, w_gate, w_up, w_down):
    row = lambda v: v.reshape(1, -1).astype(F32)
    n_heads = N_GROUPS * HEADS_PER_GROUP
    head_id = np.arange(ATTN_WIDTH) // HEAD_DIM
    mseg = jnp.asarray((head_id[:, None] == head_id[None, :]) / HEAD_DIM, BF16)
    slot_id = np.arange(GROUP_WIDTH) // HEAD_DIM
    expand = jnp.asarray(np.arange(LANES)[:, None] == slot_id[None, :], BF16)
    wr0, br0 = _router_weights(w_router_group[0], b_router_group[0], w_router_expert[0], b_router_expert[0])
    wr1, br1 = _router_weights(w_router_group[1], b_router_group[1], w_router_expert[1], b_router_expert[1])
    p = dict(
        ln_mix0=row(ln_mix[0]), ln_mix1=row(ln_mix[1]), ln_ffn0=row(ln_ffn[0]), ln_ffn1=row(ln_ffn[1]),
        w_qkv=w_qkv[0].astype(BF16), mseg=mseg, expand=expand,
        qg=row(jnp.tile(q_gain[0], n_heads) * (HEAD_DIM ** -0.5 * LOG2E)), kg=row(jnp.tile(k_gain[0], n_heads)),
        w_o=w_o[0].astype(BF16),
        w_pw1=w_pw1[0].astype(BF16), b_pw1=row(b_pw1[0]), w_dw=w_dw[0].astype(F32), b_dw=row(b_dw[0]),
        conv_ln_g=row(conv_ln_g[0]), conv_ln_b=row(conv_ln_b[0]),
        w_pw2=w_pw2[0].astype(BF16), b_pw2=row(b_pw2[0]),
        wr0=wr0, br0=br0, wr1=wr1, br1=br1,
        experts=(w_gate, w_up, w_down),
    )
    return _encoder((x_prompt, x_sample), p)
```
